```python
import math
import jax, jax.numpy as jnp
from jax import lax
import numpy as np

D_MODEL = 1024
BATCH = 32
SEQ = 2048
DEPTH = 1
DEC_BATCH = 128
DEC_SEQ = 1
PAST_LEN = 8192
PAGE_SIZE = 128

ATT_HEADS = 8
ATT_KV_HEADS = 4
ATT_GROUP = ATT_HEADS // ATT_KV_HEADS
ATT_DHALF = 64
ATT_DV = 2 * ATT_DHALF
Q_BLOCK = 128
HG_HEADS = 8
HG_DK = 128
HG_DV = 128
HG_CHUNK = 32
N_GROUPS = 4
EXP_PER_GROUP = 8
N_EXPERTS = N_GROUPS * EXP_PER_GROUP
TOP_K_INNER = 2
D_FF_EXPERT = 512
EPS = 1e-6

Q_COLS = ATT_HEADS * 2 * ATT_DHALF
K_COLS = ATT_KV_HEADS * 2 * ATT_DHALF
V_COLS = ATT_KV_HEADS * ATT_DV
ATT_OUT = ATT_HEADS * ATT_DV
HG_WIDTH = HG_HEADS * HG_DK
HG_V_WIDTH = HG_HEADS * HG_DV
SPLITS = (Q_COLS, K_COLS, V_COLS, HG_WIDTH, HG_WIDTH, HG_V_WIDTH, HG_V_WIDTH, D_MODEL, D_MODEL)
IN_COLS = Q_COLS + K_COLS + V_COLS + 2 * HG_WIDTH + 2 * HG_V_WIDTH + 2 * D_MODEL

kernel_name = 'hybrid_diffattn_hgrn2_hmoe_step'


def rms_norm(x, g):
    xf = x.astype(jnp.float32)
    y = xf * lax.rsqrt(jnp.mean(xf * xf, axis=-1, keepdims=True) + EPS)
    return (y * g.astype(jnp.float32)).astype(x.dtype)


def _split(z):
    outs = []
    off = 0
    for w in SPLITS:
        outs.append(z[..., off:off + w])
        off += w
    return outs


def _lambda_init(layer):
    return 0.8 - 0.6 * math.exp(-0.3 * layer)


def _diff_lambda(lam_p, lam_init):
    lp = lam_p.astype(jnp.float32)
    return jnp.exp(jnp.sum(lp[0] * lp[1])) - jnp.exp(jnp.sum(lp[2] * lp[3])) + lam_init


def _attn_qkv(zq, zk, zv, gq, gk):
    B, T = zq.shape[:2]
    q = rms_norm(zq.reshape(B, T, ATT_KV_HEADS, ATT_GROUP, 2, ATT_DHALF), gq) * (ATT_DHALF ** -0.5)
    k = rms_norm(zk.reshape(B, T, ATT_KV_HEADS, 2, ATT_DHALF), gk)
    v = zv.reshape(B, T, ATT_KV_HEADS, ATT_DV)
    return q, k, v


def _diff_attend(q, segs, lam):
    scores = []
    for k, v, mask in segs:
        s = jnp.einsum('btngcd,bsncd->bngcts', q, k, preferred_element_type=jnp.float32)
        if mask is not None:
            s = jnp.where(mask, s, -jnp.inf)
        scores.append(s)
    p = jax.nn.softmax(jnp.concatenate(scores, axis=-1), axis=-1)
    w = p[:, :, :, 0] - lam * p[:, :, :, 1]
    outs = []
    off = 0
    for k, v, _ in segs:
        S = k.shape[1]
        outs.append(jnp.einsum('bngts,bsne->btnge', w[..., off:off + S], v))
        off += S
    return sum(outs)


def _attn_prompt(q, k, v, lam):
    B, T = q.shape[:2]
    nb = T // Q_BLOCK
    qb = q.reshape((B, nb, Q_BLOCK) + q.shape[2:]).swapaxes(0, 1)
    kpos = jnp.arange(T)

    def one(args):
        qi, start = args
        qpos = start + jnp.arange(Q_BLOCK)
        mask = kpos[None, :] <= qpos[:, None]
        return _diff_attend(qi, [(k, v, mask)], lam)

    ob = lax.map(one, (qb, jnp.arange(nb) * Q_BLOCK))
    return ob.swapaxes(0, 1).reshape(B, T, ATT_KV_HEADS, ATT_GROUP, ATT_DV)


def _attn_sample_core(k_cache, v_cache, page_table):
    def core(q, k, v, lam):
        DB, DS = q.shape[:2]
        k_past = k_cache[page_table].reshape(DB, -1, ATT_KV_HEADS, 2, ATT_DHALF)
        v_past = v_cache[page_table].reshape(DB, -1, ATT_KV_HEADS, ATT_DV)
        tri = jnp.tril(jnp.ones((DS, DS), dtype=bool))
        return _diff_attend(q, [(k_past, v_past, None), (k, v, tri)], lam)
    return core


def _hgrn_inputs(hq, hf, hi, lb):
    B, T, _ = hq.shape

    def heads(a, d):
        return a.astype(jnp.float32).reshape(B, T, HG_HEADS, d).transpose(0, 2, 1, 3)

    zf = hf.astype(jnp.float32)
    f = lb + (1.0 - lb) * jax.nn.sigmoid(zf)
    logf = jnp.log(f)
    kk = (1.0 - lb) * jax.nn.sigmoid(-zf)
    return heads(hq, HG_DK) * (HG_DK ** -0.5), heads(kk, HG_DK), heads(hi, HG_DV), heads(logf, HG_DK)


def _gla_chunk(S, q, k, v, g):
    C = q.shape[2]
    b = jnp.cumsum(g, axis=2)
    causal = jnp.tril(jnp.ones((C, C), dtype=bool))
    dlt = b[:, :, :, None, :] - b[:, :, None, :, :]
    decay = jnp.exp(jnp.where(causal[:, :, None], dlt, -jnp.inf))
    A = jnp.einsum('bhtk,bhsk,bhtsk->bhts', q, k, decay)
    o = jnp.einsum('bhtk,bhkv->bhtv', q * jnp.exp(b), S) + jnp.einsum('bhts,bhsv->bhtv', A, v)
    b_last = b[:, :, -1:, :]
    S_new = S * jnp.exp(b_last)[:, :, 0, :, None] + jnp.einsum('bhsk,bhsv->bhkv', k * jnp.exp(b_last - b), v)
    return S_new, o


def _hgrn_prompt(q, k, v, g):
    B, H, T, _ = q.shape
    nc = T // HG_CHUNK

    def chunks(a):
        return a.reshape(B, H, nc, HG_CHUNK, a.shape[-1]).transpose(2, 0, 1, 3, 4)

    S0 = jnp.zeros((B, H, HG_DK, HG_DV), jnp.float32)
    S, o = lax.scan(lambda S, xs: _gla_chunk(S, *xs), S0, (chunks(q), chunks(k), chunks(v), chunks(g)))
    o = o.transpose(1, 2, 0, 3, 4).reshape(B, H, T, HG_DV)
    return o, S


def _hgrn_sample_core(S_cache):
    def core(q, k, v, g):
        S_new, o = _gla_chunk(S_cache.astype(jnp.float32), q, k, v, g)
        return o, S_new
    return core


def _moe(h, w_rg, b_rg, w_re, b_re, w_eg, w_eu, w_ed):
    shp = h.shape
    t = h.reshape(-1, D_MODEL)
    n = t.shape[0]
    pg = jax.nn.softmax((t @ w_rg + b_rg).astype(jnp.float32), axis=-1)
    pg_top, gsel = lax.top_k(pg, 1)
    le = (t @ w_re + b_re).astype(jnp.float32).reshape(n, N_GROUPS, EXP_PER_GROUP)
    le_sel = jnp.take_along_axis(le, gsel[:, :, None], axis=1)[:, 0]
    pe_top, eidx = lax.top_k(jax.nn.softmax(le_sel, axis=-1), TOP_K_INNER)
    wts = pe_top / jnp.sum(pe_top, axis=-1, keepdims=True) * pg_top
    ids = gsel * EXP_PER_GROUP + eidx
    comb = jnp.einsum('nk,nke->ne', wts, jax.nn.one_hot(ids, N_EXPERTS, dtype=jnp.float32))

    def body(acc, xs):
        wg, wu, wd, c = xs
        hid = jax.nn.silu(t @ wg) * (t @ wu)
        return acc + (hid * c[:, None]) @ wd, None

    acc, _ = lax.scan(body, jnp.zeros((n, D_MODEL), jnp.float32), (w_eg, w_eu, w_ed, comb.T))
    return acc.astype(h.dtype).reshape(shp)


def _layer(x, attn_core, hg_core, lam_init, lb, rms_in, w_in, q_norm, k_norm, lam_p, sub_norm,
           hg_norm, w_pa, w_ph, w_o, rms_ffn, w_rg, b_rg, w_re, b_re, w_eg, w_eu, w_ed):
    B, T, _ = x.shape
    h = rms_norm(x, rms_in)
    zq, zk, zv, hq, hf, hi, hgt, ga, gh = _split(h @ w_in)
    q, k, v = _attn_qkv(zq, zk, zv, q_norm, k_norm)
    lam = _diff_lambda(lam_p, lam_init)
    oa = rms_norm(attn_core(q, k, v, lam), sub_norm) * (1.0 - lam_init)
    ya = oa.reshape(B, T, ATT_OUT).astype(x.dtype) @ w_pa
    qh, kh, vh, gl = _hgrn_inputs(hq, hf, hi, lb)
    oh, s_new = hg_core(qh, kh, vh, gl)
    oh = rms_norm(oh.transpose(0, 2, 1, 3), hg_norm) * jax.nn.silu(hgt.astype(jnp.float32)).reshape(B, T, HG_HEADS, HG_DV)
    yh = oh.reshape(B, T, HG_V_WIDTH).astype(x.dtype) @ w_ph
    mixed = (jax.nn.sigmoid(ga.astype(jnp.float32)) * ya + jax.nn.sigmoid(gh.astype(jnp.float32)) * yh).astype(x.dtype)
    x1 = x + mixed @ w_o
    y = x1 + _moe(rms_norm(x1, rms_ffn), w_rg, b_rg, w_re, b_re, w_eg, w_eu, w_ed)
    return y, k, v, s_new


def setup_inputs(seed: int = 0) -> dict:
    key = jax.random.key(seed)
    ks = jax.random.split(key, 26)

    def nrm(k, shape, scale):
        return jax.random.normal(k, shape, jnp.float32) * scale

    def gain(k, shape):
        return 1.0 + nrm(k, shape, 0.01)

    n_pages = PAST_LEN // PAGE_SIZE
    n_used = DEC_BATCH * n_pages
    n_pool = n_used + max(1, n_used // 4)
    perm = jax.random.permutation(ks[5], n_pool)
    page_table = perm[:n_used].reshape(DEC_BATCH, n_pages).astype(jnp.int32)
    return {
        'x_prompt': nrm(ks[0], (BATCH, SEQ, D_MODEL), 1.0),
        'x_sample': nrm(ks[1], (DEC_BATCH, DEC_SEQ, D_MODEL), 1.0),
        'cache_k': nrm(ks[2], (DEPTH, n_pool, PAGE_SIZE, ATT_KV_HEADS, 2, ATT_DHALF), 1.0),
        'cache_v': nrm(ks[3], (DEPTH, n_pool, PAGE_SIZE, ATT_KV_HEADS, ATT_DV), 1.0),
        'state_hgrn': nrm(ks[4], (DEPTH, DEC_BATCH, HG_HEADS, HG_DK, HG_DV), 0.5),
        'page_table': page_table,
        'rms_in': gain(ks[6], (DEPTH, D_MODEL)),
        'w_in': nrm(ks[7], (DEPTH, D_MODEL, IN_COLS), D_MODEL ** -0.5),
        'att_q_norm': gain(ks[8], (DEPTH, ATT_DHALF)),
        'att_k_norm': gain(ks[9], (DEPTH, ATT_DHALF)),
        'att_lambda': nrm(ks[10], (DEPTH, 4, ATT_DHALF), 0.1),
        'att_sub_norm': gain(ks[11], (DEPTH, ATT_DV)),
        'hg_lower_bound': nrm(ks[12], (DEPTH + 1, HG_WIDTH), 0.5),
        'hg_out_norm': gain(ks[13], (DEPTH, HG_DV)),
        'w_branch_att': nrm(ks[14], (DEPTH, ATT_OUT, D_MODEL), ATT_OUT ** -0.5),
        'w_branch_hg': nrm(ks[15], (DEPTH, HG_V_WIDTH, D_MODEL), HG_V_WIDTH ** -0.5),
        'w_out': nrm(ks[16], (DEPTH, D_MODEL, D_MODEL), D_MODEL ** -0.5),
        'rms_ffn': gain(ks[17], (DEPTH, D_MODEL)),
        'w_router_group': nrm(ks[18], (DEPTH, D_MODEL, N_GROUPS), D_MODEL ** -0.5),
        'b_router_group': nrm(ks[19], (DEPTH, N_GROUPS), 0.01),
        'w_router_expert': nrm(ks[20], (DEPTH, D_MODEL, N_EXPERTS), D_MODEL ** -0.5),
        'b_router_expert': nrm(ks[21], (DEPTH, N_EXPERTS), 0.01),
        'w_exp_gate': nrm(ks[22], (DEPTH, N_EXPERTS, D_MODEL, D_FF_EXPERT), D_MODEL ** -0.5),
        'w_exp_up': nrm(ks[23], (DEPTH, N_EXPERTS, D_MODEL, D_FF_EXPERT), D_MODEL ** -0.5),
        'w_exp_down': nrm(ks[24], (DEPTH, N_EXPERTS, D_FF_EXPERT, D_MODEL), D_FF_EXPERT ** -0.5),
    }


def reference(x_prompt, x_sample, cache_k, cache_v, state_hgrn, page_table, rms_in, w_in, att_q_norm,
              att_k_norm, att_lambda, att_sub_norm, hg_lower_bound, hg_out_norm, w_branch_att, w_branch_hg,
              w_out, rms_ffn, w_router_group, b_router_group, w_router_expert, b_router_expert,
              w_exp_gate, w_exp_up, w_exp_down):
    lb_all = jnp.cumsum(jax.nn.softmax(hg_lower_bound.astype(jnp.float32), axis=0), axis=0)
    yp, ys = x_prompt, x_sample
    kp_l, vp_l, sp_l, ks_l, vs_l, ss_l = [], [], [], [], [], []
    for l in range(DEPTH):
        lam_init = _lambda_init(l)
        lb = lb_all[l]
        wts = (rms_in[l], w_in[l], att_q_norm[l], att_k_norm[l], att_lambda[l], att_sub_norm[l],
               hg_out_norm[l], w_branch_att[l], w_branch_hg[l], w_out[l], rms_ffn[l],
               w_router_group[l], b_router_group[l], w_router_expert[l], b_router_expert[l],
               w_exp_gate[l], w_exp_up[l], w_exp_down[l])
        yp, kp, vp, sp = _layer(yp, _attn_prompt, _hgrn_prompt, lam_init, lb, *wts)
        ys, kn, vn, sn = _layer(ys, _attn_sample_core(cache_k[l], cache_v[l], page_table),
                                _hgrn_sample_core(state_hgrn[l]), lam_init, lb, *wts)
        kp_l.append(kp)
        vp_l.append(vp)
        sp_l.append(sp)
        ks_l.append(kn)
        vs_l.append(vn)
        ss_l.append(sn)
    k_prompt = jnp.stack(kp_l)
    v_prompt = jnp.stack(vp_l)
    state_prompt = jnp.stack(sp_l)
    k_sample = jnp.stack(ks_l)
    v_sample = jnp.stack(vs_l)
    state_sample = jnp.stack(ss_l)
    return (yp, ys, k_prompt, v_prompt, state_prompt, k_sample, v_sample, state_sample)
```

```python
import functools
import math

import numpy as np
import jax
import jax.numpy as jnp
from jax import lax
from jax.experimental import pallas as pl
from jax.experimental.pallas import tpu as pltpu

F32 = jnp.float32
BF16 = jnp.bfloat16

D_MODEL = 1024
ATT_HEADS = 8
ATT_KV_HEADS = 4
ATT_DHALF = 64
ATT_DV = 128
HG_HEADS = 8
HG_DK = 128
HG_DV = 128
N_GROUPS = 4
EXP_PER_GROUP = 8
N_EXPERTS = 32
D_FF = 512
EPS = 1e-6
IN_COLS = 8192
LANES = 128
VMEM_LIMIT = 56 * 1024 * 1024

HG_CHUNK = 64
HG_LEVELS = 6


def _dot(a, b):
    return jnp.dot(a, b, preferred_element_type=F32)


def _dot_nt(a, b):
    return lax.dot_general(a, b, (((1,), (1,)), ((), ())), preferred_element_type=F32)


def _dot_tn(a, b):
    return lax.dot_general(a, b, (((0,), (0,)), ((), ())), preferred_element_type=F32)


def _split3(x):
    hi = x.astype(BF16)
    r1 = x - hi.astype(F32)
    mid = r1.astype(BF16)
    lo = (r1 - mid.astype(F32)).astype(BF16)
    return hi, mid, lo


def _seg_rms(z, seg, gain_row, scale):
    outs = []
    for i in range(z.shape[1] // LANES):
        zi = z[:, i * LANES:(i + 1) * LANES]
        z2 = zi * zi
        hi = z2.astype(BF16)
        lo = (z2 - hi.astype(F32)).astype(BF16)
        ss = _dot(hi, seg) + _dot(lo, seg)
        y = zi * lax.rsqrt(ss * (1.0 / ATT_DHALF) + EPS) * gain_row
        if scale != 1.0:
            y = y * scale
        outs.append(y)
    return jnp.concatenate(outs, axis=1)


def _inproj_kernel(x_ref, g_ref, w_ref, qn_ref, kn_ref, lbp_ref, seg_ref,
                   q_ref, k_ref, v_ref, hq_ref, logf_ref, kk_ref, hv_ref, og_ref, sga_ref, sgh_ref,
                   h_scr):
    j = pl.program_id(1)

    @pl.when(j == 0)
    def _():
        x = x_ref[...]
        ms = jnp.mean(x * x, axis=-1, keepdims=True)
        h_scr[...] = (x * lax.rsqrt(ms + EPS) * g_ref[...]).astype(BF16)

    z = _dot(h_scr[...], w_ref[...])

    @pl.when(j == 0)
    def _():
        q_ref[...] = _seg_rms(z, seg_ref[...], qn_ref[...], ATT_DHALF ** -0.5).astype(BF16)

    @pl.when(j == 1)
    def _():
        k_ref[...] = _seg_rms(z[:, :512], seg_ref[...], kn_ref[...], 1.0)
        v_ref[...] = z[:, 512:]

    @pl.when(j == 2)
    def _():
        hq_ref[...] = (z * (HG_DK ** -0.5)).astype(BF16)

    @pl.when(j == 3)
    def _():
        lbp = lbp_ref[...]
        m = jnp.max(lbp, axis=0, keepdims=True)
        e = jnp.exp(lbp - m)
        lb = e[0:1] / jnp.sum(e, axis=0, keepdims=True)
        f = lb + (1.0 - lb) * jax.nn.sigmoid(z)
        logf_ref[...] = jnp.log(f)
        kk_ref[...] = (1.0 - lb) * jax.nn.sigmoid(-z)

    @pl.when(j == 4)
    def _():
        hv_ref[...] = z.astype(BF16)

    @pl.when(j == 5)
    def _():
        og_ref[...] = z * jax.nn.sigmoid(z)

    @pl.when(j == 6)
    def _():
        sga_ref[...] = jax.nn.sigmoid(z)

    @pl.when(j == 7)
    def _():
        sgh_ref[...] = jax.nn.sigmoid(z)


def _inproj(x, rms_in, w_in_bf, qn_row, kn_row, lbp, seg, tm):
    n = x.shape[0]
    tok = lambda w: pl.BlockSpec((tm, w), lambda i, j: (i, 0))
    const = lambda s: pl.BlockSpec(s, lambda i, j: (0, 0))
    out_shape = (
        jax.ShapeDtypeStruct((n, 1024), BF16),
        jax.ShapeDtypeStruct((n, 512), F32),
        jax.ShapeDtypeStruct((n, 512), F32),
        jax.ShapeDtypeStruct((n, 1024), BF16),
        jax.ShapeDtypeStruct((n, 1024), F32),
        jax.ShapeDtypeStruct((n, 1024), F32),
        jax.ShapeDtypeStruct((n, 1024), BF16),
        jax.ShapeDtypeStruct((n, 1024), F32),
        jax.ShapeDtypeStruct((n, 1024), F32),
        jax.ShapeDtypeStruct((n, 1024), F32),
    )
    out_specs = (tok(1024), tok(512), tok(512), tok(1024), tok(1024), tok(1024), tok(1024),
                 tok(1024), tok(1024), tok(1024))
    return pl.pallas_call(
        _inproj_kernel,
        grid=(n // tm, IN_COLS // 1024),
        in_specs=[tok(1024), const((1, 1024)),
                  pl.BlockSpec((1024, 1024), lambda i, j: (0, j)),
                  const((1, LANES)), const((1, LANES)), const((2, 1024)), const((LANES, LANES))],
        out_specs=out_specs,
        out_shape=out_shape,
        scratch_shapes=[pltpu.VMEM((tm, 1024), BF16)],
        compiler_params=pltpu.CompilerParams(
            dimension_semantics=("parallel", "arbitrary"), vmem_limit_bytes=VMEM_LIMIT),
        name="inproj",
    )(x, rms_in, w_in_bf, qn_row, kn_row, lbp, seg)


def _diff_lambda(lp, lam_init):
    a = jnp.sum(lp[0:1] * lp[1:2], axis=-1, keepdims=True)
    b = jnp.sum(lp[2:3] * lp[3:4], axis=-1, keepdims=True)
    return jnp.exp(a) - jnp.exp(b) + lam_init


def _sub_norm(o, sub_row, lam_init):
    ms = jnp.mean(o * o, axis=-1, keepdims=True)
    return o * lax.rsqrt(ms + EPS) * sub_row * (1.0 - lam_init)


def _attn_kernel(qi_tab, ki_tab, q_ref, k_ref, v_ref, lamp_ref, sub_ref, o_ref,
                 qs_scr, m_scr, l_scr, acc_scr, *, tq, tk, lam_init):
    p = pl.program_id(2)
    qi = qi_tab[p]
    ki = ki_tab[p]

    @pl.when(ki == 0)
    def _():
        q = q_ref[0].astype(F32)
        lane = lax.broadcasted_iota(jnp.int32, (tq, LANES), 1)
        for g in range(2):
            qg = q[:, g * LANES:(g + 1) * LANES]
            for c in range(2):
                keep = (lane < ATT_DHALF) if c == 0 else (lane >= ATT_DHALF)
                r = g * 2 + c
                qs_scr[r * tq:(r + 1) * tq, :] = jnp.where(keep, qg, 0.0).astype(BF16)
        m_scr[...] = jnp.full(m_scr.shape, -jnp.inf, F32)
        l_scr[...] = jnp.zeros(l_scr.shape, F32)
        acc_scr[...] = jnp.zeros(acc_scr.shape, F32)

    k = k_ref[0].astype(BF16)
    v = v_ref[0].astype(BF16)
    s = _dot_nt(qs_scr[...], k)
    qpos = qi * tq + lax.rem(lax.broadcasted_iota(jnp.int32, s.shape, 0), tq)
    kpos = ki * tk + lax.broadcasted_iota(jnp.int32, s.shape, 1)
    s = jnp.where(kpos <= qpos, s, -jnp.inf)
    m_prev = m_scr[...]
    m_new = jnp.maximum(m_prev, jnp.max(s, axis=-1, keepdims=True))
    alpha = jnp.exp(m_prev - m_new)
    pr = jnp.exp(s - m_new)
    l_scr[...] = alpha * l_scr[...] + jnp.sum(pr, axis=-1, keepdims=True)
    acc_scr[...] = alpha * acc_scr[...] + _dot(pr.astype(BF16), v)
    m_scr[...] = m_new

    @pl.when(ki == ((qi + 1) * tq - 1) // tk)
    def _():
        lam = _diff_lambda(lamp_ref[...], lam_init)
        out = acc_scr[...] / l_scr[...]
        for g in range(2):
            o0 = out[(2 * g) * tq:(2 * g + 1) * tq]
            o1 = out[(2 * g + 1) * tq:(2 * g + 2) * tq]
            o = o0 - lam * o1
            o_ref[0, :, g * LANES:(g + 1) * LANES] = _sub_norm(o, sub_ref[...], lam_init).astype(BF16)


def _attn_prompt(q, k, v, lam_p, sub_row, lam_init, tq, tk):
    b, t, _ = q.shape
    pairs = [(qi, ki) for qi in range(t // tq) for ki in range(((qi + 1) * tq - 1) // tk + 1)]
    qi_tab = jnp.asarray([p[0] for p in pairs], jnp.int32)
    ki_tab = jnp.asarray([p[1] for p in pairs], jnp.int32)
    grid_spec = pltpu.PrefetchScalarGridSpec(
        num_scalar_prefetch=2,
        grid=(b, ATT_KV_HEADS, len(pairs)),
        in_specs=[
            pl.BlockSpec((1, tq, 256), lambda bi, n, p, qt, kt: (bi, qt[p], n)),
            pl.BlockSpec((1, tk, LANES), lambda bi, n, p, qt, kt: (bi, kt[p], n)),
            pl.BlockSpec((1, tk, LANES), lambda bi, n, p, qt, kt: (bi, kt[p], n)),
            pl.BlockSpec((4, ATT_DHALF), lambda bi, n, p, qt, kt: (0, 0)),
            pl.BlockSpec((1, LANES), lambda bi, n, p, qt, kt: (0, 0)),
        ],
        out_specs=pl.BlockSpec((1, tq, 256), lambda bi, n, p, qt, kt: (bi, qt[p], n)),
        scratch_shapes=[pltpu.VMEM((4 * tq, LANES), BF16), pltpu.VMEM((4 * tq, 1), F32),
                        pltpu.VMEM((4 * tq, 1), F32), pltpu.VMEM((4 * tq, LANES), F32)],
    )
    return pl.pallas_call(
        functools.partial(_attn_kernel, tq=tq, tk=tk, lam_init=lam_init),
        grid_spec=grid_spec,
        out_shape=jax.ShapeDtypeStruct((b, t, 1024), BF16),
        compiler_params=pltpu.CompilerParams(
            dimension_semantics=("parallel", "parallel", "arbitrary"), vmem_limit_bytes=VMEM_LIMIT),
        name="attn_prompt",
    )(qi_tab, ki_tab, q, k, v, lam_p, sub_row)


def _hgrn_tables(c, levels):
    t = np.arange(c)[:, None]
    u = np.arange(c)[None, :]
    w = [(u <= t), (u > t)]
    masks = [(t == u)]
    for l in range(1, levels + 1):
        n = 1 << l
        half = n // 2
        mid = (t // n) * n + half
        hi = t >= mid
        w.append(np.where(hi, (u > mid) & (u <= t), (u > t) & (u <= mid)))
        masks.append((t // n == u // n) & (t % n >= half) & (u % n < half))
    w = np.concatenate(w, axis=0).astype(np.float32)
    masks = np.stack(masks).astype(np.float32)
    return w, masks


def _hgrn_kernel(q_ref, g_ref, k_ref, v_ref, og_ref, norm_ref, w_ref, mask_ref, o_ref, s_ref,
                 st_scr, *, c, levels, n_chunks):
    ti = pl.program_id(2)

    @pl.when(ti == 0)
    def _():
        st_scr[...] = jnp.zeros(st_scr.shape, F32)

    w = w_ref[...]
    st = st_scr[...]
    for ci in range(n_chunks):
        rows = slice(ci * c, (ci + 1) * c)
        q = q_ref[0, rows, :].astype(F32)
        g = g_ref[0, rows, :]
        k = k_ref[0, rows, :]
        v = v_ref[0, rows, :]
        hi, mid, lo = _split3(g)
        e = jnp.exp(_dot(w, hi) + _dot(w, mid) + _dot(w, lo))
        e_b = e[0:c]
        e_k = e[c:2 * c]
        a = mask_ref[0] * _dot_nt(q.astype(BF16), k.astype(BF16))
        for l in range(1, levels + 1):
            e_l = e[(l + 1) * c:(l + 2) * c]
            a = a + mask_ref[l] * _dot_nt((q * e_l).astype(BF16), (k * e_l).astype(BF16))
        o = _dot_nt((q * e_b).astype(BF16), st.astype(BF16)) + _dot(a.astype(BF16), v)
        st = st * e_b[c - 1:c, :] + _dot_tn(v, (k * e_k).astype(BF16))
        ms = jnp.mean(o * o, axis=-1, keepdims=True)
        y = o * lax.rsqrt(ms + EPS) * norm_ref[...] * og_ref[0, rows, :]
        o_ref[0, rows, :] = y.astype(BF16)
    st_scr[...] = st

    @pl.when(ti == pl.num_programs(2) - 1)
    def _():
        s_ref[0, 0] = st.T


def _hgrn_prompt(hq, logf, kk, hv, og, norm_row, ct):
    b, t, _ = hq.shape
    c, levels = HG_CHUNK, HG_LEVELS
    w_np, m_np = _hgrn_tables(c, levels)
    w = jnp.asarray(w_np, BF16)
    masks = jnp.asarray(m_np, F32)
    tok = pl.BlockSpec((1, ct, LANES), lambda bi, h, ti: (bi, ti, h))
    return pl.pallas_call(
        functools.partial(_hgrn_kernel, c=c, levels=levels, n_chunks=ct // c),
        grid=(b, HG_HEADS, t // ct),
        in_specs=[tok, tok, tok, tok, tok,
                  pl.BlockSpec((1, LANES), lambda bi, h, ti: (0, 0)),
                  pl.BlockSpec(w.shape, lambda bi, h, ti: (0, 0)),
                  pl.BlockSpec(masks.shape, lambda bi, h, ti: (0, 0, 0))],
        out_specs=(tok, pl.BlockSpec((1, 1, HG_DK, HG_DV), lambda bi, h, ti: (bi, h, 0, 0))),
        out_shape=(jax.ShapeDtypeStruct((b, t, 1024), BF16),
                   jax.ShapeDtypeStruct((b, HG_HEADS, HG_DK, HG_DV), F32)),
        scratch_shapes=[pltpu.VMEM((HG_DV, HG_DK), F32)],
        compiler_params=pltpu.CompilerParams(
            dimension_semantics=("parallel", "parallel", "arbitrary"), vmem_limit_bytes=VMEM_LIMIT),
        name="hgrn_prompt",
    )(hq, logf, kk, hv, og, norm_row, w, masks)


def _dec_attn_kernel(pt_ref, wq_ref, kn_ref, vn_ref, lamp_ref, sub_ref, *refs, pages, lam_init):
    k_refs = refs[:pages]
    v_refs = refs[pages:2 * pages]
    o_ref = refs[2 * pages]
    m_scr, l_scr, acc_scr = refs[2 * pages + 1:]
    j = pl.program_id(1)

    @pl.when(j == 0)
    def _():
        m_scr[...] = jnp.full(m_scr.shape, -jnp.inf, F32)
        l_scr[...] = jnp.zeros(l_scr.shape, F32)
        acc_scr[...] = jnp.zeros(acc_scr.shape, F32)

    wq = wq_ref[0]
    s = jnp.concatenate([_dot_nt(wq, k_refs[i][0].astype(BF16)) for i in range(pages)], axis=1)
    m_prev = m_scr[...]
    m_new = jnp.maximum(m_prev, jnp.max(s, axis=-1, keepdims=True))
    alpha = jnp.exp(m_prev - m_new)
    pr = jnp.exp(s - m_new)
    l_scr[...] = alpha * l_scr[...] + jnp.sum(pr, axis=-1, keepdims=True)
    pv = _dot(pr[:, 0:LANES].astype(BF16), v_refs[0][0].astype(BF16))
    for i in range(1, pages):
        pv = pv + _dot(pr[:, i * LANES:(i + 1) * LANES].astype(BF16), v_refs[i][0].astype(BF16))
    acc_scr[...] = alpha * acc_scr[...] + pv
    m_scr[...] = m_new

    @pl.when(j == pl.num_programs(1) - 1)
    def _():
        s_cur = jnp.sum(wq.astype(F32) * kn_ref[0], axis=-1, keepdims=True)
        m_old = m_scr[...]
        m_fin = jnp.maximum(m_old, s_cur)
        al = jnp.exp(m_old - m_fin)
        pc = jnp.exp(s_cur - m_fin)
        l_fin = al * l_scr[...] + pc
        out = (al * acc_scr[...] + pc * vn_ref[0]) / l_fin
        rown = lax.rem(lax.broadcasted_iota(jnp.int32, (16, LANES), 0), 8) // 2
        sel = jnp.zeros((16, LANES), F32)
        for n in range(ATT_KV_HEADS):
            sel = sel + jnp.where(rown == n, out[:, n * LANES:(n + 1) * LANES], 0.0)
        lam = _diff_lambda(lamp_ref[...], lam_init)
        o = sel[0:8] - lam * sel[8:16]
        o_ref[0] = _sub_norm(o, sub_ref[...], lam_init)


def _attn_sample(wq, k_new, v_new, cache_k, cache_v, page_table, lam_p, sub_row, lam_init, pages):
    db, n_pages = page_table.shape
    page = cache_k.shape[1]
    pt_flat = page_table.reshape(-1)

    def page_spec(i):
        return pl.BlockSpec((1, page, 512), lambda bi, j, pt, i=i: (pt[bi * n_pages + j * pages + i], 0, 0))

    row3 = lambda w: pl.BlockSpec((1, 1, w), lambda bi, j, pt: (bi, 0, 0))
    grid_spec = pltpu.PrefetchScalarGridSpec(
        num_scalar_prefetch=1,
        grid=(db, n_pages // pages),
        in_specs=[pl.BlockSpec((1, 16, 512), lambda bi, j, pt: (bi, 0, 0)), row3(512), row3(512),
                  pl.BlockSpec((4, ATT_DHALF), lambda bi, j, pt: (0, 0)),
                  pl.BlockSpec((1, LANES), lambda bi, j, pt: (0, 0))]
                 + [page_spec(i) for i in range(pages)] + [page_spec(i) for i in range(pages)],
        out_specs=pl.BlockSpec((1, 8, LANES), lambda bi, j, pt: (bi, 0, 0)),
        scratch_shapes=[pltpu.VMEM((16, 1), F32), pltpu.VMEM((16, 1), F32), pltpu.VMEM((16, 512), F32)],
    )
    return pl.pallas_call(
        functools.partial(_dec_attn_kernel, pages=pages, lam_init=lam_init),
        grid_spec=grid_spec,
        out_shape=jax.ShapeDtypeStruct((db, 8, LANES), F32),
        compiler_params=pltpu.CompilerParams(
            dimension_semantics=("parallel", "arbitrary"), vmem_limit_bytes=VMEM_LIMIT),
        name="attn_sample",
    )(pt_flat, wq, k_new, v_new, lam_p, sub_row, *([cache_k] * pages), *([cache_v] * pages))


def _hgrn_step_kernel(s_ref, q_ref, g_ref, k_ref, v_ref, og_ref, norm_ref, o_ref, sn_ref):
    for h in range(HG_HEADS):
        s = s_ref[0, h]
        f = jnp.exp(g_ref[0, h])
        qc = q_ref[0, h]
        kc = k_ref[0, h]
        v = v_ref[0, h]
        o = jnp.sum(s * (qc * f), axis=0, keepdims=True) + jnp.sum(qc * kc, axis=0, keepdims=True) * v
        sn_ref[0, h] = s * f + kc * v
        ms = jnp.mean(o * o, axis=-1, keepdims=True)
        o_ref[0, h] = o * lax.rsqrt(ms + EPS) * norm_ref[...] * og_ref[0, h]


def _hgrn_sample(state, qcol, gcol, kcol, vrow, ogrow, norm_row):
    db = state.shape[0]
    col = pl.BlockSpec((1, HG_HEADS, HG_DK, 1), lambda bi: (bi, 0, 0, 0))
    row = pl.BlockSpec((1, HG_HEADS, 1, HG_DV), lambda bi: (bi, 0, 0, 0))
    st = pl.BlockSpec((1, HG_HEADS, HG_DK, HG_DV), lambda bi: (bi, 0, 0, 0))
    return pl.pallas_call(
        _hgrn_step_kernel,
        grid=(db,),
        in_specs=[st, col, col, col, row, row, pl.BlockSpec((1, LANES), lambda bi: (0, 0))],
        out_specs=(row, st),
        out_shape=(jax.ShapeDtypeStruct((db, HG_HEADS, 1, HG_DV), F32),
                   jax.ShapeDtypeStruct(state.shape, F32)),
        compiler_params=pltpu.CompilerParams(
            dimension_semantics=("parallel",), vmem_limit_bytes=VMEM_LIMIT),
        name="hgrn_sample",
    )(state, qcol, gcol, kcol, vrow, ogrow, norm_row)


def _lane_first(cond, lane):
    return jnp.min(jnp.where(cond, lane, LANES), axis=-1, keepdims=True)


def _route(logits):
    lane = lax.broadcasted_iota(jnp.int32, logits.shape, 1)
    gmask = (lane >= N_EXPERTS) & (lane < N_EXPERTS + N_GROUPS)
    lg = jnp.where(gmask, logits, -jnp.inf)
    mg = jnp.max(lg, axis=-1, keepdims=True)
    eg = jnp.exp(lg - mg)
    pg = eg / jnp.sum(eg, axis=-1, keepdims=True)
    pg_top = jnp.max(pg, axis=-1, keepdims=True)
    gsel = _lane_first(gmask & (pg == pg_top), lane) - N_EXPERTS
    emask = (lane >= gsel * EXP_PER_GROUP) & (lane < (gsel + 1) * EXP_PER_GROUP)
    le = jnp.where(emask, logits, -jnp.inf)
    me = jnp.max(le, axis=-1, keepdims=True)
    ee = jnp.exp(le - me)
    pe = ee / jnp.sum(ee, axis=-1, keepdims=True)
    pe = jnp.where(emask, pe, -1.0)
    p1 = jnp.max(pe, axis=-1, keepdims=True)
    e1 = _lane_first(pe == p1, lane)
    pe2 = jnp.where(lane == e1, -1.0, pe)
    p2 = jnp.max(pe2, axis=-1, keepdims=True)
    e2 = _lane_first(pe2 == p2, lane)
    den = p1 + p2
    w1 = p1 / den * pg_top
    w2 = p2 / den * pg_top
    return jnp.where(lane == e1, w1, 0.0) + jnp.where(lane == e2, w2, 0.0)


def _merge_kernel(oa_ref, oh_ref, sga_ref, sgh_ref, x_ref, wpa_ref, wph_ref, wo_ref, gf_ref,
                  wr_ref, br_ref, x1_ref, xn_ref, comb_ref):
    ya = _dot(oa_ref[...], wpa_ref[...])
    yh = _dot(oh_ref[...], wph_ref[...])
    mixed = (sga_ref[...] * ya + sgh_ref[...] * yh).astype(BF16)
    x1 = x_ref[...] + _dot(mixed, wo_ref[...])
    x1_ref[...] = x1
    ms = jnp.mean(x1 * x1, axis=-1, keepdims=True)
    xn = (x1 * lax.rsqrt(ms + EPS) * gf_ref[...]).astype(BF16)
    xn_ref[...] = xn
    comb_ref[...] = _route(_dot(xn, wr_ref[...]) + br_ref[...])


def _merge(oa, oh, sga, sgh, x, wpa, wph, wo, gf, wr, br, tm):
    n = x.shape[0]
    tok = pl.BlockSpec((tm, 1024), lambda i: (i, 0))
    wsp = pl.BlockSpec((1024, 1024), lambda i: (0, 0))
    return pl.pallas_call(
        _merge_kernel,
        grid=(n // tm,),
        in_specs=[tok, tok, tok, tok, tok, wsp, wsp, wsp,
                  pl.BlockSpec((1, 1024), lambda i: (0, 0)),
                  pl.BlockSpec((1024, LANES), lambda i: (0, 0)),
                  pl.BlockSpec((1, LANES), lambda i: (0, 0))],
        out_specs=(tok, tok, pl.BlockSpec((tm, LANES), lambda i: (i, 0))),
        out_shape=(jax.ShapeDtypeStruct((n, 1024), F32), jax.ShapeDtypeStruct((n, 1024), BF16),
                   jax.ShapeDtypeStruct((n, LANES), F32)),
        compiler_params=pltpu.CompilerParams(
            dimension_semantics=("parallel",), vmem_limit_bytes=VMEM_LIMIT),
        name="merge",
    )(oa, oh, sga, sgh, x, wpa, wph, wo, gf, wr, br)


def _moe_kernel(xn_ref, comb_ref, x1_ref, wg_ref, wu_ref, wd_ref, y_ref, acc_scr):
    e = pl.program_id(1)

    @pl.when(e == 0)
    def _():
        acc_scr[...] = jnp.zeros(acc_scr.shape, F32)

    comb = comb_ref[...]
    lane = lax.broadcasted_iota(jnp.int32, comb.shape, 1)
    c = jnp.sum(jnp.where(lane == e, comb, 0.0), axis=-1, keepdims=True)
    xn = xn_ref[...]
    hg = _dot(xn, wg_ref[0])
    hu = _dot(xn, wu_ref[0])
    hid = hg * jax.nn.sigmoid(hg) * hu
    acc_scr[...] += _dot((hid * c).astype(BF16), wd_ref[0])

    @pl.when(e == pl.num_programs(1) - 1)
    def _():
        y_ref[...] = x1_ref[...] + acc_scr[...]


def _moe(xn, comb, x1, wg, wu, wd, tm):
    n = xn.shape[0]
    return pl.pallas_call(
        _moe_kernel,
        grid=(n // tm, N_EXPERTS),
        in_specs=[pl.BlockSpec((tm, 1024), lambda i, e: (i, 0)),
                  pl.BlockSpec((tm, LANES), lambda i, e: (i, 0)),
                  pl.BlockSpec((tm, 1024), lambda i, e: (i, 0)),
                  pl.BlockSpec((1, 1024, D_FF), lambda i, e: (e, 0, 0)),
                  pl.BlockSpec((1, 1024, D_FF), lambda i, e: (e, 0, 0)),
                  pl.BlockSpec((1, D_FF, 1024), lambda i, e: (e, 0, 0))],
        out_specs=pl.BlockSpec((tm, 1024), lambda i, e: (i, 0)),
        out_shape=jax.ShapeDtypeStruct((n, 1024), F32),
        scratch_shapes=[pltpu.VMEM((tm, 1024), F32)],
        compiler_params=pltpu.CompilerParams(
            dimension_semantics=("parallel", "arbitrary"), vmem_limit_bytes=VMEM_LIMIT),
        name="moe",
    )(xn, comb, x1, wg, wu, wd)


def _pick(n, pref):
    t = min(n, pref)
    while n % t:
        t //= 2
    return t


def kernel(x_prompt, x_sample, cache_k, cache_v, state_hgrn, page_table, rms_in, w_in, att_q_norm, att_k_norm, att_lambda, att_sub_norm, hg_lower_bound, hg_out_norm, w_branch_att, w_branch_hg, w_out, rms_ffn, w_router_group, b_router_group, w_router_expert, b_router_expert, w_exp_gate, w_exp_up, w_exp_down):
    depth = rms_in.shape[0]
    assert depth == 1 and hg_lower_bound.shape[0] == 2
    b, t, _ = x_prompt.shape
    db, ds, _ = x_sample.shape
    assert ds == 1
    lam_init = 0.8 - 0.6 * math.exp(-0.3 * 0)

    w_in_bf = w_in[0].astype(BF16)
    qn_row = jnp.tile(att_q_norm[0], 2).reshape(1, LANES)
    kn_row = jnp.tile(att_k_norm[0], 2).reshape(1, LANES)
    seg_np = (np.arange(LANES)[:, None] // ATT_DHALF) == (np.arange(LANES)[None, :] // ATT_DHALF)
    seg = jnp.asarray(seg_np.astype(np.float32), BF16)
    sub_row = att_sub_norm[0].reshape(1, LANES)
    hgn_row = hg_out_norm[0].reshape(1, LANES)
    wpa = w_branch_att[0].astype(BF16)
    wph = w_branch_hg[0].astype(BF16)
    wo = w_out[0].astype(BF16)
    wr = jnp.zeros((D_MODEL, LANES), F32)
    wr = wr.at[:, :N_EXPERTS].set(w_router_expert[0]).at[:, N_EXPERTS:N_EXPERTS + N_GROUPS].set(w_router_group[0])
    wr = wr.astype(BF16)
    br = jnp.zeros((1, LANES), F32)
    br = br.at[0, :N_EXPERTS].set(b_router_expert[0]).at[0, N_EXPERTS:N_EXPERTS + N_GROUPS].set(b_router_group[0])
    weg = w_exp_gate[0].astype(BF16)
    weu = w_exp_up[0].astype(BF16)
    wed = w_exp_down[0].astype(BF16)
    gin = rms_in[0].reshape(1, D_MODEL)
    gffn = rms_ffn[0].reshape(1, D_MODEL)
    lam_p = att_lambda[0]

    def tail(x2, oa, oh, sga, sgh, tm_merge, tm_moe):
        x1, xn, comb = _merge(oa, oh, sga, sgh, x2, wpa, wph, wo, gffn, wr, br, tm_merge)
        return _moe(xn, comb, x1, weg, weu, wed, tm_moe)

    n = b * t
    xp = x_prompt.reshape(n, D_MODEL)
    q, k, v, hq, logf, kk, hv, og, sga, sgh = _inproj(
        xp, gin, w_in_bf, qn_row, kn_row, hg_lower_bound, seg, _pick(n, 256))
    tq = _pick(t, 256)
    oa = _attn_prompt(q.reshape(b, t, 1024), k.reshape(b, t, 512), v.reshape(b, t, 512),
                      lam_p, sub_row, lam_init, tq, tq)
    r3 = lambda a: a.reshape(b, t, 1024)
    oh, state_p = _hgrn_prompt(r3(hq), r3(logf), r3(kk), r3(hv), r3(og), hgn_row, _pick(t, 256))
    yp = tail(xp, oa.reshape(n, 1024), oh.reshape(n, 1024), sga, sgh, _pick(n, 512), _pick(n, 1024))

    xs = x_sample.reshape(db, D_MODEL)
    q, ks, vs, hq, logf, kk, hv, og, sga, sgh = _inproj(
        xs, gin, w_in_bf, qn_row, kn_row, hg_lower_bound, seg, _pick(db, 256))
    q5 = q.reshape(db, ATT_KV_HEADS, 2, 2, ATT_DHALF)
    eye_n = jnp.eye(ATT_KV_HEADS, dtype=BF16)
    eye_c = jnp.eye(2, dtype=BF16)
    wq = jnp.einsum('bngcd,nm,ce->bcngmed', q5, eye_n, eye_c).reshape(db, 16, 512)
    n_pool = cache_k.shape[1]
    page = cache_k.shape[2]
    oa_s = _attn_sample(wq, ks.reshape(db, 1, 512), vs.reshape(db, 1, 512),
                        cache_k[0].reshape(n_pool, page, 512), cache_v[0].reshape(n_pool, page, 512),
                        page_table, lam_p, sub_row, lam_init, _pick(page_table.shape[1], 8))
    col = lambda a: a.astype(F32).reshape(db, HG_HEADS, HG_DK, 1)
    row = lambda a: a.astype(F32).reshape(db, HG_HEADS, 1, HG_DV)
    oh_s, state_s = _hgrn_sample(state_hgrn[0], col(hq), col(logf), col(kk), row(hv), row(og), hgn_row)
    ys = tail(xs, oa_s.reshape(db, 1024).astype(BF16), oh_s.reshape(db, 1024).astype(BF16), sga, sgh,
              _pick(db, 512), _pick(db, 1024))

    return (yp.reshape(b, t, D_MODEL), ys.reshape(db, 1, D_MODEL),
            k.reshape(1, b, t, ATT_KV_HEADS, 2, ATT_DHALF), v.reshape(1, b, t, ATT_KV_HEADS, ATT_DV),
            state_p.reshape(1, b, HG_HEADS, HG_DK, HG_DV),
            ks.reshape(1, db, 1, ATT_KV_HEADS, 2, ATT_DHALF), vs.reshape(1, db, 1, ATT_KV_HEADS, ATT_DV),
            state_s.reshape(1, db, HG_HEADS, HG_DK, HG_DV))
```

```python
import functools
import math

import numpy as np
import jax
import jax.numpy as jnp
from jax import lax
from jax.experimental import pallas as pl
from jax.experimental.pallas import tpu as pltpu

F32 = jnp.float32
BF16 = jnp.bfloat16

D_MODEL = 1024
ATT_HEADS = 8
ATT_KV_HEADS = 4
ATT_DHALF = 64
ATT_DV = 128
HG_HEADS = 8
HG_DK = 128
HG_DV = 128
N_GROUPS = 4
EXP_PER_GROUP = 8
N_EXPERTS = 32
D_FF = 512
EPS = 1e-6
LANES = 128
VMEM_LIMIT = 56 * 1024 * 1024

LOG2E = 1.4426950408889634
Q_SCALE = (ATT_DHALF ** -0.5) * LOG2E

HG_CHUNK = 64
HG_LEVELS = 6


def _dot(a, b):
    return jnp.dot(a, b, preferred_element_type=F32)


def _dot_nt(a, b):
    return lax.dot_general(a, b, (((1,), (1,)), ((), ())), preferred_element_type=F32)


def _dot_tn(a, b):
    return lax.dot_general(a, b, (((0,), (0,)), ((), ())), preferred_element_type=F32)


def _split3(x):
    hi = x.astype(BF16)
    r1 = x - hi.astype(F32)
    mid = r1.astype(BF16)
    lo = (r1 - mid.astype(F32)).astype(BF16)
    return hi, mid, lo


def _seg_rms(z, seg, gain_row, scale):
    outs = []
    for i in range(z.shape[1] // LANES):
        zi = z[:, i * LANES:(i + 1) * LANES]
        z2 = zi * zi
        hi = z2.astype(BF16)
        lo = (z2 - hi.astype(F32)).astype(BF16)
        ss = _dot(hi, seg) + _dot(lo, seg)
        y = zi * lax.rsqrt(ss * (1.0 / ATT_DHALF) + EPS) * gain_row
        if scale != 1.0:
            y = y * scale
        outs.append(y)
    return jnp.concatenate(outs, axis=1)


def _params(*sem):
    return pltpu.CompilerParams(dimension_semantics=sem, vmem_limit_bytes=VMEM_LIMIT)


def _rms_kernel(x_ref, g_ref, h_ref):
    x = x_ref[...]
    ms = jnp.mean(x * x, axis=-1, keepdims=True)
    h_ref[...] = (x * lax.rsqrt(ms + EPS) * g_ref[...]).astype(BF16)


def _rms_cast(x, g, tm):
    n = x.shape[0]
    return pl.pallas_call(
        _rms_kernel,
        grid=(n // tm,),
        in_specs=[pl.BlockSpec((tm, D_MODEL), lambda i: (i, 0)), pl.BlockSpec((1, D_MODEL), lambda i: (0, 0))],
        out_specs=pl.BlockSpec((tm, D_MODEL), lambda i: (i, 0)),
        out_shape=jax.ShapeDtypeStruct((n, D_MODEL), BF16),
        compiler_params=_params("parallel"),
        name="rms_in",
    )(x, g)


def _proj_kernel(h_ref, w_ref, *refs, mode, n_aux):
    aux, outs = refs[:n_aux], refs[n_aux:]
    z = _dot(h_ref[...], w_ref[...])
    if mode == "q":
        outs[0][...] = _seg_rms(z, aux[0][...], aux[1][...], Q_SCALE).astype(BF16)
    elif mode in ("kv", "kv_t"):
        k = _seg_rms(z[:, :512], aux[0][...], aux[1][...], 1.0)
        if mode == "kv_t":
            outs[0][0] = k.T
        else:
            outs[0][...] = k
        outs[1][...] = z[:, 512:]
    elif mode == "hq":
        outs[0][...] = (z * (HG_DK ** -0.5)).astype(BF16)
    elif mode == "hf":
        lbp = aux[0][...]
        m = jnp.max(lbp, axis=0, keepdims=True)
        e = jnp.exp(lbp - m)
        lb = e[0:1] / jnp.sum(e, axis=0, keepdims=True)
        f = lb + (1.0 - lb) * jax.nn.sigmoid(z)
        outs[0][...] = jnp.log(f)
        outs[1][...] = (1.0 - lb) * jax.nn.sigmoid(-z)
    elif mode == "hv":
        outs[0][...] = z.astype(BF16)
    elif mode == "silu":
        outs[0][...] = z * jax.nn.sigmoid(z)
    elif mode == "sigmoid":
        outs[0][...] = jax.nn.sigmoid(z)


def _proj(h, w, col, mode, aux, outs, tm, name):
    n = h.shape[0]
    aux_specs = [pl.BlockSpec(a.shape, lambda i: (0, 0)) for a in aux]
    res = pl.pallas_call(
        functools.partial(_proj_kernel, mode=mode, n_aux=len(aux)),
        grid=(n // tm,),
        in_specs=[pl.BlockSpec((tm, D_MODEL), lambda i: (i, 0)),
                  pl.BlockSpec((D_MODEL, 1024), lambda i, col=col: (0, col))] + aux_specs,
        out_specs=tuple(pl.BlockSpec(o[2], o[3]) for o in outs),
        out_shape=tuple(jax.ShapeDtypeStruct(o[0], o[1]) for o in outs),
        compiler_params=_params("parallel"),
        name=name,
    )(h, w, *aux)
    return res


def _inproj(x, gin, w, qn_row, kn_row, lbp, seg, tm, seq):
    n = x.shape[0]
    h = _rms_cast(x, gin, tm)
    tok = lambda width, dt: ((n, width), dt, (tm, width), lambda i: (i, 0))
    q, = _proj(h, w, 0, "q", [seg, qn_row], [tok(1024, BF16)], tm, "proj_q")
    if seq is None:
        k, v = _proj(h, w, 1, "kv", [seg, kn_row], [tok(512, F32), tok(512, F32)], tm, "proj_kv")
    else:
        b, t = seq
        tpb = t // tm
        k_out = ((b, 512, t), F32, (1, 512, tm), lambda i: (i // tpb, 0, i % tpb))
        k, v = _proj(h, w, 1, "kv_t", [seg, kn_row], [k_out, tok(512, F32)], tm, "proj_kv")
    hq, = _proj(h, w, 2, "hq", [], [tok(1024, BF16)], tm, "proj_hq")
    logf, kk = _proj(h, w, 3, "hf", [lbp], [tok(1024, F32), tok(1024, F32)], tm, "proj_hf")
    hv, = _proj(h, w, 4, "hv", [], [tok(1024, BF16)], tm, "proj_hv")
    og, = _proj(h, w, 5, "silu", [], [tok(1024, F32)], tm, "proj_og")
    sga, = _proj(h, w, 6, "sigmoid", [], [tok(1024, F32)], tm, "proj_ga")
    sgh, = _proj(h, w, 7, "sigmoid", [], [tok(1024, F32)], tm, "proj_gh")
    return q, k, v, hq, logf, kk, hv, og, sga, sgh


def _diff_lambda(lp, lam_init):
    a = jnp.sum(lp[0:1] * lp[1:2], axis=-1, keepdims=True)
    b = jnp.sum(lp[2:3] * lp[3:4], axis=-1, keepdims=True)
    return jnp.exp(a) - jnp.exp(b) + lam_init


def _sub_norm(o, sub_row, lam_init):
    ms = jnp.mean(o * o, axis=-1, keepdims=True)
    return o * lax.rsqrt(ms + EPS) * sub_row * (1.0 - lam_init)


def _attn_kernel(q_ref, kt_ref, v_ref, lamp_ref, sub_ref, o_ref,
                 kb_scr, vb_scr, qs_scr, m_scr, acc_scr, *, tq, lam_init):
    qi = pl.program_id(2)

    @pl.when(qi == 0)
    def _():
        kb_scr[...] = kt_ref[0].astype(BF16)
        vb_scr[:, 0:LANES] = v_ref[0].astype(BF16)
        vb_scr[:, LANES:2 * LANES] = jnp.ones((vb_scr.shape[0], LANES), BF16)

    q = q_ref[0].astype(F32)
    lane = lax.broadcasted_iota(jnp.int32, (tq, LANES), 1)
    for g in range(2):
        qg = q[:, g * LANES:(g + 1) * LANES]
        for c in range(2):
            keep = (lane < ATT_DHALF) if c == 0 else (lane >= ATT_DHALF)
            r = g * 2 + c
            qs_scr[r * tq:(r + 1) * tq, :] = jnp.where(keep, qg, 0.0).astype(BF16)
    m_scr[...] = jnp.full(m_scr.shape, -jnp.inf, F32)
    acc_scr[...] = jnp.zeros(acc_scr.shape, F32)

    def chunk(j, masked):
        off = pl.multiple_of(j * tq, tq)
        s = _dot(qs_scr[...], kb_scr[:, pl.ds(off, tq)])
        if masked:
            row = lax.rem(lax.broadcasted_iota(jnp.int32, s.shape, 0), tq)
            s = jnp.where(lax.broadcasted_iota(jnp.int32, s.shape, 1) <= row, s, -jnp.inf)
        m_prev = m_scr[...]
        m_new = jnp.maximum(m_prev, jnp.max(s, axis=-1, keepdims=True))
        alpha = jnp.exp2(m_prev - m_new)
        pr = jnp.exp2(s - pltpu.repeat(m_new, tq // LANES, axis=1))
        pv = _dot(pr.astype(BF16), vb_scr[pl.ds(off, tq), :])
        acc_scr[...] = pltpu.repeat(alpha, 2, axis=1) * acc_scr[...] + pv
        m_scr[...] = m_new

    def body(j, carry):
        chunk(j, False)
        return carry

    lax.fori_loop(0, qi, body, 0)
    chunk(qi, True)

    lam = _diff_lambda(lamp_ref[...], lam_init)
    acc = acc_scr[...]
    out = acc[:, 0:LANES] / acc[:, LANES:2 * LANES]
    for g in range(2):
        o0 = out[(2 * g) * tq:(2 * g + 1) * tq]
        o1 = out[(2 * g + 1) * tq:(2 * g + 2) * tq]
        o_ref[0, :, g * LANES:(g + 1) * LANES] = _sub_norm(o0 - lam * o1, sub_ref[...], lam_init).astype(BF16)


def _attn_prompt(q, kt, v, lam_p, sub_row, lam_init, tq):
    b, t, _ = q.shape
    return pl.pallas_call(
        functools.partial(_attn_kernel, tq=tq, lam_init=lam_init),
        grid=(b, ATT_KV_HEADS, t // tq),
        in_specs=[
            pl.BlockSpec((1, tq, 256), lambda bi, n, qi: (bi, qi, n)),
            pl.BlockSpec((1, LANES, t), lambda bi, n, qi: (bi, n, 0)),
            pl.BlockSpec((1, t, LANES), lambda bi, n, qi: (bi, 0, n)),
            pl.BlockSpec((4, ATT_DHALF), lambda bi, n, qi: (0, 0)),
            pl.BlockSpec((1, LANES), lambda bi, n, qi: (0, 0)),
        ],
        out_specs=pl.BlockSpec((1, tq, 256), lambda bi, n, qi: (bi, qi, n)),
        out_shape=jax.ShapeDtypeStruct((b, t, 1024), BF16),
        scratch_shapes=[pltpu.VMEM((LANES, t), BF16), pltpu.VMEM((t, 2 * LANES), BF16),
                        pltpu.VMEM((4 * tq, LANES), BF16), pltpu.VMEM((4 * tq, LANES), F32),
                        pltpu.VMEM((4 * tq, 2 * LANES), F32)],
        compiler_params=_params("parallel", "parallel", "arbitrary"),
        name="attn_prompt",
    )(q, kt, v, lam_p, sub_row)


def _hgrn_tables(c, levels):
    t = np.arange(c)[:, None]
    u = np.arange(c)[None, :]
    w = [(u <= t), (u > t)]
    masks = [(t == u)]
    for l in range(1, levels + 1):
        n = 1 << l
        half = n // 2
        mid = (t // n) * n + half
        hi = t >= mid
        w.append(np.where(hi, (u > mid) & (u <= t), (u > t) & (u <= mid)))
        masks.append((t // n == u // n) & (t % n >= half) & (u % n < half))
    w = np.concatenate(w, axis=0).astype(np.float32)
    w3 = np.concatenate([w, w, w], axis=1)
    if len(masks) % 2:
        masks.append(np.zeros_like(masks[0]))
    pairs = [np.concatenate([masks[i], masks[i + 1]], axis=1) for i in range(0, len(masks), 2)]
    return w3, np.stack(pairs).astype(np.float32)


def _hgrn_kernel(q_ref, g_ref, k_ref, v_ref, og_ref, norm_ref, w_ref, mask_ref, o_ref, s_ref,
                 st_scr, *, c, levels, n_chunks):
    ti = pl.program_id(2)

    @pl.when(ti == 0)
    def _():
        st_scr[...] = jnp.zeros(st_scr.shape, F32)

    w = w_ref[...]
    st = st_scr[...]
    zeros = jnp.zeros((c, LANES), BF16)
    for ci in range(n_chunks):
        rows = slice(ci * c, (ci + 1) * c)
        q = q_ref[0, rows, :].astype(F32)
        g = g_ref[0, rows, :]
        k = k_ref[0, rows, :]
        v = v_ref[0, rows, :]
        e = jnp.exp(_dot(w, jnp.concatenate(_split3(g), axis=0)))
        e_b = e[0:c]
        e_k = e[c:2 * c]
        qs = [q.astype(BF16)]
        ks = [k.astype(BF16)]
        for l in range(1, levels + 1):
            e_l = e[(l + 1) * c:(l + 2) * c]
            qs.append((q * e_l).astype(BF16))
            ks.append((k * e_l).astype(BF16))
        if len(qs) % 2:
            qs.append(zeros)
            ks.append(zeros)
        a2 = None
        for p in range(len(qs) // 2):
            ql = jnp.concatenate([qs[2 * p], qs[2 * p + 1]], axis=1)
            kbd = jnp.concatenate([jnp.concatenate([ks[2 * p], zeros], axis=1),
                                   jnp.concatenate([zeros, ks[2 * p + 1]], axis=1)], axis=0)
            term = mask_ref[p] * _dot_nt(ql, kbd)
            a2 = term if a2 is None else a2 + term
        o = (_dot_nt((q * e_b).astype(BF16), st.astype(BF16))
             + _dot(a2.astype(BF16), jnp.concatenate([v, v], axis=0)))
        st = st * e_b[c - 1:c, :] + _dot_tn(v, (k * e_k).astype(BF16))
        ms = jnp.mean(o * o, axis=-1, keepdims=True)
        y = o * lax.rsqrt(ms + EPS) * norm_ref[...] * og_ref[0, rows, :]
        o_ref[0, rows, :] = y.astype(BF16)
    st_scr[...] = st

    @pl.when(ti == pl.num_programs(2) - 1)
    def _():
        s_ref[0, 0] = st.T


def _hgrn_prompt(hq, logf, kk, hv, og, norm_row, ct):
    b, t, _ = hq.shape
    c, levels = HG_CHUNK, HG_LEVELS
    w_np, m_np = _hgrn_tables(c, levels)
    w = jnp.asarray(w_np, BF16)
    masks = jnp.asarray(m_np, F32)
    tok = pl.BlockSpec((1, ct, LANES), lambda bi, h, ti: (bi, ti, h))
    return pl.pallas_call(
        functools.partial(_hgrn_kernel, c=c, levels=levels, n_chunks=ct // c),
        grid=(b, HG_HEADS, t // ct),
        in_specs=[tok, tok, tok, tok, tok,
                  pl.BlockSpec((1, LANES), lambda bi, h, ti: (0, 0)),
                  pl.BlockSpec(w.shape, lambda bi, h, ti: (0, 0)),
                  pl.BlockSpec(masks.shape, lambda bi, h, ti: (0, 0, 0))],
        out_specs=(tok, pl.BlockSpec((1, 1, HG_DK, HG_DV), lambda bi, h, ti: (bi, h, 0, 0))),
        out_shape=(jax.ShapeDtypeStruct((b, t, 1024), BF16),
                   jax.ShapeDtypeStruct((b, HG_HEADS, HG_DK, HG_DV), F32)),
        scratch_shapes=[pltpu.VMEM((HG_DV, HG_DK), F32)],
        compiler_params=_params("parallel", "parallel", "arbitrary"),
        name="hgrn_prompt",
    )(hq, logf, kk, hv, og, norm_row, w, masks)


def _dec_attn_kernel(pt_ref, wq_ref, kn_ref, vn_ref, lamp_ref, sub_ref, *refs, pages, lam_init):
    k_refs = refs[:pages]
    v_refs = refs[pages:2 * pages]
    o_ref = refs[2 * pages]
    m_scr, l_scr, acc_scr = refs[2 * pages + 1:]
    j = pl.program_id(1)
    page = k_refs[0].shape[2]

    @pl.when(j == 0)
    def _():
        m_scr[...] = jnp.full(m_scr.shape, -jnp.inf, F32)
        l_scr[...] = jnp.zeros(l_scr.shape, F32)
        acc_scr[...] = jnp.zeros(acc_scr.shape, F32)

    rown = lax.rem(lax.broadcasted_iota(jnp.int32, (16, LANES), 0), 8) // 2
    wq = wq_ref[0]
    s = jnp.concatenate([_dot(wq, k_refs[i][0].astype(BF16)) for i in range(pages)], axis=1)
    m_prev = m_scr[...]
    m_new = jnp.maximum(m_prev, jnp.max(s, axis=-1, keepdims=True))
    alpha = jnp.exp2(m_prev - m_new)
    pr = jnp.exp2(s - m_new)
    l_scr[...] = alpha * l_scr[...] + jnp.sum(pr, axis=-1, keepdims=True)
    prb = pr.astype(BF16)
    pv = jnp.zeros((16, LANES), F32)
    for n in range(ATT_KV_HEADS):
        v_n = jnp.concatenate([v_refs[i][0, pl.ds(n, page, stride=ATT_KV_HEADS), :] for i in range(pages)],
                              axis=0).astype(BF16)
        pv = pv + jnp.where(rown == n, _dot(prb, v_n), 0.0)
    acc_scr[...] = alpha * acc_scr[...] + pv
    m_scr[...] = m_new

    @pl.when(j == pl.num_programs(1) - 1)
    def _():
        s_cur = jnp.sum(wq.astype(F32) * kn_ref[0], axis=-1, keepdims=True)
        m_old = m_scr[...]
        m_fin = jnp.maximum(m_old, s_cur)
        al = jnp.exp2(m_old - m_fin)
        pc = jnp.exp2(s_cur - m_fin)
        l_fin = al * l_scr[...] + pc
        vn = vn_ref[0]
        v_cur = jnp.zeros((16, LANES), F32)
        for n in range(ATT_KV_HEADS):
            v_cur = v_cur + jnp.where(rown == n, vn[:, n * LANES:(n + 1) * LANES], 0.0)
        out = (al * acc_scr[...] + pc * v_cur) / l_fin
        lam = _diff_lambda(lamp_ref[...], lam_init)
        o = out[0:8] - lam * out[8:16]
        o_ref[0] = _sub_norm(o, sub_ref[...], lam_init)


def _attn_sample(wq, k_new, v_new, kt_pages, v_pages, page_table, lam_p, sub_row, lam_init, pages):
    db, n_pages = page_table.shape
    pt_flat = page_table.reshape(-1)
    blk = kt_pages.shape[1:]

    def page_spec(i):
        return pl.BlockSpec((1,) + blk, lambda bi, j, pt, i=i: (pt[bi * n_pages + j * pages + i], 0, 0))

    row3 = lambda w: pl.BlockSpec((1, 1, w), lambda bi, j, pt: (bi, 0, 0))
    grid_spec = pltpu.PrefetchScalarGridSpec(
        num_scalar_prefetch=1,
        grid=(db, n_pages // pages),
        in_specs=[pl.BlockSpec((1, 16, 512), lambda bi, j, pt: (bi, 0, 0)), row3(512), row3(512),
                  pl.BlockSpec((4, ATT_DHALF), lambda bi, j, pt: (0, 0)),
                  pl.BlockSpec((1, LANES), lambda bi, j, pt: (0, 0))]
                 + [page_spec(i) for i in range(pages)] + [page_spec(i) for i in range(pages)],
        out_specs=pl.BlockSpec((1, 8, LANES), lambda bi, j, pt: (bi, 0, 0)),
        scratch_shapes=[pltpu.VMEM((16, 1), F32), pltpu.VMEM((16, 1), F32), pltpu.VMEM((16, LANES), F32)],
    )
    return pl.pallas_call(
        functools.partial(_dec_attn_kernel, pages=pages, lam_init=lam_init),
        grid_spec=grid_spec,
        out_shape=jax.ShapeDtypeStruct((db, 8, LANES), F32),
        compiler_params=_params("parallel", "arbitrary"),
        name="attn_sample",
    )(pt_flat, wq, k_new, v_new, lam_p, sub_row, *([kt_pages] * pages), *([v_pages] * pages))


def _hgrn_step_kernel(s_ref, q_ref, g_ref, k_ref, v_ref, og_ref, norm_ref, o_ref, sn_ref):
    for h in range(HG_HEADS):
        s = s_ref[0, h]
        f = jnp.exp(g_ref[0, h])
        qc = q_ref[0, h]
        kc = k_ref[0, h]
        v = v_ref[0, h]
        o = jnp.sum(s * (qc * f), axis=0, keepdims=True) + jnp.sum(qc * kc, axis=0, keepdims=True) * v
        sn_ref[0, h] = s * f + kc * v
        ms = jnp.mean(o * o, axis=-1, keepdims=True)
        o_ref[0, h] = o * lax.rsqrt(ms + EPS) * norm_ref[...] * og_ref[0, h]


def _hgrn_sample(state, qcol, gcol, kcol, vrow, ogrow, norm_row):
    db = state.shape[0]
    col = pl.BlockSpec((1, HG_HEADS, HG_DK, 1), lambda bi: (bi, 0, 0, 0))
    row = pl.BlockSpec((1, HG_HEADS, 1, HG_DV), lambda bi: (bi, 0, 0, 0))
    st = pl.BlockSpec((1, HG_HEADS, HG_DK, HG_DV), lambda bi: (bi, 0, 0, 0))
    return pl.pallas_call(
        _hgrn_step_kernel,
        grid=(db,),
        in_specs=[st, col, col, col, row, row, pl.BlockSpec((1, LANES), lambda bi: (0, 0))],
        out_specs=(row, st),
        out_shape=(jax.ShapeDtypeStruct((db, HG_HEADS, 1, HG_DV), F32),
                   jax.ShapeDtypeStruct(state.shape, F32)),
        compiler_params=_params("parallel"),
        name="hgrn_sample",
    )(state, qcol, gcol, kcol, vrow, ogrow, norm_row)


def _lane_first(cond, lane):
    return jnp.min(jnp.where(cond, lane, LANES), axis=-1, keepdims=True)


def _route(logits):
    lane = lax.broadcasted_iota(jnp.int32, logits.shape, 1)
    gmask = (lane >= N_EXPERTS) & (lane < N_EXPERTS + N_GROUPS)
    lg = jnp.where(gmask, logits, -jnp.inf)
    mg = jnp.max(lg, axis=-1, keepdims=True)
    eg = jnp.exp(lg - mg)
    pg = eg / jnp.sum(eg, axis=-1, keepdims=True)
    pg_top = jnp.max(pg, axis=-1, keepdims=True)
    gsel = _lane_first(gmask & (pg == pg_top), lane) - N_EXPERTS
    emask = (lane >= gsel * EXP_PER_GROUP) & (lane < (gsel + 1) * EXP_PER_GROUP)
    le = jnp.where(emask, logits, -jnp.inf)
    me = jnp.max(le, axis=-1, keepdims=True)
    ee = jnp.exp(le - me)
    pe = ee / jnp.sum(ee, axis=-1, keepdims=True)
    pe = jnp.where(emask, pe, -1.0)
    p1 = jnp.max(pe, axis=-1, keepdims=True)
    e1 = _lane_first(pe == p1, lane)
    pe2 = jnp.where(lane == e1, -1.0, pe)
    p2 = jnp.max(pe2, axis=-1, keepdims=True)
    e2 = _lane_first(pe2 == p2, lane)
    den = p1 + p2
    w1 = p1 / den * pg_top
    w2 = p2 / den * pg_top
    return jnp.where(lane == e1, w1, 0.0) + jnp.where(lane == e2, w2, 0.0)


def _merge_kernel(oa_ref, oh_ref, sga_ref, sgh_ref, x_ref, wpa_ref, wph_ref, wo_ref, gf_ref,
                  wr_ref, br_ref, x1_ref, xn_ref, comb_ref):
    ya = _dot(oa_ref[...], wpa_ref[...])
    yh = _dot(oh_ref[...], wph_ref[...])
    mixed = (sga_ref[...] * ya + sgh_ref[...] * yh).astype(BF16)
    x1 = x_ref[...] + _dot(mixed, wo_ref[...])
    x1_ref[...] = x1
    ms = jnp.mean(x1 * x1, axis=-1, keepdims=True)
    xn = (x1 * lax.rsqrt(ms + EPS) * gf_ref[...]).astype(BF16)
    xn_ref[...] = xn
    comb_ref[...] = _route(_dot(xn, wr_ref[...]) + br_ref[...])


def _merge(oa, oh, sga, sgh, x, wpa, wph, wo, gf, wr, br, tm):
    n = x.shape[0]
    tok = pl.BlockSpec((tm, 1024), lambda i: (i, 0))
    wsp = pl.BlockSpec((1024, 1024), lambda i: (0, 0))
    return pl.pallas_call(
        _merge_kernel,
        grid=(n // tm,),
        in_specs=[tok, tok, tok, tok, tok, wsp, wsp, wsp,
                  pl.BlockSpec((1, 1024), lambda i: (0, 0)),
                  pl.BlockSpec((1024, LANES), lambda i: (0, 0)),
                  pl.BlockSpec((1, LANES), lambda i: (0, 0))],
        out_specs=(tok, tok, pl.BlockSpec((tm, LANES), lambda i: (i, 0))),
        out_shape=(jax.ShapeDtypeStruct((n, 1024), F32), jax.ShapeDtypeStruct((n, 1024), BF16),
                   jax.ShapeDtypeStruct((n, LANES), F32)),
        compiler_params=_params("parallel"),
        name="merge",
    )(oa, oh, sga, sgh, x, wpa, wph, wo, gf, wr, br)


def _moe_kernel(xn_ref, comb_ref, x1_ref, wg_ref, wu_ref, wd_ref, y_ref, acc_scr):
    e = pl.program_id(1)

    @pl.when(e == 0)
    def _():
        acc_scr[...] = jnp.zeros(acc_scr.shape, F32)

    comb = comb_ref[...]
    lane = lax.broadcasted_iota(jnp.int32, comb.shape, 1)
    c = jnp.sum(jnp.where(lane == e, comb, 0.0), axis=-1, keepdims=True)
    xn = xn_ref[...]
    hg = _dot(xn, wg_ref[0])
    hu = _dot(xn, wu_ref[0])
    hid = hg * jax.nn.sigmoid(hg) * hu
    acc_scr[...] += _dot((hid * c).astype(BF16), wd_ref[0])

    @pl.when(e == pl.num_programs(1) - 1)
    def _():
        y_ref[...] = x1_ref[...] + acc_scr[...]


def _moe(xn, comb, x1, wg, wu, wd, tm):
    n = xn.shape[0]
    return pl.pallas_call(
        _moe_kernel,
        grid=(n // tm, N_EXPERTS),
        in_specs=[pl.BlockSpec((tm, 1024), lambda i, e: (i, 0)),
                  pl.BlockSpec((tm, LANES), lambda i, e: (i, 0)),
                  pl.BlockSpec((tm, 1024), lambda i, e: (i, 0)),
                  pl.BlockSpec((1, 1024, D_FF), lambda i, e: (e, 0, 0)),
                  pl.BlockSpec((1, 1024, D_FF), lambda i, e: (e, 0, 0)),
                  pl.BlockSpec((1, D_FF, 1024), lambda i, e: (e, 0, 0))],
        out_specs=pl.BlockSpec((tm, 1024), lambda i, e: (i, 0)),
        out_shape=jax.ShapeDtypeStruct((n, 1024), F32),
        scratch_shapes=[pltpu.VMEM((tm, 1024), F32)],
        compiler_params=_params("parallel", "arbitrary"),
        name="moe",
    )(xn, comb, x1, wg, wu, wd)


def _pick(n, pref):
    t = min(n, pref)
    while n % t:
        t //= 2
    return t


def kernel(x_prompt, x_sample, cache_k, cache_v, state_hgrn, page_table, rms_in, w_in, att_q_norm, att_k_norm, att_lambda, att_sub_norm, hg_lower_bound, hg_out_norm, w_branch_att, w_branch_hg, w_out, rms_ffn, w_router_group, b_router_group, w_router_expert, b_router_expert, w_exp_gate, w_exp_up, w_exp_down):
    depth = rms_in.shape[0]
    assert depth == 1 and hg_lower_bound.shape[0] == 2
    b, t, _ = x_prompt.shape
    db, ds, _ = x_sample.shape
    assert ds == 1
    lam_init = 0.8 - 0.6 * math.exp(-0.3 * 0)

    w_in_bf = w_in[0].astype(BF16)
    qn_row = jnp.tile(att_q_norm[0], 2).reshape(1, LANES)
    kn_row = jnp.tile(att_k_norm[0], 2).reshape(1, LANES)
    seg_np = (np.arange(LANES)[:, None] // ATT_DHALF) == (np.arange(LANES)[None, :] // ATT_DHALF)
    seg = jnp.asarray(seg_np.astype(np.float32), BF16)
    sub_row = att_sub_norm[0].reshape(1, LANES)
    hgn_row = hg_out_norm[0].reshape(1, LANES)
    wpa = w_branch_att[0].astype(BF16)
    wph = w_branch_hg[0].astype(BF16)
    wo = w_out[0].astype(BF16)
    wr = jnp.zeros((D_MODEL, LANES), F32)
    wr = wr.at[:, :N_EXPERTS].set(w_router_expert[0]).at[:, N_EXPERTS:N_EXPERTS + N_GROUPS].set(w_router_group[0])
    wr = wr.astype(BF16)
    br = jnp.zeros((1, LANES), F32)
    br = br.at[0, :N_EXPERTS].set(b_router_expert[0]).at[0, N_EXPERTS:N_EXPERTS + N_GROUPS].set(b_router_group[0])
    weg = w_exp_gate[0].astype(BF16)
    weu = w_exp_up[0].astype(BF16)
    wed = w_exp_down[0].astype(BF16)
    gin = rms_in[0].reshape(1, D_MODEL)
    gffn = rms_ffn[0].reshape(1, D_MODEL)
    lam_p = att_lambda[0]

    def tail(x2, oa, oh, sga, sgh, tm_merge, tm_moe):
        x1, xn, comb = _merge(oa, oh, sga, sgh, x2, wpa, wph, wo, gffn, wr, br, tm_merge)
        return _moe(xn, comb, x1, weg, weu, wed, tm_moe)

    n = b * t
    xp = x_prompt.reshape(n, D_MODEL)
    q, kt, v, hq, logf, kk, hv, og, sga, sgh = _inproj(
        xp, gin, w_in_bf, qn_row, kn_row, hg_lower_bound, seg, _pick(t, 1024), (b, t))
    oa = _attn_prompt(q.reshape(b, t, 1024), kt, v.reshape(b, t, 512), lam_p, sub_row, lam_init, _pick(t, 512))
    r3 = lambda a: a.reshape(b, t, 1024)
    oh, state_p = _hgrn_prompt(r3(hq), r3(logf), r3(kk), r3(hv), r3(og), hgn_row, _pick(t, 256))
    yp = tail(xp, oa.reshape(n, 1024), oh.reshape(n, 1024), sga, sgh, _pick(n, 512), _pick(n, 1024))
    k_prompt = jnp.transpose(kt.reshape(1, b, ATT_KV_HEADS, 2, ATT_DHALF, t), (0, 1, 5, 2, 3, 4))

    xs = x_sample.reshape(db, D_MODEL)
    q, ks, vs, hq, logf, kk, hv, og, sga, sgh = _inproj(
        xs, gin, w_in_bf, qn_row, kn_row, hg_lower_bound, seg, _pick(db, 1024), None)
    q5 = q.reshape(db, ATT_KV_HEADS, 2, 2, ATT_DHALF)
    eye_n = jnp.eye(ATT_KV_HEADS, dtype=BF16)
    eye_c = jnp.eye(2, dtype=BF16)
    wq = jnp.einsum('bngcd,nm,ce->bcngmed', q5, eye_n, eye_c).reshape(db, 16, 512)
    n_pool = cache_k.shape[1]
    page = cache_k.shape[2]
    kt_pages = jnp.transpose(cache_k[0], (0, 2, 3, 4, 1)).reshape(n_pool, 512, page)
    v_pages = cache_v[0].reshape(n_pool, page * ATT_KV_HEADS, ATT_DV)
    oa_s = _attn_sample(wq, ks.reshape(db, 1, 512), vs.reshape(db, 1, 512), kt_pages, v_pages,
                        page_table, lam_p, sub_row, lam_init, _pick(page_table.shape[1], 8))
    col = lambda a: a.astype(F32).reshape(db, HG_HEADS, HG_DK, 1)
    row = lambda a: a.astype(F32).reshape(db, HG_HEADS, 1, HG_DV)
    oh_s, state_s = _hgrn_sample(state_hgrn[0], col(hq), col(logf), col(kk), row(hv), row(og), hgn_row)
    ys = tail(xs, oa_s.reshape(db, 1024).astype(BF16), oh_s.reshape(db, 1024).astype(BF16), sga, sgh,
              _pick(db, 512), _pick(db, 1024))

    return (yp.reshape(b, t, D_MODEL), ys.reshape(db, 1, D_MODEL),
            k_prompt, v.reshape(1, b, t, ATT_KV_HEADS, ATT_DV),
            state_p.reshape(1, b, HG_HEADS, HG_DK, HG_DV),
            ks.reshape(1, db, 1, ATT_KV_HEADS, 2, ATT_DHALF), vs.reshape(1, db, 1, ATT_KV_HEADS, ATT_DV),
            state_s.reshape(1, db, HG_HEADS, HG_DK, HG_DV))
```

```python
import functools
import math

import numpy as np
import jax
import jax.numpy as jnp
from jax import lax
from jax.experimental import pallas as pl
from jax.experimental.pallas import tpu as pltpu

F32 = jnp.float32
BF16 = jnp.bfloat16

D_MODEL = 1024
ATT_HEADS = 8
ATT_KV_HEADS = 4
ATT_DHALF = 64
ATT_DV = 128
HG_HEADS = 8
HG_DK = 128
HG_DV = 128
N_GROUPS = 4
EXP_PER_GROUP = 8
N_EXPERTS = 32
D_FF = 512
EPS = 1e-6
LANES = 128
VMEM_LIMIT = 56 * 1024 * 1024

LOG2E = 1.4426950408889634
Q_SCALE = (ATT_DHALF ** -0.5) * LOG2E

MOE_TILE = 512
HG_CHUNK = 64
HG_LEVELS = 6


def _dot(a, b):
    return jnp.dot(a, b, preferred_element_type=F32)


def _dot_nt(a, b):
    return lax.dot_general(a, b, (((1,), (1,)), ((), ())), preferred_element_type=F32)


def _dot_tn(a, b):
    return lax.dot_general(a, b, (((0,), (0,)), ((), ())), preferred_element_type=F32)


def _split3(x):
    hi = x.astype(BF16)
    r1 = x - hi.astype(F32)
    mid = r1.astype(BF16)
    lo = (r1 - mid.astype(F32)).astype(BF16)
    return hi, mid, lo


def _seg_rms(z, seg, gain_row, scale):
    outs = []
    for i in range(z.shape[1] // LANES):
        zi = z[:, i * LANES:(i + 1) * LANES]
        z2 = zi * zi
        hi = z2.astype(BF16)
        lo = (z2 - hi.astype(F32)).astype(BF16)
        ss = _dot(hi, seg) + _dot(lo, seg)
        y = zi * lax.rsqrt(ss * (1.0 / ATT_DHALF) + EPS) * gain_row
        if scale != 1.0:
            y = y * scale
        outs.append(y)
    return jnp.concatenate(outs, axis=1)


def _params(*sem):
    return pltpu.CompilerParams(dimension_semantics=sem, vmem_limit_bytes=VMEM_LIMIT)


def _rms_kernel(x_ref, g_ref, h_ref):
    x = x_ref[...]
    ms = jnp.mean(x * x, axis=-1, keepdims=True)
    h_ref[...] = (x * lax.rsqrt(ms + EPS) * g_ref[...]).astype(BF16)


def _rms_cast(x, g, tm):
    n = x.shape[0]
    return pl.pallas_call(
        _rms_kernel,
        grid=(n // tm,),
        in_specs=[pl.BlockSpec((tm, D_MODEL), lambda i: (i, 0)), pl.BlockSpec((1, D_MODEL), lambda i: (0, 0))],
        out_specs=pl.BlockSpec((tm, D_MODEL), lambda i: (i, 0)),
        out_shape=jax.ShapeDtypeStruct((n, D_MODEL), BF16),
        compiler_params=_params("parallel"),
        name="rms_in",
    )(x, g)


def _proj_kernel(h_ref, w_ref, *refs, mode, n_aux):
    aux, outs = refs[:n_aux], refs[n_aux:]
    z = _dot(h_ref[...], w_ref[...])
    if mode == "q":
        outs[0][...] = _seg_rms(z, aux[0][...], aux[1][...], Q_SCALE).astype(BF16)
    elif mode in ("kv", "kv_t"):
        k = _seg_rms(z[:, :512], aux[0][...], aux[1][...], 1.0)
        if mode == "kv_t":
            outs[0][0] = k.T
        else:
            outs[0][...] = k
        outs[1][...] = z[:, 512:]
    elif mode == "hq":
        outs[0][...] = (z * (HG_DK ** -0.5)).astype(BF16)
    elif mode == "hf":
        lbp = aux[0][...]
        m = jnp.max(lbp, axis=0, keepdims=True)
        e = jnp.exp(lbp - m)
        lb = e[0:1] / jnp.sum(e, axis=0, keepdims=True)
        f = lb + (1.0 - lb) * jax.nn.sigmoid(z)
        outs[0][...] = jnp.log(f)
        outs[1][...] = (1.0 - lb) * jax.nn.sigmoid(-z)
    elif mode == "hv":
        outs[0][...] = z.astype(BF16)
    elif mode == "silu":
        outs[0][...] = z * jax.nn.sigmoid(z)
    elif mode == "sigmoid":
        outs[0][...] = jax.nn.sigmoid(z)


def _proj(h, w, col, mode, aux, outs, tm, name):
    n = h.shape[0]
    aux_specs = [pl.BlockSpec(a.shape, lambda i: (0, 0)) for a in aux]
    res = pl.pallas_call(
        functools.partial(_proj_kernel, mode=mode, n_aux=len(aux)),
        grid=(n // tm,),
        in_specs=[pl.BlockSpec((tm, D_MODEL), lambda i: (i, 0)),
                  pl.BlockSpec((D_MODEL, 1024), lambda i, col=col: (0, col))] + aux_specs,
        out_specs=tuple(pl.BlockSpec(o[2], o[3]) for o in outs),
        out_shape=tuple(jax.ShapeDtypeStruct(o[0], o[1]) for o in outs),
        compiler_params=_params("parallel"),
        name=name,
    )(h, w, *aux)
    return res


def _inproj(x, gin, w, qn_row, kn_row, lbp, seg, tm, seq):
    n = x.shape[0]
    h = _rms_cast(x, gin, tm)
    tok = lambda width, dt: ((n, width), dt, (tm, width), lambda i: (i, 0))
    q, = _proj(h, w, 0, "q", [seg, qn_row], [tok(1024, BF16)], tm, "proj_q")
    if seq is None:
        k, v = _proj(h, w, 1, "kv", [seg, kn_row], [tok(512, F32), tok(512, F32)], tm, "proj_kv")
    else:
        b, t = seq
        tpb = t // tm
        k_out = ((b, 512, t), F32, (1, 512, tm), lambda i: (i // tpb, 0, i % tpb))
        k, v = _proj(h, w, 1, "kv_t", [seg, kn_row], [k_out, tok(512, F32)], tm, "proj_kv")
    hq, = _proj(h, w, 2, "hq", [], [tok(1024, BF16)], tm, "proj_hq")
    logf, kk = _proj(h, w, 3, "hf", [lbp], [tok(1024, F32), tok(1024, F32)], tm, "proj_hf")
    hv, = _proj(h, w, 4, "hv", [], [tok(1024, BF16)], tm, "proj_hv")
    og, = _proj(h, w, 5, "silu", [], [tok(1024, F32)], tm, "proj_og")
    sga, = _proj(h, w, 6, "sigmoid", [], [tok(1024, F32)], tm, "proj_ga")
    sgh, = _proj(h, w, 7, "sigmoid", [], [tok(1024, F32)], tm, "proj_gh")
    return q, k, v, hq, logf, kk, hv, og, sga, sgh


def _diff_lambda(lp, lam_init):
    a = jnp.sum(lp[0:1] * lp[1:2], axis=-1, keepdims=True)
    b = jnp.sum(lp[2:3] * lp[3:4], axis=-1, keepdims=True)
    return jnp.exp(a) - jnp.exp(b) + lam_init


def _sub_norm(o, sub_row, lam_init):
    ms = jnp.mean(o * o, axis=-1, keepdims=True)
    return o * lax.rsqrt(ms + EPS) * sub_row * (1.0 - lam_init)


def _attn_kernel(q_ref, kt_ref, v_ref, lamp_ref, sub_ref, o_ref,
                 kb_scr, vb_scr, qs_scr, m_scr, acc_scr, *, tq, lam_init):
    qi = pl.program_id(2)

    @pl.when(qi == 0)
    def _():
        kb_scr[...] = kt_ref[0].astype(BF16)
        vb_scr[:, 0:LANES] = v_ref[0].astype(BF16)
        vb_scr[:, LANES:2 * LANES] = jnp.ones((vb_scr.shape[0], LANES), BF16)

    q = q_ref[0].astype(F32)
    lane = lax.broadcasted_iota(jnp.int32, (tq, LANES), 1)
    for g in range(2):
        qg = q[:, g * LANES:(g + 1) * LANES]
        for c in range(2):
            keep = (lane < ATT_DHALF) if c == 0 else (lane >= ATT_DHALF)
            r = g * 2 + c
            qs_scr[r * tq:(r + 1) * tq, :] = jnp.where(keep, qg, 0.0).astype(BF16)
    m_scr[...] = jnp.full(m_scr.shape, -jnp.inf, F32)
    acc_scr[...] = jnp.zeros(acc_scr.shape, F32)

    def chunk(j, masked):
        off = pl.multiple_of(j * tq, tq)
        s = _dot(qs_scr[...], kb_scr[:, pl.ds(off, tq)])
        if masked:
            row = lax.rem(lax.broadcasted_iota(jnp.int32, s.shape, 0), tq)
            s = jnp.where(lax.broadcasted_iota(jnp.int32, s.shape, 1) <= row, s, -jnp.inf)
        m_prev = m_scr[...]
        m_new = jnp.maximum(m_prev, jnp.max(s, axis=-1, keepdims=True))
        alpha = jnp.exp2(m_prev - m_new)
        pr = jnp.exp2(s - jnp.concatenate([m_new] * (tq // LANES), axis=1))
        pv = _dot(pr.astype(BF16), vb_scr[pl.ds(off, tq), :])
        acc_scr[...] = jnp.concatenate([alpha, alpha], axis=1) * acc_scr[...] + pv
        m_scr[...] = m_new

    def body(j, carry):
        chunk(j, False)
        return carry

    lax.fori_loop(0, qi, body, 0)
    chunk(qi, True)

    lam = _diff_lambda(lamp_ref[...], lam_init)
    acc = acc_scr[...]
    out = acc[:, 0:LANES] / acc[:, LANES:2 * LANES]
    for g in range(2):
        o0 = out[(2 * g) * tq:(2 * g + 1) * tq]
        o1 = out[(2 * g + 1) * tq:(2 * g + 2) * tq]
        o_ref[0, :, g * LANES:(g + 1) * LANES] = _sub_norm(o0 - lam * o1, sub_ref[...], lam_init).astype(BF16)


def _attn_prompt(q, kt, v, lam_p, sub_row, lam_init, tq):
    b, t, _ = q.shape
    return pl.pallas_call(
        functools.partial(_attn_kernel, tq=tq, lam_init=lam_init),
        grid=(b, ATT_KV_HEADS, t // tq),
        in_specs=[
            pl.BlockSpec((1, tq, 256), lambda bi, n, qi: (bi, qi, n)),
            pl.BlockSpec((1, LANES, t), lambda bi, n, qi: (bi, n, 0)),
            pl.BlockSpec((1, t, LANES), lambda bi, n, qi: (bi, 0, n)),
            pl.BlockSpec((4, ATT_DHALF), lambda bi, n, qi: (0, 0)),
            pl.BlockSpec((1, LANES), lambda bi, n, qi: (0, 0)),
        ],
        out_specs=pl.BlockSpec((1, tq, 256), lambda bi, n, qi: (bi, qi, n)),
        out_shape=jax.ShapeDtypeStruct((b, t, 1024), BF16),
        scratch_shapes=[pltpu.VMEM((LANES, t), BF16), pltpu.VMEM((t, 2 * LANES), BF16),
                        pltpu.VMEM((4 * tq, LANES), BF16), pltpu.VMEM((4 * tq, LANES), F32),
                        pltpu.VMEM((4 * tq, 2 * LANES), F32)],
        compiler_params=_params("parallel", "parallel", "arbitrary"),
        name="attn_prompt",
    )(q, kt, v, lam_p, sub_row)


def _hgrn_tables(c, levels):
    t = np.arange(c)[:, None]
    u = np.arange(c)[None, :]
    w = [(u <= t), (u > t)]
    masks = [(t == u)]
    for l in range(1, levels + 1):
        n = 1 << l
        half = n // 2
        mid = (t // n) * n + half
        hi = t >= mid
        w.append(np.where(hi, (u > mid) & (u <= t), (u > t) & (u <= mid)))
        masks.append((t // n == u // n) & (t % n >= half) & (u % n < half))
    w = np.concatenate(w, axis=0).astype(np.float32)
    w3 = np.concatenate([w, w, w], axis=1)
    if len(masks) % 2:
        masks.append(np.zeros_like(masks[0]))
    pairs = [np.concatenate([masks[i], masks[i + 1]], axis=1) for i in range(0, len(masks), 2)]
    return w3, np.stack(pairs).astype(np.float32)


def _hgrn_kernel(q_ref, g_ref, k_ref, v_ref, og_ref, norm_ref, w_ref, mask_ref, o_ref, s_ref,
                 st_scr, *, c, levels, n_chunks):
    ti = pl.program_id(2)

    @pl.when(ti == 0)
    def _():
        st_scr[...] = jnp.zeros(st_scr.shape, F32)

    w = w_ref[...]
    st = st_scr[...]
    zeros = jnp.zeros((c, LANES), BF16)
    for ci in range(n_chunks):
        rows = slice(ci * c, (ci + 1) * c)
        q = q_ref[0, rows, :].astype(F32)
        g = g_ref[0, rows, :]
        k = k_ref[0, rows, :]
        v = v_ref[0, rows, :]
        e = jnp.exp(_dot(w, jnp.concatenate(_split3(g), axis=0)))
        e_b = e[0:c]
        e_k = e[c:2 * c]
        qs = [q.astype(BF16)]
        ks = [k.astype(BF16)]
        for l in range(1, levels + 1):
            e_l = e[(l + 1) * c:(l + 2) * c]
            qs.append((q * e_l).astype(BF16))
            ks.append((k * e_l).astype(BF16))
        if len(qs) % 2:
            qs.append(zeros)
            ks.append(zeros)
        a2 = None
        for p in range(len(qs) // 2):
            ql = jnp.concatenate([qs[2 * p], qs[2 * p + 1]], axis=1)
            kbd = jnp.concatenate([jnp.concatenate([ks[2 * p], zeros], axis=1),
                                   jnp.concatenate([zeros, ks[2 * p + 1]], axis=1)], axis=0)
            term = mask_ref[p] * _dot_nt(ql, kbd)
            a2 = term if a2 is None else a2 + term
        o = (_dot_nt((q * e_b).astype(BF16), st.astype(BF16))
             + _dot(a2.astype(BF16), jnp.concatenate([v, v], axis=0)))
        st = st * e_b[c - 1:c, :] + _dot_tn(v, (k * e_k).astype(BF16))
        ms = jnp.mean(o * o, axis=-1, keepdims=True)
        y = o * lax.rsqrt(ms + EPS) * norm_ref[...] * og_ref[0, rows, :]
        o_ref[0, rows, :] = y.astype(BF16)
    st_scr[...] = st

    @pl.when(ti == pl.num_programs(2) - 1)
    def _():
        s_ref[0, 0] = st.T


def _hgrn_prompt(hq, logf, kk, hv, og, norm_row, ct):
    b, t, _ = hq.shape
    c, levels = HG_CHUNK, HG_LEVELS
    w_np, m_np = _hgrn_tables(c, levels)
    w = jnp.asarray(w_np, BF16)
    masks = jnp.asarray(m_np, F32)
    tok = pl.BlockSpec((1, ct, LANES), lambda bi, h, ti: (bi, ti, h))
    return pl.pallas_call(
        functools.partial(_hgrn_kernel, c=c, levels=levels, n_chunks=ct // c),
        grid=(b, HG_HEADS, t // ct),
        in_specs=[tok, tok, tok, tok, tok,
                  pl.BlockSpec((1, LANES), lambda bi, h, ti: (0, 0)),
                  pl.BlockSpec(w.shape, lambda bi, h, ti: (0, 0)),
                  pl.BlockSpec(masks.shape, lambda bi, h, ti: (0, 0, 0))],
        out_specs=(tok, pl.BlockSpec((1, 1, HG_DK, HG_DV), lambda bi, h, ti: (bi, h, 0, 0))),
        out_shape=(jax.ShapeDtypeStruct((b, t, 1024), BF16),
                   jax.ShapeDtypeStruct((b, HG_HEADS, HG_DK, HG_DV), F32)),
        scratch_shapes=[pltpu.VMEM((HG_DV, HG_DK), F32)],
        compiler_params=_params("parallel", "parallel", "arbitrary"),
        name="hgrn_prompt",
    )(hq, logf, kk, hv, og, norm_row, w, masks)


def _dec_attn_kernel(pt_ref, wq_ref, kn_ref, vn_ref, lamp_ref, sub_ref, *refs, pages, lam_init):
    k_refs = refs[:pages]
    v_refs = refs[pages:2 * pages]
    o_ref = refs[2 * pages]
    m_scr, l_scr, acc_scr = refs[2 * pages + 1:]
    j = pl.program_id(1)
    page = k_refs[0].shape[2]

    @pl.when(j == 0)
    def _():
        m_scr[...] = jnp.full(m_scr.shape, -jnp.inf, F32)
        l_scr[...] = jnp.zeros(l_scr.shape, F32)
        acc_scr[...] = jnp.zeros(acc_scr.shape, F32)

    rown = lax.rem(lax.broadcasted_iota(jnp.int32, (16, LANES), 0), 8) // 2
    wq = wq_ref[0]
    s = jnp.concatenate([_dot(wq, k_refs[i][0].astype(BF16)) for i in range(pages)], axis=1)
    m_prev = m_scr[...]
    m_new = jnp.maximum(m_prev, jnp.max(s, axis=-1, keepdims=True))
    alpha = jnp.exp2(m_prev - m_new)
    pr = jnp.exp2(s - m_new)
    l_scr[...] = alpha * l_scr[...] + jnp.sum(pr, axis=-1, keepdims=True)
    prb = pr.astype(BF16)
    pv = jnp.zeros((16, LANES), F32)
    for n in range(ATT_KV_HEADS):
        v_n = jnp.concatenate([v_refs[i][0, pl.ds(n, page, stride=ATT_KV_HEADS), :] for i in range(pages)],
                              axis=0).astype(BF16)
        pv = pv + jnp.where(rown == n, _dot(prb, v_n), 0.0)
    acc_scr[...] = alpha * acc_scr[...] + pv
    m_scr[...] = m_new

    @pl.when(j == pl.num_programs(1) - 1)
    def _():
        s_cur = jnp.sum(wq.astype(F32) * kn_ref[0], axis=-1, keepdims=True)
        m_old = m_scr[...]
        m_fin = jnp.maximum(m_old, s_cur)
        al = jnp.exp2(m_old - m_fin)
        pc = jnp.exp2(s_cur - m_fin)
        l_fin = al * l_scr[...] + pc
        vn = vn_ref[0]
        v_cur = jnp.zeros((16, LANES), F32)
        for n in range(ATT_KV_HEADS):
            v_cur = v_cur + jnp.where(rown == n, vn[:, n * LANES:(n + 1) * LANES], 0.0)
        out = (al * acc_scr[...] + pc * v_cur) / l_fin
        lam = _diff_lambda(lamp_ref[...], lam_init)
        o = out[0:8] - lam * out[8:16]
        o_ref[0] = _sub_norm(o, sub_ref[...], lam_init)


def _attn_sample(wq, k_new, v_new, kt_pages, v_pages, page_table, lam_p, sub_row, lam_init, pages):
    db, n_pages = page_table.shape
    pt_flat = page_table.reshape(-1)
    blk = kt_pages.shape[1:]

    def page_spec(i):
        return pl.BlockSpec((1,) + blk, lambda bi, j, pt, i=i: (pt[bi * n_pages + j * pages + i], 0, 0))

    row3 = lambda w: pl.BlockSpec((1, 1, w), lambda bi, j, pt: (bi, 0, 0))
    grid_spec = pltpu.PrefetchScalarGridSpec(
        num_scalar_prefetch=1,
        grid=(db, n_pages // pages),
        in_specs=[pl.BlockSpec((1, 16, 512), lambda bi, j, pt: (bi, 0, 0)), row3(512), row3(512),
                  pl.BlockSpec((4, ATT_DHALF), lambda bi, j, pt: (0, 0)),
                  pl.BlockSpec((1, LANES), lambda bi, j, pt: (0, 0))]
                 + [page_spec(i) for i in range(pages)] + [page_spec(i) for i in range(pages)],
        out_specs=pl.BlockSpec((1, 8, LANES), lambda bi, j, pt: (bi, 0, 0)),
        scratch_shapes=[pltpu.VMEM((16, 1), F32), pltpu.VMEM((16, 1), F32), pltpu.VMEM((16, LANES), F32)],
    )
    return pl.pallas_call(
        functools.partial(_dec_attn_kernel, pages=pages, lam_init=lam_init),
        grid_spec=grid_spec,
        out_shape=jax.ShapeDtypeStruct((db, 8, LANES), F32),
        compiler_params=_params("parallel", "arbitrary"),
        name="attn_sample",
    )(pt_flat, wq, k_new, v_new, lam_p, sub_row, *([kt_pages] * pages), *([v_pages] * pages))


def _hgrn_step_kernel(s_ref, q_ref, g_ref, k_ref, v_ref, og_ref, norm_ref, o_ref, sn_ref):
    for h in range(HG_HEADS):
        s = s_ref[0, h]
        f = jnp.exp(g_ref[0, h])
        qc = q_ref[0, h]
        kc = k_ref[0, h]
        v = v_ref[0, h]
        o = jnp.sum(s * (qc * f), axis=0, keepdims=True) + jnp.sum(qc * kc, axis=0, keepdims=True) * v
        sn_ref[0, h] = s * f + kc * v
        ms = jnp.mean(o * o, axis=-1, keepdims=True)
        o_ref[0, h] = o * lax.rsqrt(ms + EPS) * norm_ref[...] * og_ref[0, h]


def _hgrn_sample(state, qcol, gcol, kcol, vrow, ogrow, norm_row):
    db = state.shape[0]
    col = pl.BlockSpec((1, HG_HEADS, HG_DK, 1), lambda bi: (bi, 0, 0, 0))
    row = pl.BlockSpec((1, HG_HEADS, 1, HG_DV), lambda bi: (bi, 0, 0, 0))
    st = pl.BlockSpec((1, HG_HEADS, HG_DK, HG_DV), lambda bi: (bi, 0, 0, 0))
    return pl.pallas_call(
        _hgrn_step_kernel,
        grid=(db,),
        in_specs=[st, col, col, col, row, row, pl.BlockSpec((1, LANES), lambda bi: (0, 0))],
        out_specs=(row, st),
        out_shape=(jax.ShapeDtypeStruct((db, HG_HEADS, 1, HG_DV), F32),
                   jax.ShapeDtypeStruct(state.shape, F32)),
        compiler_params=_params("parallel"),
        name="hgrn_sample",
    )(state, qcol, gcol, kcol, vrow, ogrow, norm_row)


def _lane_first(cond, lane):
    return jnp.min(jnp.where(cond, lane, LANES), axis=-1, keepdims=True)


def _route(logits):
    lane = lax.broadcasted_iota(jnp.int32, logits.shape, 1)
    gmask = (lane >= N_EXPERTS) & (lane < N_EXPERTS + N_GROUPS)
    lg = jnp.where(gmask, logits, -jnp.inf)
    mg = jnp.max(lg, axis=-1, keepdims=True)
    eg = jnp.exp(lg - mg)
    pg = eg / jnp.sum(eg, axis=-1, keepdims=True)
    pg_top = jnp.max(pg, axis=-1, keepdims=True)
    gsel = _lane_first(gmask & (pg == pg_top), lane) - N_EXPERTS
    emask = (lane >= gsel * EXP_PER_GROUP) & (lane < (gsel + 1) * EXP_PER_GROUP)
    le = jnp.where(emask, logits, -jnp.inf)
    me = jnp.max(le, axis=-1, keepdims=True)
    ee = jnp.exp(le - me)
    pe = ee / jnp.sum(ee, axis=-1, keepdims=True)
    pe = jnp.where(emask, pe, -1.0)
    p1 = jnp.max(pe, axis=-1, keepdims=True)
    e1 = _lane_first(pe == p1, lane)
    pe2 = jnp.where(lane == e1, -1.0, pe)
    p2 = jnp.max(pe2, axis=-1, keepdims=True)
    e2 = _lane_first(pe2 == p2, lane)
    den = p1 + p2
    w1 = p1 / den * pg_top
    w2 = p2 / den * pg_top
    comb = jnp.where(lane == e1, w1, 0.0) + jnp.where(lane == e2, w2, 0.0)
    picks = jnp.where(lane == 0, e1.astype(F32), jnp.where(lane == 1, e2.astype(F32),
                      jnp.where(lane == 2, w1, jnp.where(lane == 3, w2, 0.0))))
    return comb, picks


ROW_TILES = D_MODEL // LANES


def _merge_kernel(oa_ref, oh_ref, sga_ref, sgh_ref, x_ref, wpa_ref, wph_ref, wo_ref, gf_ref,
                  wr_ref, br_ref, x1_ref, xn_ref, route_ref, *, sparse):
    ya = _dot(oa_ref[...], wpa_ref[...])
    yh = _dot(oh_ref[...], wph_ref[...])
    mixed = (sga_ref[...] * ya + sgh_ref[...] * yh).astype(BF16)
    x1 = x_ref[...] + _dot(mixed, wo_ref[...])
    x1_ref[...] = x1
    ms = jnp.mean(x1 * x1, axis=-1, keepdims=True)
    xn_f = x1 * lax.rsqrt(ms + EPS) * gf_ref[...]
    xn = xn_f.astype(BF16)
    comb, picks = _route(_dot(xn, wr_ref[...]) + br_ref[...])
    if sparse:
        tm = x1.shape[0]
        for s in range(ROW_TILES):
            xn_ref[pl.ds(s, tm, stride=ROW_TILES), :] = xn_f[:, s * LANES:(s + 1) * LANES]
        route_ref[...] = picks
    else:
        xn_ref[...] = xn
        route_ref[...] = comb


def _merge(oa, oh, sga, sgh, x, wpa, wph, wo, gf, wr, br, tm, sparse):
    n = x.shape[0]
    tok = pl.BlockSpec((tm, 1024), lambda i: (i, 0))
    wsp = pl.BlockSpec((1024, 1024), lambda i: (0, 0))
    if sparse:
        xn_spec = pl.BlockSpec((tm * ROW_TILES, LANES), lambda i: (i, 0))
        xn_shape = jax.ShapeDtypeStruct((n * ROW_TILES, LANES), F32)
    else:
        xn_spec, xn_shape = tok, jax.ShapeDtypeStruct((n, 1024), BF16)
    return pl.pallas_call(
        functools.partial(_merge_kernel, sparse=sparse),
        grid=(n // tm,),
        in_specs=[tok, tok, tok, tok, tok, wsp, wsp, wsp,
                  pl.BlockSpec((1, 1024), lambda i: (0, 0)),
                  pl.BlockSpec((1024, LANES), lambda i: (0, 0)),
                  pl.BlockSpec((1, LANES), lambda i: (0, 0))],
        out_specs=(tok, xn_spec, pl.BlockSpec((tm, LANES), lambda i: (i, 0))),
        out_shape=(jax.ShapeDtypeStruct((n, 1024), F32), xn_shape,
                   jax.ShapeDtypeStruct((n, LANES), F32)),
        compiler_params=_params("parallel"),
        name="merge",
    )(oa, oh, sga, sgh, x, wpa, wph, wo, gf, wr, br)


def _moe_kernel(xn_ref, comb_ref, x1_ref, wg_ref, wu_ref, wd_ref, y_ref, acc_scr):
    e = pl.program_id(1)

    @pl.when(e == 0)
    def _():
        acc_scr[...] = jnp.zeros(acc_scr.shape, F32)

    comb = comb_ref[...]
    lane = lax.broadcasted_iota(jnp.int32, comb.shape, 1)
    c = jnp.sum(jnp.where(lane == e, comb, 0.0), axis=-1, keepdims=True)
    xn = xn_ref[...]
    hg = _dot(xn, wg_ref[0])
    hu = _dot(xn, wu_ref[0])
    hid = hg * jax.nn.sigmoid(hg) * hu
    acc_scr[...] += _dot((hid * c).astype(BF16), wd_ref[0])

    @pl.when(e == pl.num_programs(1) - 1)
    def _():
        y_ref[...] = x1_ref[...] + acc_scr[...]


def _moe(xn, comb, x1, wg, wu, wd, tm):
    n = xn.shape[0]
    return pl.pallas_call(
        _moe_kernel,
        grid=(n // tm, N_EXPERTS),
        in_specs=[pl.BlockSpec((tm, 1024), lambda i, e: (i, 0)),
                  pl.BlockSpec((tm, LANES), lambda i, e: (i, 0)),
                  pl.BlockSpec((tm, 1024), lambda i, e: (i, 0)),
                  pl.BlockSpec((1, 1024, D_FF), lambda i, e: (e, 0, 0)),
                  pl.BlockSpec((1, 1024, D_FF), lambda i, e: (e, 0, 0)),
                  pl.BlockSpec((1, D_FF, 1024), lambda i, e: (e, 0, 0))],
        out_specs=pl.BlockSpec((tm, 1024), lambda i, e: (i, 0)),
        out_shape=jax.ShapeDtypeStruct((n, 1024), F32),
        scratch_shapes=[pltpu.VMEM((tm, 1024), F32)],
        compiler_params=_params("parallel", "arbitrary"),
        name="moe",
    )(xn, comb, x1, wg, wu, wd)


def _row_copy(src_hbm, src_row, dst, dst_row, sem):
    return pltpu.make_async_copy(src_hbm.at[pl.ds(src_row * ROW_TILES, ROW_TILES)],
                                 dst.at[pl.ds(dst_row * ROW_TILES, ROW_TILES)], sem)


def _gather_rows(idx_ref, src_hbm, dst, sem, n_rows):
    def body(r, carry):
        _row_copy(src_hbm, idx_ref[r], dst, r, sem).start()
        return carry
    lax.fori_loop(0, n_rows, body, 0)


def _wait_rows(src_hbm, dst, sem):
    pltpu.make_async_copy(src_hbm.at[pl.ds(0, dst.shape[0])], dst, sem).wait()


def _rows_to_mat(buf, n_rows):
    return jnp.concatenate([buf[pl.ds(s, n_rows, stride=ROW_TILES), :] for s in range(ROW_TILES)], axis=1)


def _moe_ffn_kernel(te_ref, tok_ref, tokn_ref, x_hbm, rw_ref, wg_ref, wu_ref, wd_ref, ys_ref, buf, sem, *, tp):
    i = pl.program_id(0)
    slot = lax.rem(i, 2)

    @pl.when(i == 0)
    def _():
        _gather_rows(tok_ref, x_hbm, buf.at[0], sem.at[0], tp)

    @pl.when(i + 1 < pl.num_programs(0))
    def _():
        _gather_rows(tokn_ref, x_hbm, buf.at[1 - slot], sem.at[1 - slot], tp)

    _wait_rows(x_hbm, buf.at[slot], sem.at[slot])
    x = _rows_to_mat(buf.at[slot], tp).astype(BF16)
    hg = _dot(x, wg_ref[0])
    hu = _dot(x, wu_ref[0])
    hid = (hg * jax.nn.sigmoid(hg) * hu).astype(BF16)
    out = _dot(hid, wd_ref[0]) * rw_ref[...]
    for s in range(ROW_TILES):
        ys_ref[pl.ds(s, tp, stride=ROW_TILES), :] = out[:, s * LANES:(s + 1) * LANES]


def _moe_ffn(xg, row_token, row_w, tile_expert, wg, wu, wd, tp):
    n_tiles = tile_expert.shape[0]
    idx = lambda off: pl.BlockSpec((tp,), lambda i, te: (jnp.minimum(i + off, n_tiles - 1),),
                                   memory_space=pltpu.SMEM)
    wsp = lambda a, b: pl.BlockSpec((1, a, b), lambda i, te: (te[i], 0, 0))
    grid_spec = pltpu.PrefetchScalarGridSpec(
        num_scalar_prefetch=1,
        grid=(n_tiles,),
        in_specs=[idx(0), idx(1), pl.BlockSpec(memory_space=pl.ANY),
                  pl.BlockSpec((tp, 1), lambda i, te: (i, 0)),
                  wsp(1024, D_FF), wsp(1024, D_FF), wsp(D_FF, 1024)],
        out_specs=pl.BlockSpec((tp * ROW_TILES, LANES), lambda i, te: (i, 0)),
        scratch_shapes=[pltpu.VMEM((2, tp * ROW_TILES, LANES), F32), pltpu.SemaphoreType.DMA((2,))],
    )
    return pl.pallas_call(
        functools.partial(_moe_ffn_kernel, tp=tp),
        grid_spec=grid_spec,
        out_shape=jax.ShapeDtypeStruct((n_tiles * tp * ROW_TILES, LANES), F32),
        compiler_params=_params("arbitrary"),
        name="moe_ffn",
    )(tile_expert, row_token, row_token, xg, row_w, wg, wu, wd)


def _moe_combine_kernel(p1_ref, p1n_ref, p2_ref, p2n_ref, ys_hbm, x1_ref, y_ref, buf, sem, *, tm):
    i = pl.program_id(0)
    slot = lax.rem(i, 2)

    def gather(a_ref, b_ref, s):
        _gather_rows(a_ref, ys_hbm, buf.at[s, 0], sem.at[s, 0], tm)
        _gather_rows(b_ref, ys_hbm, buf.at[s, 1], sem.at[s, 1], tm)

    @pl.when(i == 0)
    def _():
        gather(p1_ref, p2_ref, 0)

    @pl.when(i + 1 < pl.num_programs(0))
    def _():
        gather(p1n_ref, p2n_ref, 1 - slot)

    _wait_rows(ys_hbm, buf.at[slot, 0], sem.at[slot, 0])
    _wait_rows(ys_hbm, buf.at[slot, 1], sem.at[slot, 1])
    y_ref[...] = x1_ref[...] + (_rows_to_mat(buf.at[slot, 0], tm) + _rows_to_mat(buf.at[slot, 1], tm))


def _moe_combine(ys, pos1, pos2, x1, tm):
    n = x1.shape[0]
    n_tiles = n // tm
    idx = lambda off: pl.BlockSpec((tm,), lambda i: (jnp.minimum(i + off, n_tiles - 1),),
                                   memory_space=pltpu.SMEM)
    return pl.pallas_call(
        functools.partial(_moe_combine_kernel, tm=tm),
        grid=(n_tiles,),
        in_specs=[idx(0), idx(1), idx(0), idx(1), pl.BlockSpec(memory_space=pl.ANY),
                  pl.BlockSpec((tm, 1024), lambda i: (i, 0))],
        out_specs=pl.BlockSpec((tm, 1024), lambda i: (i, 0)),
        out_shape=jax.ShapeDtypeStruct((n, 1024), F32),
        scratch_shapes=[pltpu.VMEM((2, 2, tm * ROW_TILES, LANES), F32), pltpu.SemaphoreType.DMA((2, 2))],
        compiler_params=_params("arbitrary"),
        name="moe_combine",
    )(pos1, pos1, pos2, pos2, ys, x1)


def _dispatch(picks, tp):
    n = picks.shape[0]
    e_pair = jnp.concatenate([picks[:, 0], picks[:, 1]]).astype(jnp.int32)
    w_pair = jnp.concatenate([picks[:, 2], picks[:, 3]])
    t_pair = jnp.concatenate([jnp.arange(n, dtype=jnp.int32)] * 2)
    onehot = (e_pair[:, None] == jnp.arange(N_EXPERTS, dtype=jnp.int32)[None, :]).astype(jnp.int32)
    counts = jnp.sum(onehot, axis=0)
    rank = jnp.sum(jnp.cumsum(onehot, axis=0) * onehot, axis=1) - 1
    tiles = (counts + tp - 1) // tp
    tile_end = jnp.cumsum(tiles)
    row_start = (tile_end - tiles) * tp
    pos = row_start[e_pair] + rank
    n_tiles = (2 * n) // tp + N_EXPERTS
    rows = n_tiles * tp
    row_token = jnp.zeros((rows,), jnp.int32).at[pos].set(t_pair, unique_indices=True)
    row_w = jnp.zeros((rows,), F32).at[pos].set(w_pair, unique_indices=True)
    tile_expert = jnp.minimum(jnp.searchsorted(tile_end, jnp.arange(n_tiles, dtype=jnp.int32), side='right'),
                              N_EXPERTS - 1).astype(jnp.int32)
    return row_token, row_w.reshape(rows, 1), tile_expert, pos[:n], pos[n:]


def _pick(n, pref):
    t = min(n, pref)
    while n % t:
        t //= 2
    return t


def kernel(x_prompt, x_sample, cache_k, cache_v, state_hgrn, page_table, rms_in, w_in, att_q_norm, att_k_norm, att_lambda, att_sub_norm, hg_lower_bound, hg_out_norm, w_branch_att, w_branch_hg, w_out, rms_ffn, w_router_group, b_router_group, w_router_expert, b_router_expert, w_exp_gate, w_exp_up, w_exp_down):
    depth = rms_in.shape[0]
    assert depth == 1 and hg_lower_bound.shape[0] == 2
    b, t, _ = x_prompt.shape
    db, ds, _ = x_sample.shape
    assert ds == 1
    lam_init = 0.8 - 0.6 * math.exp(-0.3 * 0)

    w_in_bf = w_in[0].astype(BF16)
    qn_row = jnp.tile(att_q_norm[0], 2).reshape(1, LANES)
    kn_row = jnp.tile(att_k_norm[0], 2).reshape(1, LANES)
    seg_np = (np.arange(LANES)[:, None] // ATT_DHALF) == (np.arange(LANES)[None, :] // ATT_DHALF)
    seg = jnp.asarray(seg_np.astype(np.float32), BF16)
    sub_row = att_sub_norm[0].reshape(1, LANES)
    hgn_row = hg_out_norm[0].reshape(1, LANES)
    wpa = w_branch_att[0].astype(BF16)
    wph = w_branch_hg[0].astype(BF16)
    wo = w_out[0].astype(BF16)
    wr = jnp.zeros((D_MODEL, LANES), F32)
    wr = wr.at[:, :N_EXPERTS].set(w_router_expert[0]).at[:, N_EXPERTS:N_EXPERTS + N_GROUPS].set(w_router_group[0])
    wr = wr.astype(BF16)
    br = jnp.zeros((1, LANES), F32)
    br = br.at[0, :N_EXPERTS].set(b_router_expert[0]).at[0, N_EXPERTS:N_EXPERTS + N_GROUPS].set(b_router_group[0])
    weg = w_exp_gate[0].astype(BF16)
    weu = w_exp_up[0].astype(BF16)
    wed = w_exp_down[0].astype(BF16)
    gin = rms_in[0].reshape(1, D_MODEL)
    gffn = rms_ffn[0].reshape(1, D_MODEL)
    lam_p = att_lambda[0]

    def tail(x2, oa, oh, sga, sgh, tm_merge, tm_moe):
        sparse = x2.shape[0] >= 16 * MOE_TILE
        x1, xn, route = _merge(oa, oh, sga, sgh, x2, wpa, wph, wo, gffn, wr, br, tm_merge, sparse)
        if not sparse:
            return _moe(xn, route, x1, weg, weu, wed, tm_moe)
        row_token, row_w, tile_expert, pos1, pos2 = _dispatch(route, MOE_TILE)
        ys = _moe_ffn(xn, row_token, row_w, tile_expert, weg, weu, wed, MOE_TILE)
        return _moe_combine(ys, pos1, pos2, x1, _pick(x2.shape[0], 256))

    n = b * t
    xp = x_prompt.reshape(n, D_MODEL)
    q, kt, v, hq, logf, kk, hv, og, sga, sgh = _inproj(
        xp, gin, w_in_bf, qn_row, kn_row, hg_lower_bound, seg, _pick(t, 1024), (b, t))
    oa = _attn_prompt(q.reshape(b, t, 1024), kt, v.reshape(b, t, 512), lam_p, sub_row, lam_init, _pick(t, 512))
    r3 = lambda a: a.reshape(b, t, 1024)
    oh, state_p = _hgrn_prompt(r3(hq), r3(logf), r3(kk), r3(hv), r3(og), hgn_row, _pick(t, 256))
    yp = tail(xp, oa.reshape(n, 1024), oh.reshape(n, 1024), sga, sgh, _pick(n, 512), _pick(n, 1024))
    k_prompt = jnp.transpose(kt.reshape(1, b, ATT_KV_HEADS, 2, ATT_DHALF, t), (0, 1, 5, 2, 3, 4))

    xs = x_sample.reshape(db, D_MODEL)
    q, ks, vs, hq, logf, kk, hv, og, sga, sgh = _inproj(
        xs, gin, w_in_bf, qn_row, kn_row, hg_lower_bound, seg, _pick(db, 1024), None)
    q5 = q.reshape(db, ATT_KV_HEADS, 2, 2, ATT_DHALF)
    eye_n = jnp.eye(ATT_KV_HEADS, dtype=BF16)
    eye_c = jnp.eye(2, dtype=BF16)
    wq = jnp.einsum('bngcd,nm,ce->bcngmed', q5, eye_n, eye_c).reshape(db, 16, 512)
    n_pool = cache_k.shape[1]
    page = cache_k.shape[2]
    kt_pages = jnp.transpose(cache_k[0], (0, 2, 3, 4, 1)).reshape(n_pool, 512, page)
    v_pages = cache_v[0].reshape(n_pool, page * ATT_KV_HEADS, ATT_DV)
    oa_s = _attn_sample(wq, ks.reshape(db, 1, 512), vs.reshape(db, 1, 512), kt_pages, v_pages,
                        page_table, lam_p, sub_row, lam_init, _pick(page_table.shape[1], 8))
    col = lambda a: a.astype(F32).reshape(db, HG_HEADS, HG_DK, 1)
    row = lambda a: a.astype(F32).reshape(db, HG_HEADS, 1, HG_DV)
    oh_s, state_s = _hgrn_sample(state_hgrn[0], col(hq), col(logf), col(kk), row(hv), row(og), hgn_row)
    ys = tail(xs, oa_s.reshape(db, 1024).astype(BF16), oh_s.reshape(db, 1024).astype(BF16), sga, sgh,
              _pick(db, 512), _pick(db, 1024))

    return (yp.reshape(b, t, D_MODEL), ys.reshape(db, 1, D_MODEL),
            k_prompt, v.reshape(1, b, t, ATT_KV_HEADS, ATT_DV),
            state_p.reshape(1, b, HG_HEADS, HG_DK, HG_DV),
            ks.reshape(1, db, 1, ATT_KV_HEADS, 2, ATT_DHALF), vs.reshape(1, db, 1, ATT_KV_HEADS, ATT_DV),
            state_s.reshape(1, db, HG_HEADS, HG_DK, HG_DV))
```

```python
import functools
import math

import numpy as np
import jax
import jax.numpy as jnp
from jax import lax
from jax.experimental import pallas as pl
from jax.experimental.pallas import tpu as pltpu

F32 = jnp.float32
BF16 = jnp.bfloat16

D_MODEL = 1024
ATT_HEADS = 8
ATT_KV_HEADS = 4
ATT_DHALF = 64
ATT_DV = 128
HG_HEADS = 8
HG_DK = 128
HG_DV = 128
N_GROUPS = 4
EXP_PER_GROUP = 8
N_EXPERTS = 32
D_FF = 512
EPS = 1e-6
LANES = 128
VMEM_LIMIT = 56 * 1024 * 1024

LOG2E = 1.4426950408889634
Q_SCALE = (ATT_DHALF ** -0.5) * LOG2E

MOE_TILE = 512
HG_CHUNK = 64
HG_LEVELS = 6


def _dot(a, b):
    return jnp.dot(a, b, preferred_element_type=F32)


def _dot_nt(a, b):
    return lax.dot_general(a, b, (((1,), (1,)), ((), ())), preferred_element_type=F32)


def _dot_tn(a, b):
    return lax.dot_general(a, b, (((0,), (0,)), ((), ())), preferred_element_type=F32)


def _split3(x):
    hi = x.astype(BF16)
    r1 = x - hi.astype(F32)
    mid = r1.astype(BF16)
    lo = (r1 - mid.astype(F32)).astype(BF16)
    return hi, mid, lo


def _seg_rms(z, seg, gain_row, scale):
    outs = []
    for i in range(z.shape[1] // LANES):
        zi = z[:, i * LANES:(i + 1) * LANES]
        z2 = zi * zi
        hi = z2.astype(BF16)
        lo = (z2 - hi.astype(F32)).astype(BF16)
        ss = _dot(hi, seg) + _dot(lo, seg)
        y = zi * lax.rsqrt(ss * (1.0 / ATT_DHALF) + EPS) * gain_row
        if scale != 1.0:
            y = y * scale
        outs.append(y)
    return jnp.concatenate(outs, axis=1)


def _params(*sem):
    return pltpu.CompilerParams(dimension_semantics=sem, vmem_limit_bytes=VMEM_LIMIT)


def _rms_kernel(x_ref, g_ref, h_ref):
    x = x_ref[...]
    ms = jnp.mean(x * x, axis=-1, keepdims=True)
    h_ref[...] = (x * lax.rsqrt(ms + EPS) * g_ref[...]).astype(BF16)


def _rms_cast(x, g, tm):
    n = x.shape[0]
    return pl.pallas_call(
        _rms_kernel,
        grid=(n // tm,),
        in_specs=[pl.BlockSpec((tm, D_MODEL), lambda i: (i, 0)), pl.BlockSpec((1, D_MODEL), lambda i: (0, 0))],
        out_specs=pl.BlockSpec((tm, D_MODEL), lambda i: (i, 0)),
        out_shape=jax.ShapeDtypeStruct((n, D_MODEL), BF16),
        compiler_params=_params("parallel"),
        name="rms_in",
    )(x, g)


def _proj_kernel(h_ref, w_ref, *refs, mode, n_aux):
    aux, outs = refs[:n_aux], refs[n_aux:]
    z = _dot(h_ref[...], w_ref[...])
    if mode == "q":
        outs[0][...] = _seg_rms(z, aux[0][...], aux[1][...], Q_SCALE).astype(BF16)
    elif mode in ("kv", "kv_t"):
        k = _seg_rms(z[:, :512], aux[0][...], aux[1][...], 1.0)
        if mode == "kv_t":
            outs[0][0] = k.T
        else:
            outs[0][...] = k
        outs[1][...] = z[:, 512:]
    elif mode == "hq":
        outs[0][...] = (z * (HG_DK ** -0.5)).astype(BF16)
    elif mode == "hf":
        lbp = aux[0][...]
        m = jnp.max(lbp, axis=0, keepdims=True)
        e = jnp.exp(lbp - m)
        lb = e[0:1] / jnp.sum(e, axis=0, keepdims=True)
        f = lb + (1.0 - lb) * jax.nn.sigmoid(z)
        outs[0][...] = jnp.log(f)
        outs[1][...] = (1.0 - lb) * jax.nn.sigmoid(-z)
    elif mode == "hv":
        outs[0][...] = z.astype(BF16)
    elif mode == "silu":
        outs[0][...] = z * jax.nn.sigmoid(z)
    elif mode == "sigmoid":
        outs[0][...] = jax.nn.sigmoid(z)


def _proj(h, w, col, mode, aux, outs, tm, name):
    n = h.shape[0]
    aux_specs = [pl.BlockSpec(a.shape, lambda i: (0, 0)) for a in aux]
    res = pl.pallas_call(
        functools.partial(_proj_kernel, mode=mode, n_aux=len(aux)),
        grid=(n // tm,),
        in_specs=[pl.BlockSpec((tm, D_MODEL), lambda i: (i, 0)),
                  pl.BlockSpec((D_MODEL, 1024), lambda i, col=col: (0, col))] + aux_specs,
        out_specs=tuple(pl.BlockSpec(o[2], o[3]) for o in outs),
        out_shape=tuple(jax.ShapeDtypeStruct(o[0], o[1]) for o in outs),
        compiler_params=_params("parallel"),
        name=name,
    )(h, w, *aux)
    return res


def _inproj(x, gin, w, qn_row, kn_row, lbp, seg, tm, seq):
    n = x.shape[0]
    h = _rms_cast(x, gin, tm)
    tok = lambda width, dt: ((n, width), dt, (tm, width), lambda i: (i, 0))
    q, = _proj(h, w, 0, "q", [seg, qn_row], [tok(1024, BF16)], tm, "proj_q")
    if seq is None:
        k, v = _proj(h, w, 1, "kv", [seg, kn_row], [tok(512, F32), tok(512, F32)], tm, "proj_kv")
    else:
        b, t = seq
        tpb = t // tm
        k_out = ((b, 512, t), F32, (1, 512, tm), lambda i: (i // tpb, 0, i % tpb))
        k, v = _proj(h, w, 1, "kv_t", [seg, kn_row], [k_out, tok(512, F32)], tm, "proj_kv")
    hq, = _proj(h, w, 2, "hq", [], [tok(1024, BF16)], tm, "proj_hq")
    logf, kk = _proj(h, w, 3, "hf", [lbp], [tok(1024, F32), tok(1024, F32)], tm, "proj_hf")
    hv, = _proj(h, w, 4, "hv", [], [tok(1024, BF16)], tm, "proj_hv")
    og, = _proj(h, w, 5, "silu", [], [tok(1024, F32)], tm, "proj_og")
    sga, = _proj(h, w, 6, "sigmoid", [], [tok(1024, F32)], tm, "proj_ga")
    sgh, = _proj(h, w, 7, "sigmoid", [], [tok(1024, F32)], tm, "proj_gh")
    return q, k, v, hq, logf, kk, hv, og, sga, sgh


def _diff_lambda(lp, lam_init):
    a = jnp.sum(lp[0:1] * lp[1:2], axis=-1, keepdims=True)
    b = jnp.sum(lp[2:3] * lp[3:4], axis=-1, keepdims=True)
    return jnp.exp(a) - jnp.exp(b) + lam_init


def _sub_norm(o, sub_row, lam_init):
    ms = jnp.mean(o * o, axis=-1, keepdims=True)
    return o * lax.rsqrt(ms + EPS) * sub_row * (1.0 - lam_init)


def _attn_kernel(q_ref, kt_ref, v_ref, lamp_ref, sub_ref, o_ref,
                 kb_scr, vb_scr, qs_scr, m_scr, acc_scr, *, tq, lam_init):
    qi = pl.program_id(2)

    @pl.when(qi == 0)
    def _():
        kb_scr[...] = kt_ref[0].astype(BF16)
        vb_scr[:, 0:LANES] = v_ref[0].astype(BF16)
        vb_scr[:, LANES:2 * LANES] = jnp.ones((vb_scr.shape[0], LANES), BF16)

    q = q_ref[0].astype(F32)
    lane = lax.broadcasted_iota(jnp.int32, (tq, LANES), 1)
    for g in range(2):
        qg = q[:, g * LANES:(g + 1) * LANES]
        for c in range(2):
            keep = (lane < ATT_DHALF) if c == 0 else (lane >= ATT_DHALF)
            r = g * 2 + c
            qs_scr[r * tq:(r + 1) * tq, :] = jnp.where(keep, qg, 0.0).astype(BF16)
    m_scr[...] = jnp.full(m_scr.shape, -jnp.inf, F32)
    acc_scr[...] = jnp.zeros(acc_scr.shape, F32)

    def chunk(j, masked):
        off = pl.multiple_of(j * tq, tq)
        s = _dot(qs_scr[...], kb_scr[:, pl.ds(off, tq)])
        if masked:
            row = lax.rem(lax.broadcasted_iota(jnp.int32, s.shape, 0), tq)
            s = jnp.where(lax.broadcasted_iota(jnp.int32, s.shape, 1) <= row, s, -jnp.inf)
        m_prev = m_scr[...]
        m_new = jnp.maximum(m_prev, jnp.max(s, axis=-1, keepdims=True))
        alpha = jnp.exp2(m_prev - m_new)
        pr = jnp.exp2(s - jnp.concatenate([m_new] * (tq // LANES), axis=1))
        pv = _dot(pr.astype(BF16), vb_scr[pl.ds(off, tq), :])
        acc_scr[...] = jnp.concatenate([alpha, alpha], axis=1) * acc_scr[...] + pv
        m_scr[...] = m_new

    def body(j, carry):
        chunk(j, False)
        return carry

    lax.fori_loop(0, qi, body, 0)
    chunk(qi, True)

    lam = _diff_lambda(lamp_ref[...], lam_init)
    acc = acc_scr[...]
    out = acc[:, 0:LANES] / acc[:, LANES:2 * LANES]
    for g in range(2):
        o0 = out[(2 * g) * tq:(2 * g + 1) * tq]
        o1 = out[(2 * g + 1) * tq:(2 * g + 2) * tq]
        o_ref[0, :, g * LANES:(g + 1) * LANES] = _sub_norm(o0 - lam * o1, sub_ref[...], lam_init).astype(BF16)


def _attn_prompt(q, kt, v, lam_p, sub_row, lam_init, tq):
    b, t, _ = q.shape
    return pl.pallas_call(
        functools.partial(_attn_kernel, tq=tq, lam_init=lam_init),
        grid=(b, ATT_KV_HEADS, t // tq),
        in_specs=[
            pl.BlockSpec((1, tq, 256), lambda bi, n, qi: (bi, qi, n)),
            pl.BlockSpec((1, LANES, t), lambda bi, n, qi: (bi, n, 0)),
            pl.BlockSpec((1, t, LANES), lambda bi, n, qi: (bi, 0, n)),
            pl.BlockSpec((4, ATT_DHALF), lambda bi, n, qi: (0, 0)),
            pl.BlockSpec((1, LANES), lambda bi, n, qi: (0, 0)),
        ],
        out_specs=pl.BlockSpec((1, tq, 256), lambda bi, n, qi: (bi, qi, n)),
        out_shape=jax.ShapeDtypeStruct((b, t, 1024), BF16),
        scratch_shapes=[pltpu.VMEM((LANES, t), BF16), pltpu.VMEM((t, 2 * LANES), BF16),
                        pltpu.VMEM((4 * tq, LANES), BF16), pltpu.VMEM((4 * tq, LANES), F32),
                        pltpu.VMEM((4 * tq, 2 * LANES), F32)],
        compiler_params=_params("parallel", "parallel", "arbitrary"),
        name="attn_prompt",
    )(q, kt, v, lam_p, sub_row)


def _hgrn_tables(c, levels):
    t = np.arange(c)[:, None]
    u = np.arange(c)[None, :]
    w = [(u <= t), (u > t)]
    masks = [(t == u)]
    for l in range(1, levels + 1):
        n = 1 << l
        half = n // 2
        mid = (t // n) * n + half
        hi = t >= mid
        w.append(np.where(hi, (u > mid) & (u <= t), (u > t) & (u <= mid)))
        masks.append((t // n == u // n) & (t % n >= half) & (u % n < half))
    w = np.concatenate(w, axis=0).astype(np.float32)
    w3 = np.concatenate([w, w, w], axis=1)
    if len(masks) % 2:
        masks.append(np.zeros_like(masks[0]))
    pairs = [np.concatenate([masks[i], masks[i + 1]], axis=1) for i in range(0, len(masks), 2)]
    return w3, np.stack(pairs).astype(np.float32)


def _hgrn_kernel(q_ref, g_ref, k_ref, v_ref, og_ref, norm_ref, w_ref, mask_ref, o_ref, s_ref,
                 st_scr, *, c, levels, n_chunks):
    ti = pl.program_id(2)

    @pl.when(ti == 0)
    def _():
        st_scr[...] = jnp.zeros(st_scr.shape, F32)

    w = w_ref[...]
    zeros = jnp.zeros((c, LANES), BF16)
    chunk_rows = [slice(ci * c, (ci + 1) * c) for ci in range(n_chunks)]
    g3 = jnp.concatenate([jnp.concatenate(_split3(g_ref[0, rows, :]), axis=0) for rows in chunk_rows], axis=1)
    e_all = jnp.exp(_dot(w, g3))

    parts = []
    for ci, rows in enumerate(chunk_rows):
        q = q_ref[0, rows, :].astype(F32)
        k = k_ref[0, rows, :]
        v = v_ref[0, rows, :]
        e = e_all[:, ci * LANES:(ci + 1) * LANES]
        e_b = e[0:c]
        e_k = e[c:2 * c]
        qs = [q.astype(BF16)]
        ks = [k.astype(BF16)]
        for l in range(1, levels + 1):
            e_l = e[(l + 1) * c:(l + 2) * c]
            qs.append((q * e_l).astype(BF16))
            ks.append((k * e_l).astype(BF16))
        if len(qs) % 2:
            qs.append(zeros)
            ks.append(zeros)
        a2 = None
        for p in range(len(qs) // 2):
            ql = jnp.concatenate([qs[2 * p], qs[2 * p + 1]], axis=1)
            kbd = jnp.concatenate([jnp.concatenate([ks[2 * p], zeros], axis=1),
                                   jnp.concatenate([zeros, ks[2 * p + 1]], axis=1)], axis=0)
            term = mask_ref[p] * _dot_nt(ql, kbd)
            a2 = term if a2 is None else a2 + term
        o_intra = _dot(a2.astype(BF16), jnp.concatenate([v, v], axis=0))
        st_add = _dot_tn(v, (k * e_k).astype(BF16))
        parts.append(((q * e_b).astype(BF16), e_b[c - 1:c, :], o_intra, st_add))

    st = st_scr[...]
    for rows, (q_b, decay, o_intra, st_add) in zip(chunk_rows, parts):
        o = _dot_nt(q_b, st.astype(BF16)) + o_intra
        st = st * decay + st_add
        ms = jnp.mean(o * o, axis=-1, keepdims=True)
        y = o * lax.rsqrt(ms + EPS) * norm_ref[...] * og_ref[0, rows, :]
        o_ref[0, rows, :] = y.astype(BF16)
    st_scr[...] = st

    @pl.when(ti == pl.num_programs(2) - 1)
    def _():
        s_ref[0, 0] = st.T


def _hgrn_prompt(hq, logf, kk, hv, og, norm_row, ct):
    b, t, _ = hq.shape
    c, levels = HG_CHUNK, HG_LEVELS
    w_np, m_np = _hgrn_tables(c, levels)
    w = jnp.asarray(w_np, BF16)
    masks = jnp.asarray(m_np, F32)
    tok = pl.BlockSpec((1, ct, LANES), lambda bi, h, ti: (bi, ti, h))
    return pl.pallas_call(
        functools.partial(_hgrn_kernel, c=c, levels=levels, n_chunks=ct // c),
        grid=(b, HG_HEADS, t // ct),
        in_specs=[tok, tok, tok, tok, tok,
                  pl.BlockSpec((1, LANES), lambda bi, h, ti: (0, 0)),
                  pl.BlockSpec(w.shape, lambda bi, h, ti: (0, 0)),
                  pl.BlockSpec(masks.shape, lambda bi, h, ti: (0, 0, 0))],
        out_specs=(tok, pl.BlockSpec((1, 1, HG_DK, HG_DV), lambda bi, h, ti: (bi, h, 0, 0))),
        out_shape=(jax.ShapeDtypeStruct((b, t, 1024), BF16),
                   jax.ShapeDtypeStruct((b, HG_HEADS, HG_DK, HG_DV), F32)),
        scratch_shapes=[pltpu.VMEM((HG_DV, HG_DK), F32)],
        compiler_params=_params("parallel", "parallel", "arbitrary"),
        name="hgrn_prompt",
    )(hq, logf, kk, hv, og, norm_row, w, masks)


def _dec_attn_kernel(pt_ref, wq_ref, kn_ref, vn_ref, lamp_ref, sub_ref, *refs, pages, lam_init):
    k_refs = refs[:pages]
    v_refs = refs[pages:2 * pages]
    o_ref = refs[2 * pages]
    m_scr, l_scr, acc_scr = refs[2 * pages + 1:]
    j = pl.program_id(1)
    page = k_refs[0].shape[2]

    @pl.when(j == 0)
    def _():
        m_scr[...] = jnp.full(m_scr.shape, -jnp.inf, F32)
        l_scr[...] = jnp.zeros(l_scr.shape, F32)
        acc_scr[...] = jnp.zeros(acc_scr.shape, F32)

    rown = lax.rem(lax.broadcasted_iota(jnp.int32, (16, LANES), 0), 8) // 2
    wq = wq_ref[0]
    s = jnp.concatenate([_dot(wq, k_refs[i][0].astype(BF16)) for i in range(pages)], axis=1)
    m_prev = m_scr[...]
    m_new = jnp.maximum(m_prev, jnp.max(s, axis=-1, keepdims=True))
    alpha = jnp.exp2(m_prev - m_new)
    pr = jnp.exp2(s - m_new)
    l_scr[...] = alpha * l_scr[...] + jnp.sum(pr, axis=-1, keepdims=True)
    prb = pr.astype(BF16)
    pv = jnp.zeros((16, LANES), F32)
    for n in range(ATT_KV_HEADS):
        v_n = jnp.concatenate([v_refs[i][0, pl.ds(n, page, stride=ATT_KV_HEADS), :] for i in range(pages)],
                              axis=0).astype(BF16)
        pv = pv + jnp.where(rown == n, _dot(prb, v_n), 0.0)
    acc_scr[...] = alpha * acc_scr[...] + pv
    m_scr[...] = m_new

    @pl.when(j == pl.num_programs(1) - 1)
    def _():
        s_cur = jnp.sum(wq.astype(F32) * kn_ref[0], axis=-1, keepdims=True)
        m_old = m_scr[...]
        m_fin = jnp.maximum(m_old, s_cur)
        al = jnp.exp2(m_old - m_fin)
        pc = jnp.exp2(s_cur - m_fin)
        l_fin = al * l_scr[...] + pc
        vn = vn_ref[0]
        v_cur = jnp.zeros((16, LANES), F32)
        for n in range(ATT_KV_HEADS):
            v_cur = v_cur + jnp.where(rown == n, vn[:, n * LANES:(n + 1) * LANES], 0.0)
        out = (al * acc_scr[...] + pc * v_cur) / l_fin
        lam = _diff_lambda(lamp_ref[...], lam_init)
        o = out[0:8] - lam * out[8:16]
        o_ref[0] = _sub_norm(o, sub_ref[...], lam_init)


def _attn_sample(wq, k_new, v_new, kt_pages, v_pages, page_table, lam_p, sub_row, lam_init, pages):
    db, n_pages = page_table.shape
    pt_flat = page_table.reshape(-1)
    blk = kt_pages.shape[1:]

    def page_spec(i):
        return pl.BlockSpec((1,) + blk, lambda bi, j, pt, i=i: (pt[bi * n_pages + j * pages + i], 0, 0))

    row3 = lambda w: pl.BlockSpec((1, 1, w), lambda bi, j, pt: (bi, 0, 0))
    grid_spec = pltpu.PrefetchScalarGridSpec(
        num_scalar_prefetch=1,
        grid=(db, n_pages // pages),
        in_specs=[pl.BlockSpec((1, 16, 512), lambda bi, j, pt: (bi, 0, 0)), row3(512), row3(512),
                  pl.BlockSpec((4, ATT_DHALF), lambda bi, j, pt: (0, 0)),
                  pl.BlockSpec((1, LANES), lambda bi, j, pt: (0, 0))]
                 + [page_spec(i) for i in range(pages)] + [page_spec(i) for i in range(pages)],
        out_specs=pl.BlockSpec((1, 8, LANES), lambda bi, j, pt: (bi, 0, 0)),
        scratch_shapes=[pltpu.VMEM((16, 1), F32), pltpu.VMEM((16, 1), F32), pltpu.VMEM((16, LANES), F32)],
    )
    return pl.pallas_call(
        functools.partial(_dec_attn_kernel, pages=pages, lam_init=lam_init),
        grid_spec=grid_spec,
        out_shape=jax.ShapeDtypeStruct((db, 8, LANES), F32),
        compiler_params=_params("parallel", "arbitrary"),
        name="attn_sample",
    )(pt_flat, wq, k_new, v_new, lam_p, sub_row, *([kt_pages] * pages), *([v_pages] * pages))


def _hgrn_step_kernel(s_ref, q_ref, g_ref, k_ref, v_ref, og_ref, norm_ref, o_ref, sn_ref):
    for h in range(HG_HEADS):
        s = s_ref[0, h]
        f = jnp.exp(g_ref[0, h])
        qc = q_ref[0, h]
        kc = k_ref[0, h]
        v = v_ref[0, h]
        o = jnp.sum(s * (qc * f), axis=0, keepdims=True) + jnp.sum(qc * kc, axis=0, keepdims=True) * v
        sn_ref[0, h] = s * f + kc * v
        ms = jnp.mean(o * o, axis=-1, keepdims=True)
        o_ref[0, h] = o * lax.rsqrt(ms + EPS) * norm_ref[...] * og_ref[0, h]


def _hgrn_sample(state, qcol, gcol, kcol, vrow, ogrow, norm_row):
    db = state.shape[0]
    col = pl.BlockSpec((1, HG_HEADS, HG_DK, 1), lambda bi: (bi, 0, 0, 0))
    row = pl.BlockSpec((1, HG_HEADS, 1, HG_DV), lambda bi: (bi, 0, 0, 0))
    st = pl.BlockSpec((1, HG_HEADS, HG_DK, HG_DV), lambda bi: (bi, 0, 0, 0))
    return pl.pallas_call(
        _hgrn_step_kernel,
        grid=(db,),
        in_specs=[st, col, col, col, row, row, pl.BlockSpec((1, LANES), lambda bi: (0, 0))],
        out_specs=(row, st),
        out_shape=(jax.ShapeDtypeStruct((db, HG_HEADS, 1, HG_DV), F32),
                   jax.ShapeDtypeStruct(state.shape, F32)),
        compiler_params=_params("parallel"),
        name="hgrn_sample",
    )(state, qcol, gcol, kcol, vrow, ogrow, norm_row)


def _lane_first(cond, lane):
    return jnp.min(jnp.where(cond, lane, LANES), axis=-1, keepdims=True)


def _route(logits):
    lane = lax.broadcasted_iota(jnp.int32, logits.shape, 1)
    gmask = (lane >= N_EXPERTS) & (lane < N_EXPERTS + N_GROUPS)
    lg = jnp.where(gmask, logits, -jnp.inf)
    mg = jnp.max(lg, axis=-1, keepdims=True)
    eg = jnp.exp(lg - mg)
    pg = eg / jnp.sum(eg, axis=-1, keepdims=True)
    pg_top = jnp.max(pg, axis=-1, keepdims=True)
    gsel = _lane_first(gmask & (pg == pg_top), lane) - N_EXPERTS
    emask = (lane >= gsel * EXP_PER_GROUP) & (lane < (gsel + 1) * EXP_PER_GROUP)
    le = jnp.where(emask, logits, -jnp.inf)
    me = jnp.max(le, axis=-1, keepdims=True)
    ee = jnp.exp(le - me)
    pe = ee / jnp.sum(ee, axis=-1, keepdims=True)
    pe = jnp.where(emask, pe, -1.0)
    p1 = jnp.max(pe, axis=-1, keepdims=True)
    e1 = _lane_first(pe == p1, lane)
    pe2 = jnp.where(lane == e1, -1.0, pe)
    p2 = jnp.max(pe2, axis=-1, keepdims=True)
    e2 = _lane_first(pe2 == p2, lane)
    den = p1 + p2
    w1 = p1 / den * pg_top
    w2 = p2 / den * pg_top
    comb = jnp.where(lane == e1, w1, 0.0) + jnp.where(lane == e2, w2, 0.0)
    picks = jnp.where(lane == 0, e1.astype(F32), jnp.where(lane == 1, e2.astype(F32),
                      jnp.where(lane == 2, w1, jnp.where(lane == 3, w2, 0.0))))
    return comb, picks


ROW_TILES = D_MODEL // LANES


def _merge_kernel(oa_ref, oh_ref, sga_ref, sgh_ref, x_ref, wpa_ref, wph_ref, wo_ref, gf_ref,
                  wr_ref, br_ref, x1_ref, xn_ref, route_ref, *, sparse):
    ya = _dot(oa_ref[...], wpa_ref[...])
    yh = _dot(oh_ref[...], wph_ref[...])
    mixed = (sga_ref[...] * ya + sgh_ref[...] * yh).astype(BF16)
    x1 = x_ref[...] + _dot(mixed, wo_ref[...])
    x1_ref[...] = x1
    ms = jnp.mean(x1 * x1, axis=-1, keepdims=True)
    xn_f = x1 * lax.rsqrt(ms + EPS) * gf_ref[...]
    xn = xn_f.astype(BF16)
    comb, picks = _route(_dot(xn, wr_ref[...]) + br_ref[...])
    if sparse:
        tm = x1.shape[0]
        for s in range(ROW_TILES):
            xn_ref[pl.ds(s, tm, stride=ROW_TILES), :] = xn_f[:, s * LANES:(s + 1) * LANES]
        route_ref[...] = picks
    else:
        xn_ref[...] = xn
        route_ref[...] = comb


def _merge(oa, oh, sga, sgh, x, wpa, wph, wo, gf, wr, br, tm, sparse):
    n = x.shape[0]
    tok = pl.BlockSpec((tm, 1024), lambda i: (i, 0))
    wsp = pl.BlockSpec((1024, 1024), lambda i: (0, 0))
    if sparse:
        xn_spec = pl.BlockSpec((tm * ROW_TILES, LANES), lambda i: (i, 0))
        xn_shape = jax.ShapeDtypeStruct((n * ROW_TILES, LANES), F32)
    else:
        xn_spec, xn_shape = tok, jax.ShapeDtypeStruct((n, 1024), BF16)
    return pl.pallas_call(
        functools.partial(_merge_kernel, sparse=sparse),
        grid=(n // tm,),
        in_specs=[tok, tok, tok, tok, tok, wsp, wsp, wsp,
                  pl.BlockSpec((1, 1024), lambda i: (0, 0)),
                  pl.BlockSpec((1024, LANES), lambda i: (0, 0)),
                  pl.BlockSpec((1, LANES), lambda i: (0, 0))],
        out_specs=(tok, xn_spec, pl.BlockSpec((tm, LANES), lambda i: (i, 0))),
        out_shape=(jax.ShapeDtypeStruct((n, 1024), F32), xn_shape,
                   jax.ShapeDtypeStruct((n, LANES), F32)),
        compiler_params=_params("parallel"),
        name="merge",
    )(oa, oh, sga, sgh, x, wpa, wph, wo, gf, wr, br)


def _moe_kernel(xn_ref, comb_ref, x1_ref, wg_ref, wu_ref, wd_ref, y_ref, acc_scr):
    e = pl.program_id(1)

    @pl.when(e == 0)
    def _():
        acc_scr[...] = jnp.zeros(acc_scr.shape, F32)

    comb = comb_ref[...]
    lane = lax.broadcasted_iota(jnp.int32, comb.shape, 1)
    c = jnp.sum(jnp.where(lane == e, comb, 0.0), axis=-1, keepdims=True)
    xn = xn_ref[...]
    hg = _dot(xn, wg_ref[0])
    hu = _dot(xn, wu_ref[0])
    hid = hg * jax.nn.sigmoid(hg) * hu
    acc_scr[...] += _dot((hid * c).astype(BF16), wd_ref[0])

    @pl.when(e == pl.num_programs(1) - 1)
    def _():
        y_ref[...] = x1_ref[...] + acc_scr[...]


def _moe(xn, comb, x1, wg, wu, wd, tm):
    n = xn.shape[0]
    return pl.pallas_call(
        _moe_kernel,
        grid=(n // tm, N_EXPERTS),
        in_specs=[pl.BlockSpec((tm, 1024), lambda i, e: (i, 0)),
                  pl.BlockSpec((tm, LANES), lambda i, e: (i, 0)),
                  pl.BlockSpec((tm, 1024), lambda i, e: (i, 0)),
                  pl.BlockSpec((1, 1024, D_FF), lambda i, e: (e, 0, 0)),
                  pl.BlockSpec((1, 1024, D_FF), lambda i, e: (e, 0, 0)),
                  pl.BlockSpec((1, D_FF, 1024), lambda i, e: (e, 0, 0))],
        out_specs=pl.BlockSpec((tm, 1024), lambda i, e: (i, 0)),
        out_shape=jax.ShapeDtypeStruct((n, 1024), F32),
        scratch_shapes=[pltpu.VMEM((tm, 1024), F32)],
        compiler_params=_params("parallel", "arbitrary"),
        name="moe",
    )(xn, comb, x1, wg, wu, wd)


def _row_copy(src_hbm, src_row, dst, dst_row, sem):
    return pltpu.make_async_copy(src_hbm.at[pl.ds(src_row * ROW_TILES, ROW_TILES)],
                                 dst.at[pl.ds(dst_row * ROW_TILES, ROW_TILES)], sem)


def _gather_rows(idx_ref, src_hbm, dst, sem, n_rows):
    def body(r, carry):
        _row_copy(src_hbm, idx_ref[r], dst, r, sem).start()
        return carry
    lax.fori_loop(0, n_rows, body, 0)


def _wait_rows(src_hbm, dst, sem):
    pltpu.make_async_copy(src_hbm.at[pl.ds(0, dst.shape[0])], dst, sem).wait()


def _rows_to_mat(buf, n_rows):
    return jnp.concatenate([buf[pl.ds(s, n_rows, stride=ROW_TILES), :] for s in range(ROW_TILES)], axis=1)


def _moe_scatter_kernel(p1_ref, p2_ref, xn_ref, xs_in, xs_hbm, sem):
    del xs_in
    i = pl.program_id(0)
    tm = p1_ref.shape[0]

    def wait_tile():
        for _ in range(2):
            pltpu.make_async_copy(xn_ref, xs_hbm.at[pl.ds(0, tm * ROW_TILES)], sem.at[0]).wait()

    def body(r, carry):
        src = xn_ref.at[pl.ds(r * ROW_TILES, ROW_TILES)]
        pltpu.make_async_copy(src, xs_hbm.at[pl.ds(p1_ref[r] * ROW_TILES, ROW_TILES)], sem.at[0]).start()
        pltpu.make_async_copy(src, xs_hbm.at[pl.ds(p2_ref[r] * ROW_TILES, ROW_TILES)], sem.at[0]).start()
        return carry

    lax.fori_loop(0, tm, body, 0)
    wait_tile()


def _moe_scatter(xg, pos1, pos2, rows, tm):
    n = pos1.shape[0]
    idx = pl.BlockSpec((tm,), lambda i: (i,), memory_space=pltpu.SMEM)
    xs0 = jnp.zeros((rows * ROW_TILES, LANES), F32)
    return pl.pallas_call(
        _moe_scatter_kernel,
        grid=(n // tm,),
        in_specs=[idx, idx, pl.BlockSpec((tm * ROW_TILES, LANES), lambda i: (i, 0)),
                  pl.BlockSpec(memory_space=pl.ANY)],
        out_specs=pl.BlockSpec(memory_space=pl.ANY),
        out_shape=jax.ShapeDtypeStruct(xs0.shape, F32),
        scratch_shapes=[pltpu.SemaphoreType.DMA((1,))],
        input_output_aliases={3: 0},
        compiler_params=_params("arbitrary"),
        name="moe_scatter",
    )(pos1, pos2, xg, xs0)


def _moe_ffn_kernel(te_ref, xs_ref, wg_ref, wu_ref, wd_ref, ys_ref, *, tp):
    del te_ref
    x = _rows_to_mat(xs_ref, tp).astype(BF16)
    hg = _dot(x, wg_ref[0])
    hu = _dot(x, wu_ref[0])
    hid = (hg * jax.nn.sigmoid(hg) * hu).astype(BF16)
    out = _dot(hid, wd_ref[0])
    for s in range(ROW_TILES):
        ys_ref[pl.ds(s, tp, stride=ROW_TILES), :] = out[:, s * LANES:(s + 1) * LANES]


def _moe_ffn(xs, tile_expert, wg, wu, wd, tp):
    n_tiles = tile_expert.shape[0]
    rows = pl.BlockSpec((tp * ROW_TILES, LANES), lambda i, te: (i, 0))
    wsp = lambda a, b: pl.BlockSpec((1, a, b), lambda i, te: (te[i], 0, 0))
    grid_spec = pltpu.PrefetchScalarGridSpec(
        num_scalar_prefetch=1,
        grid=(n_tiles,),
        in_specs=[rows, wsp(1024, D_FF), wsp(1024, D_FF), wsp(D_FF, 1024)],
        out_specs=rows,
    )
    return pl.pallas_call(
        functools.partial(_moe_ffn_kernel, tp=tp),
        grid_spec=grid_spec,
        out_shape=jax.ShapeDtypeStruct(xs.shape, F32),
        compiler_params=_params("parallel"),
        name="moe_ffn",
    )(tile_expert, xs, wg, wu, wd)


def _moe_combine_kernel(p1_ref, p1n_ref, p2_ref, p2n_ref, ys_hbm, x1_ref, w1_ref, w2_ref, y_ref, buf, sem, *, tm):
    i = pl.program_id(0)
    slot = lax.rem(i, 2)

    def gather(a_ref, b_ref, s):
        _gather_rows(a_ref, ys_hbm, buf.at[s, 0], sem.at[s, 0], tm)
        _gather_rows(b_ref, ys_hbm, buf.at[s, 1], sem.at[s, 1], tm)

    @pl.when(i == 0)
    def _():
        gather(p1_ref, p2_ref, 0)

    @pl.when(i + 1 < pl.num_programs(0))
    def _():
        gather(p1n_ref, p2n_ref, 1 - slot)

    _wait_rows(ys_hbm, buf.at[slot, 0], sem.at[slot, 0])
    _wait_rows(ys_hbm, buf.at[slot, 1], sem.at[slot, 1])
    y_ref[...] = x1_ref[...] + (w1_ref[...] * _rows_to_mat(buf.at[slot, 0], tm)
                                + w2_ref[...] * _rows_to_mat(buf.at[slot, 1], tm))


def _moe_combine(ys, pos1, pos2, w1, w2, x1, tm):
    n = x1.shape[0]
    n_tiles = n // tm
    idx = lambda off: pl.BlockSpec((tm,), lambda i: (jnp.minimum(i + off, n_tiles - 1),),
                                   memory_space=pltpu.SMEM)
    col = pl.BlockSpec((tm, 1), lambda i: (i, 0))
    return pl.pallas_call(
        functools.partial(_moe_combine_kernel, tm=tm),
        grid=(n_tiles,),
        in_specs=[idx(0), idx(1), idx(0), idx(1), pl.BlockSpec(memory_space=pl.ANY),
                  pl.BlockSpec((tm, 1024), lambda i: (i, 0)), col, col],
        out_specs=pl.BlockSpec((tm, 1024), lambda i: (i, 0)),
        out_shape=jax.ShapeDtypeStruct((n, 1024), F32),
        scratch_shapes=[pltpu.VMEM((2, 2, tm * ROW_TILES, LANES), F32), pltpu.SemaphoreType.DMA((2, 2))],
        compiler_params=_params("arbitrary"),
        name="moe_combine",
    )(pos1, pos1, pos2, pos2, ys, x1, w1, w2)


def _dispatch(picks, tp):
    n = picks.shape[0]
    e_pair = jnp.concatenate([picks[:, 0], picks[:, 1]]).astype(jnp.int32)
    onehot = (e_pair[:, None] == jnp.arange(N_EXPERTS, dtype=jnp.int32)[None, :]).astype(jnp.int32)
    counts = jnp.sum(onehot, axis=0)
    rank = jnp.sum(jnp.cumsum(onehot, axis=0) * onehot, axis=1) - 1
    tiles = (counts + tp - 1) // tp
    tile_end = jnp.cumsum(tiles)
    row_start = (tile_end - tiles) * tp
    pos = row_start[e_pair] + rank
    n_tiles = (2 * n) // tp + N_EXPERTS
    tile_expert = jnp.minimum(jnp.searchsorted(tile_end, jnp.arange(n_tiles, dtype=jnp.int32), side='right'),
                              N_EXPERTS - 1).astype(jnp.int32)
    return pos[:n], pos[n:], tile_expert, n_tiles * tp


def _pick(n, pref):
    t = min(n, pref)
    while n % t:
        t //= 2
    return t


def kernel(x_prompt, x_sample, cache_k, cache_v, state_hgrn, page_table, rms_in, w_in, att_q_norm, att_k_norm, att_lambda, att_sub_norm, hg_lower_bound, hg_out_norm, w_branch_att, w_branch_hg, w_out, rms_ffn, w_router_group, b_router_group, w_router_expert, b_router_expert, w_exp_gate, w_exp_up, w_exp_down):
    depth = rms_in.shape[0]
    assert depth == 1 and hg_lower_bound.shape[0] == 2
    b, t, _ = x_prompt.shape
    db, ds, _ = x_sample.shape
    assert ds == 1
    lam_init = 0.8 - 0.6 * math.exp(-0.3 * 0)

    w_in_bf = w_in[0].astype(BF16)
    qn_row = jnp.tile(att_q_norm[0], 2).reshape(1, LANES)
    kn_row = jnp.tile(att_k_norm[0], 2).reshape(1, LANES)
    seg_np = (np.arange(LANES)[:, None] // ATT_DHALF) == (np.arange(LANES)[None, :] // ATT_DHALF)
    seg = jnp.asarray(seg_np.astype(np.float32), BF16)
    sub_row = att_sub_norm[0].reshape(1, LANES)
    hgn_row = hg_out_norm[0].reshape(1, LANES)
    wpa = w_branch_att[0].astype(BF16)
    wph = w_branch_hg[0].astype(BF16)
    wo = w_out[0].astype(BF16)
    wr = jnp.zeros((D_MODEL, LANES), F32)
    wr = wr.at[:, :N_EXPERTS].set(w_router_expert[0]).at[:, N_EXPERTS:N_EXPERTS + N_GROUPS].set(w_router_group[0])
    wr = wr.astype(BF16)
    br = jnp.zeros((1, LANES), F32)
    br = br.at[0, :N_EXPERTS].set(b_router_expert[0]).at[0, N_EXPERTS:N_EXPERTS + N_GROUPS].set(b_router_group[0])
    weg = w_exp_gate[0].astype(BF16)
    weu = w_exp_up[0].astype(BF16)
    wed = w_exp_down[0].astype(BF16)
    gin = rms_in[0].reshape(1, D_MODEL)
    gffn = rms_ffn[0].reshape(1, D_MODEL)
    lam_p = att_lambda[0]

    def tail(x2, oa, oh, sga, sgh, tm_merge, tm_moe):
        sparse = x2.shape[0] >= 16 * MOE_TILE
        x1, xn, route = _merge(oa, oh, sga, sgh, x2, wpa, wph, wo, gffn, wr, br, tm_merge, sparse)
        if not sparse:
            return _moe(xn, route, x1, weg, weu, wed, tm_moe)
        pos1, pos2, tile_expert, rows = _dispatch(route, MOE_TILE)
        xs = _moe_scatter(xn, pos1, pos2, rows, _pick(x2.shape[0], 512))
        ys = _moe_ffn(xs, tile_expert, weg, weu, wed, MOE_TILE)
        return _moe_combine(ys, pos1, pos2, route[:, 2:3], route[:, 3:4], x1, _pick(x2.shape[0], 256))

    n = b * t
    xp = x_prompt.reshape(n, D_MODEL)
    q, kt, v, hq, logf, kk, hv, og, sga, sgh = _inproj(
        xp, gin, w_in_bf, qn_row, kn_row, hg_lower_bound, seg, _pick(t, 1024), (b, t))
    oa = _attn_prompt(q.reshape(b, t, 1024), kt, v.reshape(b, t, 512), lam_p, sub_row, lam_init, _pick(t, 512))
    r3 = lambda a: a.reshape(b, t, 1024)
    oh, state_p = _hgrn_prompt(r3(hq), r3(logf), r3(kk), r3(hv), r3(og), hgn_row, _pick(t, 512))
    yp = tail(xp, oa.reshape(n, 1024), oh.reshape(n, 1024), sga, sgh, _pick(n, 512), _pick(n, 1024))
    k_prompt = jnp.transpose(kt.reshape(1, b, ATT_KV_HEADS, 2, ATT_DHALF, t), (0, 1, 5, 2, 3, 4))

    xs = x_sample.reshape(db, D_MODEL)
    q, ks, vs, hq, logf, kk, hv, og, sga, sgh = _inproj(
        xs, gin, w_in_bf, qn_row, kn_row, hg_lower_bound, seg, _pick(db, 1024), None)
    q5 = q.reshape(db, ATT_KV_HEADS, 2, 2, ATT_DHALF)
    eye_n = jnp.eye(ATT_KV_HEADS, dtype=BF16)
    eye_c = jnp.eye(2, dtype=BF16)
    wq = jnp.einsum('bngcd,nm,ce->bcngmed', q5, eye_n, eye_c).reshape(db, 16, 512)
    n_pool = cache_k.shape[1]
    page = cache_k.shape[2]
    kt_pages = jnp.transpose(cache_k[0], (0, 2, 3, 4, 1)).reshape(n_pool, 512, page)
    v_pages = cache_v[0].reshape(n_pool, page * ATT_KV_HEADS, ATT_DV)
    oa_s = _attn_sample(wq, ks.reshape(db, 1, 512), vs.reshape(db, 1, 512), kt_pages, v_pages,
                        page_table, lam_p, sub_row, lam_init, _pick(page_table.shape[1], 16))
    col = lambda a: a.astype(F32).reshape(db, HG_HEADS, HG_DK, 1)
    row = lambda a: a.astype(F32).reshape(db, HG_HEADS, 1, HG_DV)
    oh_s, state_s = _hgrn_sample(state_hgrn[0], col(hq), col(logf), col(kk), row(hv), row(og), hgn_row)
    ys = tail(xs, oa_s.reshape(db, 1024).astype(BF16), oh_s.reshape(db, 1024).astype(BF16), sga, sgh,
              _pick(db, 512), _pick(db, 1024))

    return (yp.reshape(b, t, D_MODEL), ys.reshape(db, 1, D_MODEL),
            k_prompt, v.reshape(1, b, t, ATT_KV_HEADS, ATT_DV),
            state_p.reshape(1, b, HG_HEADS, HG_DK, HG_DV),
            ks.reshape(1, db, 1, ATT_KV_HEADS, 2, ATT_DHALF), vs.reshape(1, db, 1, ATT_KV_HEADS, ATT_DV),
            state_s.reshape(1, db, HG_HEADS, HG_DK, HG_DV))
```

```python
import functools
import math

import numpy as np
import jax
import jax.numpy as jnp
from jax import lax
from jax.experimental import pallas as pl
from jax.experimental.pallas import tpu as pltpu

F32 = jnp.float32
BF16 = jnp.bfloat16

D_MODEL = 1024
ATT_HEADS = 8
ATT_KV_HEADS = 4
ATT_DHALF = 64
ATT_DV = 128
HG_HEADS = 8
HG_DK = 128
HG_DV = 128
N_GROUPS = 4
EXP_PER_GROUP = 8
N_EXPERTS = 32
D_FF = 512
EPS = 1e-6
LANES = 128
VMEM_LIMIT = 56 * 1024 * 1024

LOG2E = 1.4426950408889634
Q_SCALE = (ATT_DHALF ** -0.5) * LOG2E

MOE_TILE = 512
HG_CHUNK = 64
HG_LEVELS = 6


def _dot(a, b):
    return jnp.dot(a, b, preferred_element_type=F32)


def _dot_nt(a, b):
    return lax.dot_general(a, b, (((1,), (1,)), ((), ())), preferred_element_type=F32)


def _dot_tn(a, b):
    return lax.dot_general(a, b, (((0,), (0,)), ((), ())), preferred_element_type=F32)


def _split3(x):
    hi = x.astype(BF16)
    r1 = x - hi.astype(F32)
    mid = r1.astype(BF16)
    lo = (r1 - mid.astype(F32)).astype(BF16)
    return hi, mid, lo


def _seg_rms(z, seg, gain_row, scale):
    outs = []
    for i in range(z.shape[1] // LANES):
        zi = z[:, i * LANES:(i + 1) * LANES]
        z2 = zi * zi
        hi = z2.astype(BF16)
        lo = (z2 - hi.astype(F32)).astype(BF16)
        ss = _dot(hi, seg) + _dot(lo, seg)
        y = zi * lax.rsqrt(ss * (1.0 / ATT_DHALF) + EPS) * gain_row
        if scale != 1.0:
            y = y * scale
        outs.append(y)
    return jnp.concatenate(outs, axis=1)


def _params(*sem):
    return pltpu.CompilerParams(dimension_semantics=sem, vmem_limit_bytes=VMEM_LIMIT)


def _rms_kernel(x_ref, g_ref, h_ref):
    x = x_ref[...]
    ms = jnp.mean(x * x, axis=-1, keepdims=True)
    h_ref[...] = (x * lax.rsqrt(ms + EPS) * g_ref[...]).astype(BF16)


def _rms_cast(x, g, tm):
    n = x.shape[0]
    return pl.pallas_call(
        _rms_kernel,
        grid=(n // tm,),
        in_specs=[pl.BlockSpec((tm, D_MODEL), lambda i: (i, 0)), pl.BlockSpec((1, D_MODEL), lambda i: (0, 0))],
        out_specs=pl.BlockSpec((tm, D_MODEL), lambda i: (i, 0)),
        out_shape=jax.ShapeDtypeStruct((n, D_MODEL), BF16),
        compiler_params=_params("parallel"),
        name="rms_in",
    )(x, g)


def _proj_kernel(h_ref, w_ref, *refs, mode, n_aux):
    aux, outs = refs[:n_aux], refs[n_aux:]
    z = _dot(h_ref[...], w_ref[...])
    if mode == "q":
        outs[0][...] = _seg_rms(z, aux[0][...], aux[1][...], Q_SCALE).astype(BF16)
    elif mode in ("kv", "kv_t"):
        k = _seg_rms(z[:, :512], aux[0][...], aux[1][...], 1.0)
        if mode == "kv_t":
            outs[0][0] = k.T
            tm = z.shape[0]
            for n in range(ATT_KV_HEADS):
                outs[1][pl.ds(n, tm, stride=ATT_KV_HEADS), :] = z[:, 512 + n * LANES:512 + (n + 1) * LANES]
        else:
            outs[0][...] = k
            outs[1][...] = z[:, 512:]
    elif mode == "hq":
        outs[0][...] = (z * (HG_DK ** -0.5)).astype(BF16)
    elif mode == "hf":
        lbp = aux[0][...]
        m = jnp.max(lbp, axis=0, keepdims=True)
        e = jnp.exp(lbp - m)
        lb = e[0:1] / jnp.sum(e, axis=0, keepdims=True)
        f = lb + (1.0 - lb) * jax.nn.sigmoid(z)
        outs[0][...] = jnp.log(f)
        outs[1][...] = (1.0 - lb) * jax.nn.sigmoid(-z)
    elif mode == "hv":
        outs[0][...] = z.astype(BF16)
    elif mode == "silu":
        outs[0][...] = z * jax.nn.sigmoid(z)
    elif mode == "sigmoid":
        outs[0][...] = jax.nn.sigmoid(z)


def _proj(h, w, col, mode, aux, outs, tm, name):
    n = h.shape[0]
    aux_specs = [pl.BlockSpec(a.shape, lambda i: (0, 0)) for a in aux]
    res = pl.pallas_call(
        functools.partial(_proj_kernel, mode=mode, n_aux=len(aux)),
        grid=(n // tm,),
        in_specs=[pl.BlockSpec((tm, D_MODEL), lambda i: (i, 0)),
                  pl.BlockSpec((D_MODEL, 1024), lambda i, col=col: (0, col))] + aux_specs,
        out_specs=tuple(pl.BlockSpec(o[2], o[3]) for o in outs),
        out_shape=tuple(jax.ShapeDtypeStruct(o[0], o[1]) for o in outs),
        compiler_params=_params("parallel"),
        name=name,
    )(h, w, *aux)
    return res


def _inproj(x, gin, w, qn_row, kn_row, lbp, seg, tm, seq):
    n = x.shape[0]
    h = _rms_cast(x, gin, tm)
    tok = lambda width, dt: ((n, width), dt, (tm, width), lambda i: (i, 0))
    q, = _proj(h, w, 0, "q", [seg, qn_row], [tok(1024, BF16)], tm, "proj_q")
    if seq is None:
        k, v = _proj(h, w, 1, "kv", [seg, kn_row], [tok(512, F32), tok(512, F32)], tm, "proj_kv")
    else:
        b, t = seq
        tpb = t // tm
        k_out = ((b, 512, t), F32, (1, 512, tm), lambda i: (i // tpb, 0, i % tpb))
        v_out = ((n * ATT_KV_HEADS, LANES), F32, (tm * ATT_KV_HEADS, LANES), lambda i: (i, 0))
        k, v = _proj(h, w, 1, "kv_t", [seg, kn_row], [k_out, v_out], tm, "proj_kv")
    hq, = _proj(h, w, 2, "hq", [], [tok(1024, BF16)], tm, "proj_hq")
    logf, kk = _proj(h, w, 3, "hf", [lbp], [tok(1024, F32), tok(1024, F32)], tm, "proj_hf")
    hv, = _proj(h, w, 4, "hv", [], [tok(1024, BF16)], tm, "proj_hv")
    og, = _proj(h, w, 5, "silu", [], [tok(1024, F32)], tm, "proj_og")
    sga, = _proj(h, w, 6, "sigmoid", [], [tok(1024, F32)], tm, "proj_ga")
    sgh, = _proj(h, w, 7, "sigmoid", [], [tok(1024, F32)], tm, "proj_gh")
    return q, k, v, hq, logf, kk, hv, og, sga, sgh


def _diff_lambda(lp, lam_init):
    a = jnp.sum(lp[0:1] * lp[1:2], axis=-1, keepdims=True)
    b = jnp.sum(lp[2:3] * lp[3:4], axis=-1, keepdims=True)
    return jnp.exp(a) - jnp.exp(b) + lam_init


def _sub_norm(o, sub_row, lam_init):
    ms = jnp.mean(o * o, axis=-1, keepdims=True)
    return o * lax.rsqrt(ms + EPS) * sub_row * (1.0 - lam_init)


def _attn_kernel(q_ref, kt_ref, v_ref, lamp_ref, sub_ref, o_ref,
                 kb_scr, vb_scr, qs_scr, m_scr, acc_scr, *, tq, lam_init):
    qi = pl.program_id(2)

    @pl.when(qi == 0)
    def _():
        kb_scr[...] = kt_ref[0].astype(BF16)
        t_len = vb_scr.shape[0]
        vb_scr[:, 0:LANES] = v_ref[0, pl.ds(pl.program_id(1), t_len, stride=ATT_KV_HEADS), :].astype(BF16)
        vb_scr[:, LANES:2 * LANES] = jnp.ones((vb_scr.shape[0], LANES), BF16)

    q = q_ref[0].astype(F32)
    lane = lax.broadcasted_iota(jnp.int32, (tq, LANES), 1)
    for g in range(2):
        qg = q[:, g * LANES:(g + 1) * LANES]
        for c in range(2):
            keep = (lane < ATT_DHALF) if c == 0 else (lane >= ATT_DHALF)
            r = g * 2 + c
            qs_scr[r * tq:(r + 1) * tq, :] = jnp.where(keep, qg, 0.0).astype(BF16)
    m_scr[...] = jnp.full(m_scr.shape, -jnp.inf, F32)
    acc_scr[...] = jnp.zeros(acc_scr.shape, F32)

    def chunk(j, masked):
        off = pl.multiple_of(j * tq, tq)
        s = _dot(qs_scr[...], kb_scr[:, pl.ds(off, tq)])
        if masked:
            row = lax.rem(lax.broadcasted_iota(jnp.int32, s.shape, 0), tq)
            s = jnp.where(lax.broadcasted_iota(jnp.int32, s.shape, 1) <= row, s, -jnp.inf)
        m_prev = m_scr[...]
        m_new = jnp.maximum(m_prev, jnp.max(s, axis=-1, keepdims=True))
        alpha = jnp.exp2(m_prev - m_new)
        pr = jnp.exp2(s - jnp.concatenate([m_new] * (tq // LANES), axis=1))
        pv = _dot(pr.astype(BF16), vb_scr[pl.ds(off, tq), :])
        acc_scr[...] = jnp.concatenate([alpha, alpha], axis=1) * acc_scr[...] + pv
        m_scr[...] = m_new

    def body(j, carry):
        chunk(j, False)
        return carry

    lax.fori_loop(0, qi, body, 0)
    chunk(qi, True)

    lam = _diff_lambda(lamp_ref[...], lam_init)
    acc = acc_scr[...]
    out = acc[:, 0:LANES] / acc[:, LANES:2 * LANES]
    for g in range(2):
        o0 = out[(2 * g) * tq:(2 * g + 1) * tq]
        o1 = out[(2 * g + 1) * tq:(2 * g + 2) * tq]
        o_ref[0, :, g * LANES:(g + 1) * LANES] = _sub_norm(o0 - lam * o1, sub_ref[...], lam_init).astype(BF16)


def _attn_prompt(q, kt, v, lam_p, sub_row, lam_init, tq):
    b, t, _ = q.shape
    return pl.pallas_call(
        functools.partial(_attn_kernel, tq=tq, lam_init=lam_init),
        grid=(b, ATT_KV_HEADS, t // tq),
        in_specs=[
            pl.BlockSpec((1, tq, 256), lambda bi, n, qi: (bi, qi, n)),
            pl.BlockSpec((1, LANES, t), lambda bi, n, qi: (bi, n, 0)),
            pl.BlockSpec((1, t * ATT_KV_HEADS, LANES), lambda bi, n, qi: (bi, 0, 0)),
            pl.BlockSpec((4, ATT_DHALF), lambda bi, n, qi: (0, 0)),
            pl.BlockSpec((1, LANES), lambda bi, n, qi: (0, 0)),
        ],
        out_specs=pl.BlockSpec((1, tq, 256), lambda bi, n, qi: (bi, qi, n)),
        out_shape=jax.ShapeDtypeStruct((b, t, 1024), BF16),
        scratch_shapes=[pltpu.VMEM((LANES, t), BF16), pltpu.VMEM((t, 2 * LANES), BF16),
                        pltpu.VMEM((4 * tq, LANES), BF16), pltpu.VMEM((4 * tq, LANES), F32),
                        pltpu.VMEM((4 * tq, 2 * LANES), F32)],
        compiler_params=_params("parallel", "parallel", "arbitrary"),
        name="attn_prompt",
    )(q, kt, v, lam_p, sub_row)


def _hgrn_tables(c, levels):
    t = np.arange(c)[:, None]
    u = np.arange(c)[None, :]
    w = [(u <= t), (u > t)]
    masks = [(t == u)]
    for l in range(1, levels + 1):
        n = 1 << l
        half = n // 2
        mid = (t // n) * n + half
        hi = t >= mid
        w.append(np.where(hi, (u > mid) & (u <= t), (u > t) & (u <= mid)))
        masks.append((t // n == u // n) & (t % n >= half) & (u % n < half))
    w = np.concatenate(w, axis=0).astype(np.float32)
    w3 = np.concatenate([w, w, w], axis=1)
    if len(masks) % 2:
        masks.append(np.zeros_like(masks[0]))
    pairs = [np.concatenate([masks[i], masks[i + 1]], axis=1) for i in range(0, len(masks), 2)]
    return w3, np.stack(pairs).astype(np.float32)


def _hgrn_kernel(q_ref, g_ref, k_ref, v_ref, og_ref, norm_ref, w_ref, mask_ref, o_ref, s_ref,
                 st_scr, *, c, levels, n_chunks):
    ti = pl.program_id(2)

    @pl.when(ti == 0)
    def _():
        st_scr[...] = jnp.zeros(st_scr.shape, F32)

    w = w_ref[...]
    zeros = jnp.zeros((c, LANES), BF16)
    chunk_rows = [slice(ci * c, (ci + 1) * c) for ci in range(n_chunks)]
    g3 = jnp.concatenate([jnp.concatenate(_split3(g_ref[0, rows, :]), axis=0) for rows in chunk_rows], axis=1)
    e_all = jnp.exp(_dot(w, g3))

    parts = []
    for ci, rows in enumerate(chunk_rows):
        q = q_ref[0, rows, :].astype(F32)
        k = k_ref[0, rows, :]
        v = v_ref[0, rows, :]
        e = e_all[:, ci * LANES:(ci + 1) * LANES]
        e_b = e[0:c]
        e_k = e[c:2 * c]
        qs = [q.astype(BF16)]
        ks = [k.astype(BF16)]
        for l in range(1, levels + 1):
            e_l = e[(l + 1) * c:(l + 2) * c]
            qs.append((q * e_l).astype(BF16))
            ks.append((k * e_l).astype(BF16))
        if len(qs) % 2:
            qs.append(zeros)
            ks.append(zeros)
        a2 = None
        for p in range(len(qs) // 2):
            ql = jnp.concatenate([qs[2 * p], qs[2 * p + 1]], axis=1)
            kbd = jnp.concatenate([jnp.concatenate([ks[2 * p], zeros], axis=1),
                                   jnp.concatenate([zeros, ks[2 * p + 1]], axis=1)], axis=0)
            term = mask_ref[p] * _dot_nt(ql, kbd)
            a2 = term if a2 is None else a2 + term
        o_intra = _dot(a2.astype(BF16), jnp.concatenate([v, v], axis=0))
        st_add = _dot_tn(v, (k * e_k).astype(BF16))
        parts.append(((q * e_b).astype(BF16), e_b[c - 1:c, :], o_intra, st_add))

    st = st_scr[...]
    for rows, (q_b, decay, o_intra, st_add) in zip(chunk_rows, parts):
        o = _dot_nt(q_b, st.astype(BF16)) + o_intra
        st = st * decay + st_add
        ms = jnp.mean(o * o, axis=-1, keepdims=True)
        y = o * lax.rsqrt(ms + EPS) * norm_ref[...] * og_ref[0, rows, :]
        o_ref[0, rows, :] = y.astype(BF16)
    st_scr[...] = st

    @pl.when(ti == pl.num_programs(2) - 1)
    def _():
        s_ref[0, 0] = st.T


def _hgrn_prompt(hq, logf, kk, hv, og, norm_row, ct):
    b, t, _ = hq.shape
    c, levels = HG_CHUNK, HG_LEVELS
    w_np, m_np = _hgrn_tables(c, levels)
    w = jnp.asarray(w_np, BF16)
    masks = jnp.asarray(m_np, F32)
    tok = pl.BlockSpec((1, ct, LANES), lambda bi, h, ti: (bi, ti, h))
    return pl.pallas_call(
        functools.partial(_hgrn_kernel, c=c, levels=levels, n_chunks=ct // c),
        grid=(b, HG_HEADS, t // ct),
        in_specs=[tok, tok, tok, tok, tok,
                  pl.BlockSpec((1, LANES), lambda bi, h, ti: (0, 0)),
                  pl.BlockSpec(w.shape, lambda bi, h, ti: (0, 0)),
                  pl.BlockSpec(masks.shape, lambda bi, h, ti: (0, 0, 0))],
        out_specs=(tok, pl.BlockSpec((1, 1, HG_DK, HG_DV), lambda bi, h, ti: (bi, h, 0, 0))),
        out_shape=(jax.ShapeDtypeStruct((b, t, 1024), BF16),
                   jax.ShapeDtypeStruct((b, HG_HEADS, HG_DK, HG_DV), F32)),
        scratch_shapes=[pltpu.VMEM((HG_DV, HG_DK), F32)],
        compiler_params=_params("parallel", "parallel", "arbitrary"),
        name="hgrn_prompt",
    )(hq, logf, kk, hv, og, norm_row, w, masks)


def _dec_attn_kernel(pt_ref, wq_ref, kn_ref, vn_ref, lamp_ref, sub_ref, *refs, pages, lam_init):
    k_refs = refs[:pages]
    v_refs = refs[pages:2 * pages]
    o_ref = refs[2 * pages]
    m_scr, l_scr, acc_scr = refs[2 * pages + 1:]
    j = pl.program_id(1)
    page = k_refs[0].shape[2]

    @pl.when(j == 0)
    def _():
        m_scr[...] = jnp.full(m_scr.shape, -jnp.inf, F32)
        l_scr[...] = jnp.zeros(l_scr.shape, F32)
        acc_scr[...] = jnp.zeros(acc_scr.shape, F32)

    rown = lax.rem(lax.broadcasted_iota(jnp.int32, (16, LANES), 0), 8) // 2
    wq = wq_ref[0]
    s = jnp.concatenate([_dot(wq, k_refs[i][0].astype(BF16)) for i in range(pages)], axis=1)
    m_prev = m_scr[...]
    m_new = jnp.maximum(m_prev, jnp.max(s, axis=-1, keepdims=True))
    alpha = jnp.exp2(m_prev - m_new)
    pr = jnp.exp2(s - m_new)
    l_scr[...] = alpha * l_scr[...] + jnp.sum(pr, axis=-1, keepdims=True)
    prb = pr.astype(BF16)
    pv = jnp.zeros((16, LANES), F32)
    for n in range(ATT_KV_HEADS):
        v_n = jnp.concatenate([v_refs[i][0, pl.ds(n, page, stride=ATT_KV_HEADS), :] for i in range(pages)],
                              axis=0).astype(BF16)
        pv = pv + jnp.where(rown == n, _dot(prb, v_n), 0.0)
    acc_scr[...] = alpha * acc_scr[...] + pv
    m_scr[...] = m_new

    @pl.when(j == pl.num_programs(1) - 1)
    def _():
        s_cur = jnp.sum(wq.astype(F32) * kn_ref[0], axis=-1, keepdims=True)
        m_old = m_scr[...]
        m_fin = jnp.maximum(m_old, s_cur)
        al = jnp.exp2(m_old - m_fin)
        pc = jnp.exp2(s_cur - m_fin)
        l_fin = al * l_scr[...] + pc
        vn = vn_ref[0]
        v_cur = jnp.zeros((16, LANES), F32)
        for n in range(ATT_KV_HEADS):
            v_cur = v_cur + jnp.where(rown == n, vn[:, n * LANES:(n + 1) * LANES], 0.0)
        out = (al * acc_scr[...] + pc * v_cur) / l_fin
        lam = _diff_lambda(lamp_ref[...], lam_init)
        o = out[0:8] - lam * out[8:16]
        o_ref[0] = _sub_norm(o, sub_ref[...], lam_init)


def _attn_sample(wq, k_new, v_new, kt_pages, v_pages, page_table, lam_p, sub_row, lam_init, pages):
    db, n_pages = page_table.shape
    pt_flat = page_table.reshape(-1)
    blk = kt_pages.shape[1:]

    def page_spec(i):
        return pl.BlockSpec((1,) + blk, lambda bi, j, pt, i=i: (pt[bi * n_pages + j * pages + i], 0, 0))

    row3 = lambda w: pl.BlockSpec((1, 1, w), lambda bi, j, pt: (bi, 0, 0))
    grid_spec = pltpu.PrefetchScalarGridSpec(
        num_scalar_prefetch=1,
        grid=(db, n_pages // pages),
        in_specs=[pl.BlockSpec((1, 16, 512), lambda bi, j, pt: (bi, 0, 0)), row3(512), row3(512),
                  pl.BlockSpec((4, ATT_DHALF), lambda bi, j, pt: (0, 0)),
                  pl.BlockSpec((1, LANES), lambda bi, j, pt: (0, 0))]
                 + [page_spec(i) for i in range(pages)] + [page_spec(i) for i in range(pages)],
        out_specs=pl.BlockSpec((1, 8, LANES), lambda bi, j, pt: (bi, 0, 0)),
        scratch_shapes=[pltpu.VMEM((16, 1), F32), pltpu.VMEM((16, 1), F32), pltpu.VMEM((16, LANES), F32)],
    )
    return pl.pallas_call(
        functools.partial(_dec_attn_kernel, pages=pages, lam_init=lam_init),
        grid_spec=grid_spec,
        out_shape=jax.ShapeDtypeStruct((db, 8, LANES), F32),
        compiler_params=_params("parallel", "arbitrary"),
        name="attn_sample",
    )(pt_flat, wq, k_new, v_new, lam_p, sub_row, *([kt_pages] * pages), *([v_pages] * pages))


def _hgrn_step_kernel(s_ref, q_ref, g_ref, k_ref, v_ref, og_ref, norm_ref, o_ref, sn_ref):
    for h in range(HG_HEADS):
        s = s_ref[0, h]
        f = jnp.exp(g_ref[0, h])
        qc = q_ref[0, h]
        kc = k_ref[0, h]
        v = v_ref[0, h]
        o = jnp.sum(s * (qc * f), axis=0, keepdims=True) + jnp.sum(qc * kc, axis=0, keepdims=True) * v
        sn_ref[0, h] = s * f + kc * v
        ms = jnp.mean(o * o, axis=-1, keepdims=True)
        o_ref[0, h] = o * lax.rsqrt(ms + EPS) * norm_ref[...] * og_ref[0, h]


def _hgrn_sample(state, qcol, gcol, kcol, vrow, ogrow, norm_row):
    db = state.shape[0]
    col = pl.BlockSpec((1, HG_HEADS, HG_DK, 1), lambda bi: (bi, 0, 0, 0))
    row = pl.BlockSpec((1, HG_HEADS, 1, HG_DV), lambda bi: (bi, 0, 0, 0))
    st = pl.BlockSpec((1, HG_HEADS, HG_DK, HG_DV), lambda bi: (bi, 0, 0, 0))
    return pl.pallas_call(
        _hgrn_step_kernel,
        grid=(db,),
        in_specs=[st, col, col, col, row, row, pl.BlockSpec((1, LANES), lambda bi: (0, 0))],
        out_specs=(row, st),
        out_shape=(jax.ShapeDtypeStruct((db, HG_HEADS, 1, HG_DV), F32),
                   jax.ShapeDtypeStruct(state.shape, F32)),
        compiler_params=_params("parallel"),
        name="hgrn_sample",
    )(state, qcol, gcol, kcol, vrow, ogrow, norm_row)


def _lane_first(cond, lane):
    return jnp.min(jnp.where(cond, lane, LANES), axis=-1, keepdims=True)


def _route(logits):
    lane = lax.broadcasted_iota(jnp.int32, logits.shape, 1)
    gmask = (lane >= N_EXPERTS) & (lane < N_EXPERTS + N_GROUPS)
    lg = jnp.where(gmask, logits, -jnp.inf)
    mg = jnp.max(lg, axis=-1, keepdims=True)
    eg = jnp.exp(lg - mg)
    pg = eg / jnp.sum(eg, axis=-1, keepdims=True)
    pg_top = jnp.max(pg, axis=-1, keepdims=True)
    gsel = _lane_first(gmask & (pg == pg_top), lane) - N_EXPERTS
    emask = (lane >= gsel * EXP_PER_GROUP) & (lane < (gsel + 1) * EXP_PER_GROUP)
    le = jnp.where(emask, logits, -jnp.inf)
    me = jnp.max(le, axis=-1, keepdims=True)
    ee = jnp.exp(le - me)
    pe = ee / jnp.sum(ee, axis=-1, keepdims=True)
    pe = jnp.where(emask, pe, -1.0)
    p1 = jnp.max(pe, axis=-1, keepdims=True)
    e1 = _lane_first(pe == p1, lane)
    pe2 = jnp.where(lane == e1, -1.0, pe)
    p2 = jnp.max(pe2, axis=-1, keepdims=True)
    e2 = _lane_first(pe2 == p2, lane)
    den = p1 + p2
    w1 = p1 / den * pg_top
    w2 = p2 / den * pg_top
    comb = jnp.where(lane == e1, w1, 0.0) + jnp.where(lane == e2, w2, 0.0)
    picks = jnp.where(lane == 0, e1.astype(F32), jnp.where(lane == 1, e2.astype(F32),
                      jnp.where(lane == 2, w1, jnp.where(lane == 3, w2, 0.0))))
    return comb, picks


ROW_TILES = D_MODEL // LANES


def _merge_kernel(oa_ref, oh_ref, sga_ref, sgh_ref, x_ref, wpa_ref, wph_ref, wo_ref, gf_ref,
                  wr_ref, br_ref, x1_ref, xn_ref, route_ref, *, sparse):
    ya = _dot(oa_ref[...], wpa_ref[...])
    yh = _dot(oh_ref[...], wph_ref[...])
    mixed = (sga_ref[...] * ya + sgh_ref[...] * yh).astype(BF16)
    x1 = x_ref[...] + _dot(mixed, wo_ref[...])
    x1_ref[...] = x1
    ms = jnp.mean(x1 * x1, axis=-1, keepdims=True)
    xn_f = x1 * lax.rsqrt(ms + EPS) * gf_ref[...]
    xn = xn_f.astype(BF16)
    comb, picks = _route(_dot(xn, wr_ref[...]) + br_ref[...])
    if sparse:
        tm = x1.shape[0]
        for s in range(ROW_TILES):
            xn_ref[pl.ds(s, tm, stride=ROW_TILES), :] = xn_f[:, s * LANES:(s + 1) * LANES]
        route_ref[...] = picks
    else:
        xn_ref[...] = xn
        route_ref[...] = comb


def _merge(oa, oh, sga, sgh, x, wpa, wph, wo, gf, wr, br, tm, sparse):
    n = x.shape[0]
    tok = pl.BlockSpec((tm, 1024), lambda i: (i, 0))
    wsp = pl.BlockSpec((1024, 1024), lambda i: (0, 0))
    if sparse:
        xn_spec = pl.BlockSpec((tm * ROW_TILES, LANES), lambda i: (i, 0))
        xn_shape = jax.ShapeDtypeStruct((n * ROW_TILES, LANES), F32)
    else:
        xn_spec, xn_shape = tok, jax.ShapeDtypeStruct((n, 1024), BF16)
    return pl.pallas_call(
        functools.partial(_merge_kernel, sparse=sparse),
        grid=(n // tm,),
        in_specs=[tok, tok, tok, tok, tok, wsp, wsp, wsp,
                  pl.BlockSpec((1, 1024), lambda i: (0, 0)),
                  pl.BlockSpec((1024, LANES), lambda i: (0, 0)),
                  pl.BlockSpec((1, LANES), lambda i: (0, 0))],
        out_specs=(tok, xn_spec, pl.BlockSpec((tm, LANES), lambda i: (i, 0))),
        out_shape=(jax.ShapeDtypeStruct((n, 1024), F32), xn_shape,
                   jax.ShapeDtypeStruct((n, LANES), F32)),
        compiler_params=_params("parallel"),
        name="merge",
    )(oa, oh, sga, sgh, x, wpa, wph, wo, gf, wr, br)


def _moe_kernel(xn_ref, comb_ref, x1_ref, wg_ref, wu_ref, wd_ref, y_ref, acc_scr):
    e = pl.program_id(1)

    @pl.when(e == 0)
    def _():
        acc_scr[...] = jnp.zeros(acc_scr.shape, F32)

    comb = comb_ref[...]
    lane = lax.broadcasted_iota(jnp.int32, comb.shape, 1)
    c = jnp.sum(jnp.where(lane == e, comb, 0.0), axis=-1, keepdims=True)
    xn = xn_ref[...]
    hg = _dot(xn, wg_ref[0])
    hu = _dot(xn, wu_ref[0])
    hid = hg * jax.nn.sigmoid(hg) * hu
    acc_scr[...] += _dot((hid * c).astype(BF16), wd_ref[0])

    @pl.when(e == pl.num_programs(1) - 1)
    def _():
        y_ref[...] = x1_ref[...] + acc_scr[...]


def _moe(xn, comb, x1, wg, wu, wd, tm):
    n = xn.shape[0]
    return pl.pallas_call(
        _moe_kernel,
        grid=(n // tm, N_EXPERTS),
        in_specs=[pl.BlockSpec((tm, 1024), lambda i, e: (i, 0)),
                  pl.BlockSpec((tm, LANES), lambda i, e: (i, 0)),
                  pl.BlockSpec((tm, 1024), lambda i, e: (i, 0)),
                  pl.BlockSpec((1, 1024, D_FF), lambda i, e: (e, 0, 0)),
                  pl.BlockSpec((1, 1024, D_FF), lambda i, e: (e, 0, 0)),
                  pl.BlockSpec((1, D_FF, 1024), lambda i, e: (e, 0, 0))],
        out_specs=pl.BlockSpec((tm, 1024), lambda i, e: (i, 0)),
        out_shape=jax.ShapeDtypeStruct((n, 1024), F32),
        scratch_shapes=[pltpu.VMEM((tm, 1024), F32)],
        compiler_params=_params("parallel", "arbitrary"),
        name="moe",
    )(xn, comb, x1, wg, wu, wd)


def _row_copy(src_hbm, src_row, dst, dst_row, sem):
    return pltpu.make_async_copy(src_hbm.at[pl.ds(src_row * ROW_TILES, ROW_TILES)],
                                 dst.at[pl.ds(dst_row * ROW_TILES, ROW_TILES)], sem)


def _gather_rows(idx_ref, src_hbm, dst, sem, n_rows):
    def body(r, carry):
        _row_copy(src_hbm, idx_ref[r], dst, r, sem).start()
        return carry
    lax.fori_loop(0, n_rows, body, 0)


def _wait_rows(src_hbm, dst, sem):
    pltpu.make_async_copy(src_hbm.at[pl.ds(0, dst.shape[0])], dst, sem).wait()


def _rows_to_mat(buf, n_rows):
    return jnp.concatenate([buf[pl.ds(s, n_rows, stride=ROW_TILES), :] for s in range(ROW_TILES)], axis=1)


def _moe_scatter_kernel(p1_ref, p2_ref, xn_ref, xs_in, xs_hbm, sem):
    del xs_in
    i = pl.program_id(0)
    tm = p1_ref.shape[0]

    def wait_tile():
        for _ in range(2):
            pltpu.make_async_copy(xn_ref, xs_hbm.at[pl.ds(0, tm * ROW_TILES)], sem.at[0]).wait()

    def body(r, carry):
        src = xn_ref.at[pl.ds(r * ROW_TILES, ROW_TILES)]
        pltpu.make_async_copy(src, xs_hbm.at[pl.ds(p1_ref[r] * ROW_TILES, ROW_TILES)], sem.at[0]).start()
        pltpu.make_async_copy(src, xs_hbm.at[pl.ds(p2_ref[r] * ROW_TILES, ROW_TILES)], sem.at[0]).start()
        return carry

    lax.fori_loop(0, tm, body, 0)
    wait_tile()


def _moe_scatter(xg, pos1, pos2, rows, tm):
    n = pos1.shape[0]
    idx = pl.BlockSpec((tm,), lambda i: (i,), memory_space=pltpu.SMEM)
    xs0 = jnp.zeros((rows * ROW_TILES, LANES), F32)
    return pl.pallas_call(
        _moe_scatter_kernel,
        grid=(n // tm,),
        in_specs=[idx, idx, pl.BlockSpec((tm * ROW_TILES, LANES), lambda i: (i, 0)),
                  pl.BlockSpec(memory_space=pl.ANY)],
        out_specs=pl.BlockSpec(memory_space=pl.ANY),
        out_shape=jax.ShapeDtypeStruct(xs0.shape, F32),
        scratch_shapes=[pltpu.SemaphoreType.DMA((1,))],
        input_output_aliases={3: 0},
        compiler_params=_params("arbitrary"),
        name="moe_scatter",
    )(pos1, pos2, xg, xs0)


def _moe_ffn_kernel(te_ref, xs_ref, wg_ref, wu_ref, wd_ref, ys_ref, *, tp):
    del te_ref
    x = _rows_to_mat(xs_ref, tp).astype(BF16)
    hg = _dot(x, wg_ref[0])
    hu = _dot(x, wu_ref[0])
    hid = (hg * jax.nn.sigmoid(hg) * hu).astype(BF16)
    out = _dot(hid, wd_ref[0])
    for s in range(ROW_TILES):
        ys_ref[pl.ds(s, tp, stride=ROW_TILES), :] = out[:, s * LANES:(s + 1) * LANES]


def _moe_ffn(xs, tile_expert, wg, wu, wd, tp):
    n_tiles = tile_expert.shape[0]
    rows = pl.BlockSpec((tp * ROW_TILES, LANES), lambda i, te: (i, 0))
    wsp = lambda a, b: pl.BlockSpec((1, a, b), lambda i, te: (te[i], 0, 0))
    grid_spec = pltpu.PrefetchScalarGridSpec(
        num_scalar_prefetch=1,
        grid=(n_tiles,),
        in_specs=[rows, wsp(1024, D_FF), wsp(1024, D_FF), wsp(D_FF, 1024)],
        out_specs=rows,
    )
    return pl.pallas_call(
        functools.partial(_moe_ffn_kernel, tp=tp),
        grid_spec=grid_spec,
        out_shape=jax.ShapeDtypeStruct(xs.shape, F32),
        compiler_params=_params("parallel"),
        name="moe_ffn",
    )(tile_expert, xs, wg, wu, wd)


def _moe_combine_kernel(p1_ref, p1n_ref, p2_ref, p2n_ref, ys_hbm, x1_ref, w1_ref, w2_ref, y_ref, buf, sem, *, tm):
    i = pl.program_id(0)
    slot = lax.rem(i, 2)

    def gather(a_ref, b_ref, s):
        _gather_rows(a_ref, ys_hbm, buf.at[s, 0], sem.at[s, 0], tm)
        _gather_rows(b_ref, ys_hbm, buf.at[s, 1], sem.at[s, 1], tm)

    @pl.when(i == 0)
    def _():
        gather(p1_ref, p2_ref, 0)

    @pl.when(i + 1 < pl.num_programs(0))
    def _():
        gather(p1n_ref, p2n_ref, 1 - slot)

    _wait_rows(ys_hbm, buf.at[slot, 0], sem.at[slot, 0])
    _wait_rows(ys_hbm, buf.at[slot, 1], sem.at[slot, 1])
    y_ref[...] = x1_ref[...] + (w1_ref[...] * _rows_to_mat(buf.at[slot, 0], tm)
                                + w2_ref[...] * _rows_to_mat(buf.at[slot, 1], tm))


def _moe_combine(ys, pos1, pos2, w1, w2, x1, tm):
    n = x1.shape[0]
    n_tiles = n // tm
    idx = lambda off: pl.BlockSpec((tm,), lambda i: (jnp.minimum(i + off, n_tiles - 1),),
                                   memory_space=pltpu.SMEM)
    col = pl.BlockSpec((tm, 1), lambda i: (i, 0))
    return pl.pallas_call(
        functools.partial(_moe_combine_kernel, tm=tm),
        grid=(n_tiles,),
        in_specs=[idx(0), idx(1), idx(0), idx(1), pl.BlockSpec(memory_space=pl.ANY),
                  pl.BlockSpec((tm, 1024), lambda i: (i, 0)), col, col],
        out_specs=pl.BlockSpec((tm, 1024), lambda i: (i, 0)),
        out_shape=jax.ShapeDtypeStruct((n, 1024), F32),
        scratch_shapes=[pltpu.VMEM((2, 2, tm * ROW_TILES, LANES), F32), pltpu.SemaphoreType.DMA((2, 2))],
        compiler_params=_params("arbitrary"),
        name="moe_combine",
    )(pos1, pos1, pos2, pos2, ys, x1, w1, w2)


def _rank_kernel(picks_ref, tri_ref, r1_ref, r2_ref, cnt_ref, base_scr):
    @pl.when(pl.program_id(0) == 0)
    def _():
        base_scr[...] = jnp.zeros(base_scr.shape, F32)

    picks = picks_ref[...]
    lane = lax.broadcasted_iota(jnp.int32, picks.shape, 1)
    oh1 = lane == picks[:, 0:1].astype(jnp.int32)
    oh2 = lane == picks[:, 1:2].astype(jnp.int32)
    both = jnp.where(oh1 | oh2, 1.0, 0.0)
    before = _dot(tri_ref[...], both.astype(BF16)) + base_scr[...]
    r1_ref[...] = jnp.sum(jnp.where(oh1, before, 0.0), axis=-1, keepdims=True).astype(jnp.int32)
    r2_ref[...] = jnp.sum(jnp.where(oh2, before, 0.0), axis=-1, keepdims=True).astype(jnp.int32)
    base_scr[...] += jnp.sum(both, axis=0, keepdims=True)
    cnt_ref[...] = base_scr[...]


def _dispatch(picks, tp, tm):
    n = picks.shape[0]
    tri = jnp.asarray(np.tril(np.ones((tm, tm), np.float32), -1), BF16)
    col = pl.BlockSpec((tm, 1), lambda i: (i, 0))
    r1, r2, cnt = pl.pallas_call(
        _rank_kernel,
        grid=(n // tm,),
        in_specs=[pl.BlockSpec((tm, LANES), lambda i: (i, 0)), pl.BlockSpec((tm, tm), lambda i: (0, 0))],
        out_specs=(col, col, pl.BlockSpec((1, LANES), lambda i: (0, 0))),
        out_shape=(jax.ShapeDtypeStruct((n, 1), jnp.int32), jax.ShapeDtypeStruct((n, 1), jnp.int32),
                   jax.ShapeDtypeStruct((1, LANES), F32)),
        scratch_shapes=[pltpu.VMEM((1, LANES), F32)],
        compiler_params=_params("arbitrary"),
        name="moe_rank",
    )(picks, tri)
    counts = cnt[0, :N_EXPERTS].astype(jnp.int32)
    tiles = (counts + tp - 1) // tp
    tile_end = jnp.cumsum(tiles)
    row_start = (tile_end - tiles) * tp
    pos1 = row_start[picks[:, 0].astype(jnp.int32)] + r1[:, 0]
    pos2 = row_start[picks[:, 1].astype(jnp.int32)] + r2[:, 0]
    n_tiles = (2 * n) // tp + N_EXPERTS
    tile_expert = jnp.minimum(jnp.searchsorted(tile_end, jnp.arange(n_tiles, dtype=jnp.int32), side='right'),
                              N_EXPERTS - 1).astype(jnp.int32)
    return pos1, pos2, tile_expert, n_tiles * tp


def _pick(n, pref):
    t = min(n, pref)
    while n % t:
        t //= 2
    return t


def kernel(x_prompt, x_sample, cache_k, cache_v, state_hgrn, page_table, rms_in, w_in, att_q_norm, att_k_norm, att_lambda, att_sub_norm, hg_lower_bound, hg_out_norm, w_branch_att, w_branch_hg, w_out, rms_ffn, w_router_group, b_router_group, w_router_expert, b_router_expert, w_exp_gate, w_exp_up, w_exp_down):
    depth = rms_in.shape[0]
    assert depth == 1 and hg_lower_bound.shape[0] == 2
    b, t, _ = x_prompt.shape
    db, ds, _ = x_sample.shape
    assert ds == 1
    lam_init = 0.8 - 0.6 * math.exp(-0.3 * 0)

    w_in_bf = w_in[0].astype(BF16)
    qn_row = jnp.tile(att_q_norm[0], 2).reshape(1, LANES)
    kn_row = jnp.tile(att_k_norm[0], 2).reshape(1, LANES)
    seg_np = (np.arange(LANES)[:, None] // ATT_DHALF) == (np.arange(LANES)[None, :] // ATT_DHALF)
    seg = jnp.asarray(seg_np.astype(np.float32), BF16)
    sub_row = att_sub_norm[0].reshape(1, LANES)
    hgn_row = hg_out_norm[0].reshape(1, LANES)
    wpa = w_branch_att[0].astype(BF16)
    wph = w_branch_hg[0].astype(BF16)
    wo = w_out[0].astype(BF16)
    wr = jnp.zeros((D_MODEL, LANES), F32)
    wr = wr.at[:, :N_EXPERTS].set(w_router_expert[0]).at[:, N_EXPERTS:N_EXPERTS + N_GROUPS].set(w_router_group[0])
    wr = wr.astype(BF16)
    br = jnp.zeros((1, LANES), F32)
    br = br.at[0, :N_EXPERTS].set(b_router_expert[0]).at[0, N_EXPERTS:N_EXPERTS + N_GROUPS].set(b_router_group[0])
    weg = w_exp_gate[0].astype(BF16)
    weu = w_exp_up[0].astype(BF16)
    wed = w_exp_down[0].astype(BF16)
    gin = rms_in[0].reshape(1, D_MODEL)
    gffn = rms_ffn[0].reshape(1, D_MODEL)
    lam_p = att_lambda[0]

    def tail(x2, oa, oh, sga, sgh, tm_merge, tm_moe):
        sparse = x2.shape[0] >= 16 * MOE_TILE
        x1, xn, route = _merge(oa, oh, sga, sgh, x2, wpa, wph, wo, gffn, wr, br, tm_merge, sparse)
        if not sparse:
            return _moe(xn, route, x1, weg, weu, wed, tm_moe)
        pos1, pos2, tile_expert, rows = _dispatch(route, MOE_TILE, _pick(x2.shape[0], 512))
        xs = _moe_scatter(xn, pos1, pos2, rows, _pick(x2.shape[0], 512))
        ys = _moe_ffn(xs, tile_expert, weg, weu, wed, MOE_TILE)
        return _moe_combine(ys, pos1, pos2, route[:, 2:3], route[:, 3:4], x1, _pick(x2.shape[0], 256))

    n = b * t
    xp = x_prompt.reshape(n, D_MODEL)
    q, kt, v, hq, logf, kk, hv, og, sga, sgh = _inproj(
        xp, gin, w_in_bf, qn_row, kn_row, hg_lower_bound, seg, _pick(t, 1024), (b, t))
    oa = _attn_prompt(q.reshape(b, t, 1024), kt, v.reshape(b, t * ATT_KV_HEADS, LANES), lam_p, sub_row, lam_init,
                      _pick(t, 512))
    r3 = lambda a: a.reshape(b, t, 1024)
    oh, state_p = _hgrn_prompt(r3(hq), r3(logf), r3(kk), r3(hv), r3(og), hgn_row, _pick(t, 1024))
    yp = tail(xp, oa.reshape(n, 1024), oh.reshape(n, 1024), sga, sgh, _pick(n, 512), _pick(n, 1024))
    k_prompt = jnp.transpose(kt.reshape(1, b, ATT_KV_HEADS, 2, ATT_DHALF, t), (0, 1, 5, 2, 3, 4))

    xs = x_sample.reshape(db, D_MODEL)
    q, ks, vs, hq, logf, kk, hv, og, sga, sgh = _inproj(
        xs, gin, w_in_bf, qn_row, kn_row, hg_lower_bound, seg, _pick(db, 1024), None)
    q5 = q.reshape(db, ATT_KV_HEADS, 2, 2, ATT_DHALF)
    eye_n = jnp.eye(ATT_KV_HEADS, dtype=BF16)
    eye_c = jnp.eye(2, dtype=BF16)
    wq = jnp.einsum('bngcd,nm,ce->bcngmed', q5, eye_n, eye_c).reshape(db, 16, 512)
    n_pool = cache_k.shape[1]
    page = cache_k.shape[2]
    kt_pages = jnp.transpose(cache_k[0], (0, 2, 3, 4, 1)).reshape(n_pool, 512, page)
    v_pages = cache_v[0].reshape(n_pool, page * ATT_KV_HEADS, ATT_DV)
    oa_s = _attn_sample(wq, ks.reshape(db, 1, 512), vs.reshape(db, 1, 512), kt_pages, v_pages,
                        page_table, lam_p, sub_row, lam_init, _pick(page_table.shape[1], 16))
    col = lambda a: a.astype(F32).reshape(db, HG_HEADS, HG_DK, 1)
    row = lambda a: a.astype(F32).reshape(db, HG_HEADS, 1, HG_DV)
    oh_s, state_s = _hgrn_sample(state_hgrn[0], col(hq), col(logf), col(kk), row(hv), row(og), hgn_row)
    ys = tail(xs, oa_s.reshape(db, 1024).astype(BF16), oh_s.reshape(db, 1024).astype(BF16), sga, sgh,
              _pick(db, 512), _pick(db, 1024))

    return (yp.reshape(b, t, D_MODEL), ys.reshape(db, 1, D_MODEL),
            k_prompt, v.reshape(1, b, t, ATT_KV_HEADS, ATT_DV),
            state_p.reshape(1, b, HG_HEADS, HG_DK, HG_DV),
            ks.reshape(1, db, 1, ATT_KV_HEADS, 2, ATT_DHALF), vs.reshape(1, db, 1, ATT_KV_HEADS, ATT_DV),
            state_s.reshape(1, db, HG_HEADS, HG_DK, HG_DV))
```

```python
import functools
import math

import numpy as np
import jax
import jax.numpy as jnp
from jax import lax
from jax.experimental import pallas as pl
from jax.experimental.pallas import tpu as pltpu

F32 = jnp.float32
BF16 = jnp.bfloat16

D_MODEL = 1024
ATT_HEADS = 8
ATT_KV_HEADS = 4
ATT_DHALF = 64
ATT_DV = 128
HG_HEADS = 8
HG_DK = 128
HG_DV = 128
N_GROUPS = 4
EXP_PER_GROUP = 8
N_EXPERTS = 32
D_FF = 512
EPS = 1e-6
LANES = 128
VMEM_LIMIT = 56 * 1024 * 1024

LOG2E = 1.4426950408889634
Q_SCALE = (ATT_DHALF ** -0.5) * LOG2E

MOE_TILE = 512
HG_CHUNK = 64
HG_LEVELS = 6


def _dot(a, b):
    return jnp.dot(a, b, preferred_element_type=F32)


def _dot_nt(a, b):
    return lax.dot_general(a, b, (((1,), (1,)), ((), ())), preferred_element_type=F32)


def _dot_tn(a, b):
    return lax.dot_general(a, b, (((0,), (0,)), ((), ())), preferred_element_type=F32)


def _split3(x):
    hi = x.astype(BF16)
    r1 = x - hi.astype(F32)
    mid = r1.astype(BF16)
    lo = (r1 - mid.astype(F32)).astype(BF16)
    return hi, mid, lo


def _seg_rms(z, seg, gain_row, scale):
    outs = []
    for i in range(z.shape[1] // LANES):
        zi = z[:, i * LANES:(i + 1) * LANES]
        z2 = zi * zi
        hi = z2.astype(BF16)
        lo = (z2 - hi.astype(F32)).astype(BF16)
        ss = _dot(hi, seg) + _dot(lo, seg)
        y = zi * lax.rsqrt(ss * (1.0 / ATT_DHALF) + EPS) * gain_row
        if scale != 1.0:
            y = y * scale
        outs.append(y)
    return jnp.concatenate(outs, axis=1)


def _params(*sem):
    return pltpu.CompilerParams(dimension_semantics=sem, vmem_limit_bytes=VMEM_LIMIT)


def _rms_kernel(x_ref, g_ref, h_ref):
    x = x_ref[...]
    ms = jnp.mean(x * x, axis=-1, keepdims=True)
    h_ref[...] = (x * lax.rsqrt(ms + EPS) * g_ref[...]).astype(BF16)


def _rms_cast(x, g, tm):
    n = x.shape[0]
    return pl.pallas_call(
        _rms_kernel,
        grid=(n // tm,),
        in_specs=[pl.BlockSpec((tm, D_MODEL), lambda i: (i, 0)), pl.BlockSpec((1, D_MODEL), lambda i: (0, 0))],
        out_specs=pl.BlockSpec((tm, D_MODEL), lambda i: (i, 0)),
        out_shape=jax.ShapeDtypeStruct((n, D_MODEL), BF16),
        compiler_params=_params("parallel"),
        name="rms_in",
    )(x, g)


def _proj_kernel(h_ref, w_ref, *refs, mode, n_aux):
    aux, outs = refs[:n_aux], refs[n_aux:]
    z = _dot(h_ref[...], w_ref[...])
    if mode == "q":
        outs[0][...] = _seg_rms(z, aux[0][...], aux[1][...], Q_SCALE).astype(BF16)
    elif mode in ("kv", "kv_t"):
        k = _seg_rms(z[:, :512], aux[0][...], aux[1][...], 1.0)
        if mode == "kv_t":
            outs[0][0] = k.T
            tm = z.shape[0]
            for n in range(ATT_KV_HEADS):
                outs[1][pl.ds(n, tm, stride=ATT_KV_HEADS), :] = z[:, 512 + n * LANES:512 + (n + 1) * LANES]
        else:
            outs[0][...] = k
            outs[1][...] = z[:, 512:]
    elif mode == "hq":
        outs[0][...] = (z * (HG_DK ** -0.5)).astype(BF16)
    elif mode == "hf":
        lbp = aux[0][...]
        m = jnp.max(lbp, axis=0, keepdims=True)
        e = jnp.exp(lbp - m)
        lb = e[0:1] / jnp.sum(e, axis=0, keepdims=True)
        f = lb + (1.0 - lb) * jax.nn.sigmoid(z)
        outs[0][...] = jnp.log(f)
        outs[1][...] = (1.0 - lb) * jax.nn.sigmoid(-z)
    elif mode == "hv":
        outs[0][...] = z.astype(BF16)
    elif mode == "silu":
        outs[0][...] = z * jax.nn.sigmoid(z)
    elif mode == "sigmoid":
        outs[0][...] = jax.nn.sigmoid(z)


def _proj(h, w, col, mode, aux, outs, tm, name):
    n = h.shape[0]
    aux_specs = [pl.BlockSpec(a.shape, lambda i: (0, 0)) for a in aux]
    res = pl.pallas_call(
        functools.partial(_proj_kernel, mode=mode, n_aux=len(aux)),
        grid=(n // tm,),
        in_specs=[pl.BlockSpec((tm, D_MODEL), lambda i: (i, 0)),
                  pl.BlockSpec((D_MODEL, 1024), lambda i, col=col: (0, col))] + aux_specs,
        out_specs=tuple(pl.BlockSpec(o[2], o[3]) for o in outs),
        out_shape=tuple(jax.ShapeDtypeStruct(o[0], o[1]) for o in outs),
        compiler_params=_params("parallel"),
        name=name,
    )(h, w, *aux)
    return res


def _inproj(x, gin, w, qn_row, kn_row, lbp, seg, tm, seq):
    n = x.shape[0]
    h = _rms_cast(x, gin, tm)
    tok = lambda width, dt: ((n, width), dt, (tm, width), lambda i: (i, 0))
    q, = _proj(h, w, 0, "q", [seg, qn_row], [tok(1024, BF16)], tm, "proj_q")
    if seq is None:
        k, v = _proj(h, w, 1, "kv", [seg, kn_row], [tok(512, F32), tok(512, F32)], tm, "proj_kv")
    else:
        b, t = seq
        tpb = t // tm
        k_out = ((b, 512, t), F32, (1, 512, tm), lambda i: (i // tpb, 0, i % tpb))
        v_out = ((n * ATT_KV_HEADS, LANES), F32, (tm * ATT_KV_HEADS, LANES), lambda i: (i, 0))
        k, v = _proj(h, w, 1, "kv_t", [seg, kn_row], [k_out, v_out], tm, "proj_kv")
    hq, = _proj(h, w, 2, "hq", [], [tok(1024, BF16)], tm, "proj_hq")
    logf, kk = _proj(h, w, 3, "hf", [lbp], [tok(1024, F32), tok(1024, F32)], tm, "proj_hf")
    hv, = _proj(h, w, 4, "hv", [], [tok(1024, BF16)], tm, "proj_hv")
    og, = _proj(h, w, 5, "silu", [], [tok(1024, F32)], tm, "proj_og")
    sga, = _proj(h, w, 6, "sigmoid", [], [tok(1024, F32)], tm, "proj_ga")
    sgh, = _proj(h, w, 7, "sigmoid", [], [tok(1024, F32)], tm, "proj_gh")
    return q, k, v, hq, logf, kk, hv, og, sga, sgh


def _diff_lambda(lp, lam_init):
    a = jnp.sum(lp[0:1] * lp[1:2], axis=-1, keepdims=True)
    b = jnp.sum(lp[2:3] * lp[3:4], axis=-1, keepdims=True)
    return jnp.exp(a) - jnp.exp(b) + lam_init


def _sub_norm(o, sub_row, lam_init):
    ms = jnp.mean(o * o, axis=-1, keepdims=True)
    return o * lax.rsqrt(ms + EPS) * sub_row * (1.0 - lam_init)


def _attn_kernel(q_ref, kt_ref, v_ref, lamp_ref, sub_ref, o_ref,
                 kb_scr, vb_scr, qs_scr, m_scr, acc_scr, *, tq, lam_init):
    qi = pl.program_id(2)

    @pl.when(qi == 0)
    def _():
        kb_scr[...] = kt_ref[0].astype(BF16)
        t_len = vb_scr.shape[0]
        vb_scr[:, 0:LANES] = v_ref[0, pl.ds(pl.program_id(1), t_len, stride=ATT_KV_HEADS), :].astype(BF16)
        vb_scr[:, LANES:2 * LANES] = jnp.ones((vb_scr.shape[0], LANES), BF16)

    q = q_ref[0].astype(F32)
    lane = lax.broadcasted_iota(jnp.int32, (tq, LANES), 1)
    for g in range(2):
        qg = q[:, g * LANES:(g + 1) * LANES]
        for c in range(2):
            keep = (lane < ATT_DHALF) if c == 0 else (lane >= ATT_DHALF)
            r = g * 2 + c
            qs_scr[r * tq:(r + 1) * tq, :] = jnp.where(keep, qg, 0.0).astype(BF16)
    m_scr[...] = jnp.full(m_scr.shape, -jnp.inf, F32)
    acc_scr[...] = jnp.zeros(acc_scr.shape, F32)

    def chunk(j, masked):
        off = pl.multiple_of(j * tq, tq)
        s = _dot(qs_scr[...], kb_scr[:, pl.ds(off, tq)])
        if masked:
            row = lax.rem(lax.broadcasted_iota(jnp.int32, s.shape, 0), tq)
            s = jnp.where(lax.broadcasted_iota(jnp.int32, s.shape, 1) <= row, s, -jnp.inf)
        m_prev = m_scr[...]
        m_new = jnp.maximum(m_prev, jnp.max(s, axis=-1, keepdims=True))
        alpha = jnp.exp2(m_prev - m_new)
        pr = jnp.exp2(s - jnp.concatenate([m_new] * (tq // LANES), axis=1))
        pv = _dot(pr.astype(BF16), vb_scr[pl.ds(off, tq), :])
        acc_scr[...] = jnp.concatenate([alpha, alpha], axis=1) * acc_scr[...] + pv
        m_scr[...] = m_new

    def body(j, carry):
        chunk(j, False)
        return carry

    lax.fori_loop(0, qi, body, 0)
    chunk(qi, True)

    lam = _diff_lambda(lamp_ref[...], lam_init)
    acc = acc_scr[...]
    out = acc[:, 0:LANES] / acc[:, LANES:2 * LANES]
    for g in range(2):
        o0 = out[(2 * g) * tq:(2 * g + 1) * tq]
        o1 = out[(2 * g + 1) * tq:(2 * g + 2) * tq]
        o_ref[0, :, g * LANES:(g + 1) * LANES] = _sub_norm(o0 - lam * o1, sub_ref[...], lam_init).astype(BF16)


def _attn_prompt(q, kt, v, lam_p, sub_row, lam_init, tq):
    b, t, _ = q.shape
    return pl.pallas_call(
        functools.partial(_attn_kernel, tq=tq, lam_init=lam_init),
        grid=(b, ATT_KV_HEADS, t // tq),
        in_specs=[
            pl.BlockSpec((1, tq, 256), lambda bi, n, qi: (bi, qi, n)),
            pl.BlockSpec((1, LANES, t), lambda bi, n, qi: (bi, n, 0)),
            pl.BlockSpec((1, t * ATT_KV_HEADS, LANES), lambda bi, n, qi: (bi, 0, 0)),
            pl.BlockSpec((4, ATT_DHALF), lambda bi, n, qi: (0, 0)),
            pl.BlockSpec((1, LANES), lambda bi, n, qi: (0, 0)),
        ],
        out_specs=pl.BlockSpec((1, tq, 256), lambda bi, n, qi: (bi, qi, n)),
        out_shape=jax.ShapeDtypeStruct((b, t, 1024), BF16),
        scratch_shapes=[pltpu.VMEM((LANES, t), BF16), pltpu.VMEM((t, 2 * LANES), BF16),
                        pltpu.VMEM((4 * tq, LANES), BF16), pltpu.VMEM((4 * tq, LANES), F32),
                        pltpu.VMEM((4 * tq, 2 * LANES), F32)],
        compiler_params=_params("parallel", "parallel", "arbitrary"),
        name="attn_prompt",
    )(q, kt, v, lam_p, sub_row)


def _hgrn_tables(c, levels):
    t = np.arange(c)[:, None]
    u = np.arange(c)[None, :]
    w = [(u <= t), (u > t)]
    masks = [(t == u)]
    for l in range(1, levels + 1):
        n = 1 << l
        half = n // 2
        mid = (t // n) * n + half
        hi = t >= mid
        w.append(np.where(hi, (u > mid) & (u <= t), (u > t) & (u <= mid)))
        masks.append((t // n == u // n) & (t % n >= half) & (u % n < half))
    w = np.concatenate(w, axis=0).astype(np.float32)
    w3 = np.concatenate([w, w, w], axis=1)
    if len(masks) % 2:
        masks.append(np.zeros_like(masks[0]))
    pairs = [np.concatenate([masks[i], masks[i + 1]], axis=1) for i in range(0, len(masks), 2)]
    return w3, np.stack(pairs).astype(np.float32)


def _hgrn_kernel(q_ref, g_ref, k_ref, v_ref, og_ref, norm_ref, w_ref, mask_ref, o_ref, s_ref,
                 st_scr, *, c, levels, n_chunks):
    ti = pl.program_id(2)

    @pl.when(ti == 0)
    def _():
        st_scr[...] = jnp.zeros(st_scr.shape, F32)

    w = w_ref[...]
    zeros = jnp.zeros((c, LANES), BF16)
    chunk_rows = [slice(ci * c, (ci + 1) * c) for ci in range(n_chunks)]
    g3 = jnp.concatenate([jnp.concatenate(_split3(g_ref[0, rows, :]), axis=0) for rows in chunk_rows], axis=1)
    e_all = jnp.exp(_dot(w, g3))

    parts = []
    for ci, rows in enumerate(chunk_rows):
        q = q_ref[0, rows, :].astype(F32)
        k = k_ref[0, rows, :]
        v = v_ref[0, rows, :]
        e = e_all[:, ci * LANES:(ci + 1) * LANES]
        e_b = e[0:c]
        e_k = e[c:2 * c]
        qs = [q.astype(BF16)]
        ks = [k.astype(BF16)]
        for l in range(1, levels + 1):
            e_l = e[(l + 1) * c:(l + 2) * c]
            qs.append((q * e_l).astype(BF16))
            ks.append((k * e_l).astype(BF16))
        if len(qs) % 2:
            qs.append(zeros)
            ks.append(zeros)
        a2 = None
        for p in range(len(qs) // 2):
            ql = jnp.concatenate([qs[2 * p], qs[2 * p + 1]], axis=1)
            kbd = jnp.concatenate([jnp.concatenate([ks[2 * p], zeros], axis=1),
                                   jnp.concatenate([zeros, ks[2 * p + 1]], axis=1)], axis=0)
            term = mask_ref[p] * _dot_nt(ql, kbd)
            a2 = term if a2 is None else a2 + term
        o_intra = _dot(a2.astype(BF16), jnp.concatenate([v, v], axis=0))
        st_add = _dot_tn(v, (k * e_k).astype(BF16))
        parts.append(((q * e_b).astype(BF16), e_b[c - 1:c, :], o_intra, st_add))

    st = st_scr[...]
    for rows, (q_b, decay, o_intra, st_add) in zip(chunk_rows, parts):
        o = _dot_nt(q_b, st.astype(BF16)) + o_intra
        st = st * decay + st_add
        ms = jnp.mean(o * o, axis=-1, keepdims=True)
        y = o * lax.rsqrt(ms + EPS) * norm_ref[...] * og_ref[0, rows, :]
        o_ref[0, rows, :] = y.astype(BF16)
    st_scr[...] = st

    @pl.when(ti == pl.num_programs(2) - 1)
    def _():
        s_ref[0, 0] = st.T


def _hgrn_prompt(hq, logf, kk, hv, og, norm_row, ct):
    b, t, _ = hq.shape
    c, levels = HG_CHUNK, HG_LEVELS
    w_np, m_np = _hgrn_tables(c, levels)
    w = jnp.asarray(w_np, BF16)
    masks = jnp.asarray(m_np, F32)
    tok = pl.BlockSpec((1, ct, LANES), lambda bi, h, ti: (bi, ti, h))
    return pl.pallas_call(
        functools.partial(_hgrn_kernel, c=c, levels=levels, n_chunks=ct // c),
        grid=(b, HG_HEADS, t // ct),
        in_specs=[tok, tok, tok, tok, tok,
                  pl.BlockSpec((1, LANES), lambda bi, h, ti: (0, 0)),
                  pl.BlockSpec(w.shape, lambda bi, h, ti: (0, 0)),
                  pl.BlockSpec(masks.shape, lambda bi, h, ti: (0, 0, 0))],
        out_specs=(tok, pl.BlockSpec((1, 1, HG_DK, HG_DV), lambda bi, h, ti: (bi, h, 0, 0))),
        out_shape=(jax.ShapeDtypeStruct((b, t, 1024), BF16),
                   jax.ShapeDtypeStruct((b, HG_HEADS, HG_DK, HG_DV), F32)),
        scratch_shapes=[pltpu.VMEM((HG_DV, HG_DK), F32)],
        compiler_params=_params("parallel", "parallel", "arbitrary"),
        name="hgrn_prompt",
    )(hq, logf, kk, hv, og, norm_row, w, masks)


def _dec_attn_kernel(pt_ref, wq_ref, kn_ref, vn_ref, lamp_ref, sub_ref, *refs, pages, lam_init):
    k_refs = refs[:pages]
    v_refs = refs[pages:2 * pages]
    o_ref = refs[2 * pages]
    m_scr, l_scr, acc_scr = refs[2 * pages + 1:]
    j = pl.program_id(1)
    page = k_refs[0].shape[2]

    @pl.when(j == 0)
    def _():
        m_scr[...] = jnp.full(m_scr.shape, -jnp.inf, F32)
        l_scr[...] = jnp.zeros(l_scr.shape, F32)
        acc_scr[...] = jnp.zeros(acc_scr.shape, F32)

    rown = lax.rem(lax.broadcasted_iota(jnp.int32, (16, LANES), 0), 8) // 2
    wq = wq_ref[0]
    s = jnp.concatenate([_dot(wq, k_refs[i][0].astype(BF16)) for i in range(pages)], axis=1)
    m_prev = m_scr[...]
    m_new = jnp.maximum(m_prev, jnp.max(s, axis=-1, keepdims=True))
    alpha = jnp.exp2(m_prev - m_new)
    pr = jnp.exp2(s - m_new)
    l_scr[...] = alpha * l_scr[...] + jnp.sum(pr, axis=-1, keepdims=True)
    prb = pr.astype(BF16)
    pv = jnp.zeros((16, LANES), F32)
    for n in range(ATT_KV_HEADS):
        v_n = jnp.concatenate([v_refs[i][0, pl.ds(n, page, stride=ATT_KV_HEADS), :] for i in range(pages)],
                              axis=0).astype(BF16)
        pv = pv + jnp.where(rown == n, _dot(prb, v_n), 0.0)
    acc_scr[...] = alpha * acc_scr[...] + pv
    m_scr[...] = m_new

    @pl.when(j == pl.num_programs(1) - 1)
    def _():
        s_cur = jnp.sum(wq.astype(F32) * kn_ref[0], axis=-1, keepdims=True)
        m_old = m_scr[...]
        m_fin = jnp.maximum(m_old, s_cur)
        al = jnp.exp2(m_old - m_fin)
        pc = jnp.exp2(s_cur - m_fin)
        l_fin = al * l_scr[...] + pc
        vn = vn_ref[0]
        v_cur = jnp.zeros((16, LANES), F32)
        for n in range(ATT_KV_HEADS):
            v_cur = v_cur + jnp.where(rown == n, vn[:, n * LANES:(n + 1) * LANES], 0.0)
        out = (al * acc_scr[...] + pc * v_cur) / l_fin
        lam = _diff_lambda(lamp_ref[...], lam_init)
        o = out[0:8] - lam * out[8:16]
        o_ref[0] = _sub_norm(o, sub_ref[...], lam_init)


def _attn_sample(wq, k_new, v_new, kt_pages, v_pages, page_table, lam_p, sub_row, lam_init, pages):
    db, n_pages = page_table.shape
    pt_flat = page_table.reshape(-1)
    blk = kt_pages.shape[1:]

    def page_spec(i):
        return pl.BlockSpec((1,) + blk, lambda bi, j, pt, i=i: (pt[bi * n_pages + j * pages + i], 0, 0))

    row3 = lambda w: pl.BlockSpec((1, 1, w), lambda bi, j, pt: (bi, 0, 0))
    grid_spec = pltpu.PrefetchScalarGridSpec(
        num_scalar_prefetch=1,
        grid=(db, n_pages // pages),
        in_specs=[pl.BlockSpec((1, 16, 512), lambda bi, j, pt: (bi, 0, 0)), row3(512), row3(512),
                  pl.BlockSpec((4, ATT_DHALF), lambda bi, j, pt: (0, 0)),
                  pl.BlockSpec((1, LANES), lambda bi, j, pt: (0, 0))]
                 + [page_spec(i) for i in range(pages)] + [page_spec(i) for i in range(pages)],
        out_specs=pl.BlockSpec((1, 8, LANES), lambda bi, j, pt: (bi, 0, 0)),
        scratch_shapes=[pltpu.VMEM((16, 1), F32), pltpu.VMEM((16, 1), F32), pltpu.VMEM((16, LANES), F32)],
    )
    return pl.pallas_call(
        functools.partial(_dec_attn_kernel, pages=pages, lam_init=lam_init),
        grid_spec=grid_spec,
        out_shape=jax.ShapeDtypeStruct((db, 8, LANES), F32),
        compiler_params=_params("parallel", "arbitrary"),
        name="attn_sample",
    )(pt_flat, wq, k_new, v_new, lam_p, sub_row, *([kt_pages] * pages), *([v_pages] * pages))


def _hgrn_step_kernel(s_ref, q_ref, g_ref, k_ref, v_ref, og_ref, norm_ref, o_ref, sn_ref):
    pad = jnp.zeros((LANES - 3 * HG_HEADS, LANES), F32)
    cols = jnp.concatenate([q_ref[0], g_ref[0], k_ref[0], pad], axis=0).T
    v_all = v_ref[0]
    og_all = og_ref[0]
    outs = []
    for h in range(HG_HEADS):
        s = s_ref[0, h]
        qc = cols[:, h:h + 1]
        f = jnp.exp(cols[:, HG_HEADS + h:HG_HEADS + h + 1])
        kc = cols[:, 2 * HG_HEADS + h:2 * HG_HEADS + h + 1]
        v = v_all[h:h + 1]
        o = jnp.sum(s * (qc * f), axis=0, keepdims=True) + jnp.sum(qc * kc, axis=0, keepdims=True) * v
        sn_ref[0, h] = s * f + kc * v
        ms = jnp.mean(o * o, axis=-1, keepdims=True)
        outs.append(o * lax.rsqrt(ms + EPS) * norm_ref[...] * og_all[h:h + 1])
    o_ref[0] = jnp.concatenate(outs, axis=0)


def _hgrn_sample(state, q, g, k, v, og, norm_row):
    db = state.shape[0]
    row = pl.BlockSpec((1, HG_HEADS, HG_DK), lambda bi: (bi, 0, 0))
    st = pl.BlockSpec((1, HG_HEADS, HG_DK, HG_DV), lambda bi: (bi, 0, 0, 0))
    return pl.pallas_call(
        _hgrn_step_kernel,
        grid=(db,),
        in_specs=[st, row, row, row, row, row, pl.BlockSpec((1, LANES), lambda bi: (0, 0))],
        out_specs=(row, st),
        out_shape=(jax.ShapeDtypeStruct((db, HG_HEADS, HG_DV), F32),
                   jax.ShapeDtypeStruct(state.shape, F32)),
        compiler_params=_params("parallel"),
        name="hgrn_sample",
    )(state, q, g, k, v, og, norm_row)


def _lane_first(cond, lane):
    return jnp.min(jnp.where(cond, lane, LANES), axis=-1, keepdims=True)


def _route(logits):
    lane = lax.broadcasted_iota(jnp.int32, logits.shape, 1)
    gmask = (lane >= N_EXPERTS) & (lane < N_EXPERTS + N_GROUPS)
    lg = jnp.where(gmask, logits, -jnp.inf)
    mg = jnp.max(lg, axis=-1, keepdims=True)
    eg = jnp.exp(lg - mg)
    pg = eg / jnp.sum(eg, axis=-1, keepdims=True)
    pg_top = jnp.max(pg, axis=-1, keepdims=True)
    gsel = _lane_first(gmask & (pg == pg_top), lane) - N_EXPERTS
    emask = (lane >= gsel * EXP_PER_GROUP) & (lane < (gsel + 1) * EXP_PER_GROUP)
    le = jnp.where(emask, logits, -jnp.inf)
    me = jnp.max(le, axis=-1, keepdims=True)
    ee = jnp.exp(le - me)
    pe = ee / jnp.sum(ee, axis=-1, keepdims=True)
    pe = jnp.where(emask, pe, -1.0)
    p1 = jnp.max(pe, axis=-1, keepdims=True)
    e1 = _lane_first(pe == p1, lane)
    pe2 = jnp.where(lane == e1, -1.0, pe)
    p2 = jnp.max(pe2, axis=-1, keepdims=True)
    e2 = _lane_first(pe2 == p2, lane)
    den = p1 + p2
    w1 = p1 / den * pg_top
    w2 = p2 / den * pg_top
    comb = jnp.where(lane == e1, w1, 0.0) + jnp.where(lane == e2, w2, 0.0)
    picks = jnp.where(lane == 0, e1.astype(F32), jnp.where(lane == 1, e2.astype(F32),
                      jnp.where(lane == 2, w1, jnp.where(lane == 3, w2, 0.0))))
    return comb, picks


ROW_TILES = D_MODEL // LANES


def _merge_kernel(oa_ref, oh_ref, sga_ref, sgh_ref, x_ref, wpa_ref, wph_ref, wo_ref, gf_ref,
                  wr_ref, br_ref, x1_ref, xn_ref, route_ref, *, sparse):
    ya = _dot(oa_ref[...], wpa_ref[...])
    yh = _dot(oh_ref[...], wph_ref[...])
    mixed = (sga_ref[...] * ya + sgh_ref[...] * yh).astype(BF16)
    x1 = x_ref[...] + _dot(mixed, wo_ref[...])
    x1_ref[...] = x1
    ms = jnp.mean(x1 * x1, axis=-1, keepdims=True)
    xn_f = x1 * lax.rsqrt(ms + EPS) * gf_ref[...]
    xn = xn_f.astype(BF16)
    comb, picks = _route(_dot(xn, wr_ref[...]) + br_ref[...])
    if sparse:
        tm = x1.shape[0]
        for s in range(ROW_TILES):
            xn_ref[pl.ds(s, tm, stride=ROW_TILES), :] = xn_f[:, s * LANES:(s + 1) * LANES]
        route_ref[...] = picks
    else:
        xn_ref[...] = xn
        route_ref[...] = comb


def _merge(oa, oh, sga, sgh, x, wpa, wph, wo, gf, wr, br, tm, sparse):
    n = x.shape[0]
    tok = pl.BlockSpec((tm, 1024), lambda i: (i, 0))
    wsp = pl.BlockSpec((1024, 1024), lambda i: (0, 0))
    if sparse:
        xn_spec = pl.BlockSpec((tm * ROW_TILES, LANES), lambda i: (i, 0))
        xn_shape = jax.ShapeDtypeStruct((n * ROW_TILES, LANES), F32)
    else:
        xn_spec, xn_shape = tok, jax.ShapeDtypeStruct((n, 1024), BF16)
    return pl.pallas_call(
        functools.partial(_merge_kernel, sparse=sparse),
        grid=(n // tm,),
        in_specs=[tok, tok, tok, tok, tok, wsp, wsp, wsp,
                  pl.BlockSpec((1, 1024), lambda i: (0, 0)),
                  pl.BlockSpec((1024, LANES), lambda i: (0, 0)),
                  pl.BlockSpec((1, LANES), lambda i: (0, 0))],
        out_specs=(tok, xn_spec, pl.BlockSpec((tm, LANES), lambda i: (i, 0))),
        out_shape=(jax.ShapeDtypeStruct((n, 1024), F32), xn_shape,
                   jax.ShapeDtypeStruct((n, LANES), F32)),
        compiler_params=_params("parallel"),
        name="merge",
    )(oa, oh, sga, sgh, x, wpa, wph, wo, gf, wr, br)


def _moe_kernel(xn_ref, comb_ref, x1_ref, wg_ref, wu_ref, wd_ref, y_ref, acc_scr):
    e = pl.program_id(1)

    @pl.when(e == 0)
    def _():
        acc_scr[...] = jnp.zeros(acc_scr.shape, F32)

    comb = comb_ref[...]
    lane = lax.broadcasted_iota(jnp.int32, comb.shape, 1)
    c = jnp.sum(jnp.where(lane == e, comb, 0.0), axis=-1, keepdims=True)
    xn = xn_ref[...]
    hg = _dot(xn, wg_ref[0])
    hu = _dot(xn, wu_ref[0])
    hid = hg * jax.nn.sigmoid(hg) * hu
    acc_scr[...] += _dot((hid * c).astype(BF16), wd_ref[0])

    @pl.when(e == pl.num_programs(1) - 1)
    def _():
        y_ref[...] = x1_ref[...] + acc_scr[...]


def _moe(xn, comb, x1, wg, wu, wd, tm):
    n = xn.shape[0]
    return pl.pallas_call(
        _moe_kernel,
        grid=(n // tm, N_EXPERTS),
        in_specs=[pl.BlockSpec((tm, 1024), lambda i, e: (i, 0)),
                  pl.BlockSpec((tm, LANES), lambda i, e: (i, 0)),
                  pl.BlockSpec((tm, 1024), lambda i, e: (i, 0)),
                  pl.BlockSpec((1, 1024, D_FF), lambda i, e: (e, 0, 0)),
                  pl.BlockSpec((1, 1024, D_FF), lambda i, e: (e, 0, 0)),
                  pl.BlockSpec((1, D_FF, 1024), lambda i, e: (e, 0, 0))],
        out_specs=pl.BlockSpec((tm, 1024), lambda i, e: (i, 0)),
        out_shape=jax.ShapeDtypeStruct((n, 1024), F32),
        scratch_shapes=[pltpu.VMEM((tm, 1024), F32)],
        compiler_params=_params("parallel", "arbitrary"),
        name="moe",
    )(xn, comb, x1, wg, wu, wd)


def _row_copy(src_hbm, src_row, dst, dst_row, sem):
    return pltpu.make_async_copy(src_hbm.at[pl.ds(src_row * ROW_TILES, ROW_TILES)],
                                 dst.at[pl.ds(dst_row * ROW_TILES, ROW_TILES)], sem)


def _gather_rows(idx_ref, src_hbm, dst, sem, n_rows):
    def body(r, carry):
        _row_copy(src_hbm, idx_ref[r], dst, r, sem).start()
        return carry
    lax.fori_loop(0, n_rows, body, 0)


def _wait_rows(src_hbm, dst, sem):
    pltpu.make_async_copy(src_hbm.at[pl.ds(0, dst.shape[0])], dst, sem).wait()


def _rows_to_mat(buf, n_rows):
    return jnp.concatenate([buf[pl.ds(s, n_rows, stride=ROW_TILES), :] for s in range(ROW_TILES)], axis=1)


def _moe_scatter_kernel(p1_ref, p2_ref, xn_ref, xs_in, xs_hbm, sem):
    del xs_in
    i = pl.program_id(0)
    tm = p1_ref.shape[0]

    def wait_tile():
        for _ in range(2):
            pltpu.make_async_copy(xn_ref, xs_hbm.at[pl.ds(0, tm * ROW_TILES)], sem.at[0]).wait()

    def body(r, carry):
        src = xn_ref.at[pl.ds(r * ROW_TILES, ROW_TILES)]
        pltpu.make_async_copy(src, xs_hbm.at[pl.ds(p1_ref[r] * ROW_TILES, ROW_TILES)], sem.at[0]).start()
        pltpu.make_async_copy(src, xs_hbm.at[pl.ds(p2_ref[r] * ROW_TILES, ROW_TILES)], sem.at[0]).start()
        return carry

    lax.fori_loop(0, tm, body, 0)
    wait_tile()


def _moe_scatter(xg, pos1, pos2, rows, tm):
    n = pos1.shape[0]
    idx = pl.BlockSpec((tm,), lambda i: (i,), memory_space=pltpu.SMEM)
    xs0 = jnp.zeros((rows * ROW_TILES, LANES), F32)
    return pl.pallas_call(
        _moe_scatter_kernel,
        grid=(n // tm,),
        in_specs=[idx, idx, pl.BlockSpec((tm * ROW_TILES, LANES), lambda i: (i, 0)),
                  pl.BlockSpec(memory_space=pl.ANY)],
        out_specs=pl.BlockSpec(memory_space=pl.ANY),
        out_shape=jax.ShapeDtypeStruct(xs0.shape, F32),
        scratch_shapes=[pltpu.SemaphoreType.DMA((1,))],
        input_output_aliases={3: 0},
        compiler_params=_params("arbitrary"),
        name="moe_scatter",
    )(pos1, pos2, xg, xs0)


def _moe_ffn_kernel(te_ref, xs_ref, wg_ref, wu_ref, wd_ref, ys_ref, *, tp):
    del te_ref
    x = _rows_to_mat(xs_ref, tp).astype(BF16)
    hg = _dot(x, wg_ref[0])
    hu = _dot(x, wu_ref[0])
    hid = (hg * jax.nn.sigmoid(hg) * hu).astype(BF16)
    out = _dot(hid, wd_ref[0])
    for s in range(ROW_TILES):
        ys_ref[pl.ds(s, tp, stride=ROW_TILES), :] = out[:, s * LANES:(s + 1) * LANES]


def _moe_ffn(xs, tile_expert, wg, wu, wd, tp):
    n_tiles = tile_expert.shape[0]
    rows = pl.BlockSpec((tp * ROW_TILES, LANES), lambda i, te: (i, 0))
    wsp = lambda a, b: pl.BlockSpec((1, a, b), lambda i, te: (te[i], 0, 0))
    grid_spec = pltpu.PrefetchScalarGridSpec(
        num_scalar_prefetch=1,
        grid=(n_tiles,),
        in_specs=[rows, wsp(1024, D_FF), wsp(1024, D_FF), wsp(D_FF, 1024)],
        out_specs=rows,
    )
    return pl.pallas_call(
        functools.partial(_moe_ffn_kernel, tp=tp),
        grid_spec=grid_spec,
        out_shape=jax.ShapeDtypeStruct(xs.shape, F32),
        compiler_params=_params("parallel"),
        name="moe_ffn",
    )(tile_expert, xs, wg, wu, wd)


def _moe_combine_kernel(p1_ref, p1n_ref, p2_ref, p2n_ref, ys_hbm, x1_ref, w1_ref, w2_ref, y_ref, buf, sem, *, tm):
    i = pl.program_id(0)
    slot = lax.rem(i, 2)

    def gather(a_ref, b_ref, s):
        _gather_rows(a_ref, ys_hbm, buf.at[s, 0], sem.at[s, 0], tm)
        _gather_rows(b_ref, ys_hbm, buf.at[s, 1], sem.at[s, 1], tm)

    @pl.when(i == 0)
    def _():
        gather(p1_ref, p2_ref, 0)

    @pl.when(i + 1 < pl.num_programs(0))
    def _():
        gather(p1n_ref, p2n_ref, 1 - slot)

    _wait_rows(ys_hbm, buf.at[slot, 0], sem.at[slot, 0])
    _wait_rows(ys_hbm, buf.at[slot, 1], sem.at[slot, 1])
    y_ref[...] = x1_ref[...] + (w1_ref[...] * _rows_to_mat(buf.at[slot, 0], tm)
                                + w2_ref[...] * _rows_to_mat(buf.at[slot, 1], tm))


def _moe_combine(ys, pos1, pos2, w1, w2, x1, tm):
    n = x1.shape[0]
    n_tiles = n // tm
    idx = lambda off: pl.BlockSpec((tm,), lambda i: (jnp.minimum(i + off, n_tiles - 1),),
                                   memory_space=pltpu.SMEM)
    col = pl.BlockSpec((tm, 1), lambda i: (i, 0))
    return pl.pallas_call(
        functools.partial(_moe_combine_kernel, tm=tm),
        grid=(n_tiles,),
        in_specs=[idx(0), idx(1), idx(0), idx(1), pl.BlockSpec(memory_space=pl.ANY),
                  pl.BlockSpec((tm, 1024), lambda i: (i, 0)), col, col],
        out_specs=pl.BlockSpec((tm, 1024), lambda i: (i, 0)),
        out_shape=jax.ShapeDtypeStruct((n, 1024), F32),
        scratch_shapes=[pltpu.VMEM((2, 2, tm * ROW_TILES, LANES), F32), pltpu.SemaphoreType.DMA((2, 2))],
        compiler_params=_params("arbitrary"),
        name="moe_combine",
    )(pos1, pos1, pos2, pos2, ys, x1, w1, w2)


def _rank_kernel(picks_ref, tri_ref, r1_ref, r2_ref, cnt_ref, base_scr):
    @pl.when(pl.program_id(0) == 0)
    def _():
        base_scr[...] = jnp.zeros(base_scr.shape, F32)

    picks = picks_ref[...]
    lane = lax.broadcasted_iota(jnp.int32, picks.shape, 1)
    oh1 = lane == picks[:, 0:1].astype(jnp.int32)
    oh2 = lane == picks[:, 1:2].astype(jnp.int32)
    both = jnp.where(oh1 | oh2, 1.0, 0.0)
    before = _dot(tri_ref[...], both.astype(BF16)) + base_scr[...]
    r1_ref[...] = jnp.sum(jnp.where(oh1, before, 0.0), axis=-1, keepdims=True).astype(jnp.int32)
    r2_ref[...] = jnp.sum(jnp.where(oh2, before, 0.0), axis=-1, keepdims=True).astype(jnp.int32)
    base_scr[...] += jnp.sum(both, axis=0, keepdims=True)
    cnt_ref[...] = base_scr[...]


def _pos_kernel(picks_ref, r1_ref, r2_ref, start_ref, p1_ref, p2_ref):
    picks = picks_ref[...]
    lane = lax.broadcasted_iota(jnp.int32, picks.shape, 1)
    start = start_ref[...]
    for pick, r_ref, p_ref in ((0, r1_ref, p1_ref), (1, r2_ref, p2_ref)):
        onehot = lane == picks[:, pick:pick + 1].astype(jnp.int32)
        first = jnp.sum(jnp.where(onehot, start, 0.0), axis=-1, keepdims=True)
        p_ref[...] = r_ref[...] + first.astype(jnp.int32)


def _dispatch(picks, tp, tm):
    n = picks.shape[0]
    tri = jnp.asarray(np.tril(np.ones((tm, tm), np.float32), -1), BF16)
    col = pl.BlockSpec((tm, 1), lambda i: (i, 0))
    r1, r2, cnt = pl.pallas_call(
        _rank_kernel,
        grid=(n // tm,),
        in_specs=[pl.BlockSpec((tm, LANES), lambda i: (i, 0)), pl.BlockSpec((tm, tm), lambda i: (0, 0))],
        out_specs=(col, col, pl.BlockSpec((1, LANES), lambda i: (0, 0))),
        out_shape=(jax.ShapeDtypeStruct((n, 1), jnp.int32), jax.ShapeDtypeStruct((n, 1), jnp.int32),
                   jax.ShapeDtypeStruct((1, LANES), F32)),
        scratch_shapes=[pltpu.VMEM((1, LANES), F32)],
        compiler_params=_params("arbitrary"),
        name="moe_rank",
    )(picks, tri)
    counts = cnt[0, :N_EXPERTS].astype(jnp.int32)
    tiles = (counts + tp - 1) // tp
    tile_end = jnp.cumsum(tiles)
    row_start = jnp.zeros((1, LANES), F32).at[0, :N_EXPERTS].set(((tile_end - tiles) * tp).astype(F32))
    pos1, pos2 = pl.pallas_call(
        _pos_kernel,
        grid=(n // tm,),
        in_specs=[pl.BlockSpec((tm, LANES), lambda i: (i, 0)), col, col, pl.BlockSpec((1, LANES), lambda i: (0, 0))],
        out_specs=(col, col),
        out_shape=(jax.ShapeDtypeStruct((n, 1), jnp.int32), jax.ShapeDtypeStruct((n, 1), jnp.int32)),
        compiler_params=_params("parallel"),
        name="moe_pos",
    )(picks, r1, r2, row_start)
    n_tiles = (2 * n) // tp + N_EXPERTS
    tile_ids = jnp.arange(n_tiles, dtype=jnp.int32)
    tile_expert = jnp.minimum(jnp.sum((tile_end[None, :] <= tile_ids[:, None]).astype(jnp.int32), axis=1),
                              N_EXPERTS - 1)
    return pos1[:, 0], pos2[:, 0], tile_expert, n_tiles * tp


def _pick(n, pref):
    t = min(n, pref)
    while n % t:
        t //= 2
    return t


def kernel(x_prompt, x_sample, cache_k, cache_v, state_hgrn, page_table, rms_in, w_in, att_q_norm, att_k_norm, att_lambda, att_sub_norm, hg_lower_bound, hg_out_norm, w_branch_att, w_branch_hg, w_out, rms_ffn, w_router_group, b_router_group, w_router_expert, b_router_expert, w_exp_gate, w_exp_up, w_exp_down):
    depth = rms_in.shape[0]
    assert depth == 1 and hg_lower_bound.shape[0] == 2
    b, t, _ = x_prompt.shape
    db, ds, _ = x_sample.shape
    assert ds == 1
    lam_init = 0.8 - 0.6 * math.exp(-0.3 * 0)

    w_in_bf = w_in[0].astype(BF16)
    qn_row = jnp.tile(att_q_norm[0], 2).reshape(1, LANES)
    kn_row = jnp.tile(att_k_norm[0], 2).reshape(1, LANES)
    seg_np = (np.arange(LANES)[:, None] // ATT_DHALF) == (np.arange(LANES)[None, :] // ATT_DHALF)
    seg = jnp.asarray(seg_np.astype(np.float32), BF16)
    sub_row = att_sub_norm[0].reshape(1, LANES)
    hgn_row = hg_out_norm[0].reshape(1, LANES)
    wpa = w_branch_att[0].astype(BF16)
    wph = w_branch_hg[0].astype(BF16)
    wo = w_out[0].astype(BF16)
    wr = jnp.zeros((D_MODEL, LANES), F32)
    wr = wr.at[:, :N_EXPERTS].set(w_router_expert[0]).at[:, N_EXPERTS:N_EXPERTS + N_GROUPS].set(w_router_group[0])
    wr = wr.astype(BF16)
    br = jnp.zeros((1, LANES), F32)
    br = br.at[0, :N_EXPERTS].set(b_router_expert[0]).at[0, N_EXPERTS:N_EXPERTS + N_GROUPS].set(b_router_group[0])
    weg = w_exp_gate[0].astype(BF16)
    weu = w_exp_up[0].astype(BF16)
    wed = w_exp_down[0].astype(BF16)
    gin = rms_in[0].reshape(1, D_MODEL)
    gffn = rms_ffn[0].reshape(1, D_MODEL)
    lam_p = att_lambda[0]

    def tail(x2, oa, oh, sga, sgh, tm_merge, tm_moe):
        sparse = x2.shape[0] >= 16 * MOE_TILE
        x1, xn, route = _merge(oa, oh, sga, sgh, x2, wpa, wph, wo, gffn, wr, br, tm_merge, sparse)
        if not sparse:
            return _moe(xn, route, x1, weg, weu, wed, tm_moe)
        pos1, pos2, tile_expert, rows = _dispatch(route, MOE_TILE, _pick(x2.shape[0], 512))
        xs = _moe_scatter(xn, pos1, pos2, rows, _pick(x2.shape[0], 512))
        ys = _moe_ffn(xs, tile_expert, weg, weu, wed, MOE_TILE)
        return _moe_combine(ys, pos1, pos2, route[:, 2:3], route[:, 3:4], x1, _pick(x2.shape[0], 256))

    n = b * t
    xp = x_prompt.reshape(n, D_MODEL)
    q, kt, v, hq, logf, kk, hv, og, sga, sgh = _inproj(
        xp, gin, w_in_bf, qn_row, kn_row, hg_lower_bound, seg, _pick(t, 1024), (b, t))
    oa = _attn_prompt(q.reshape(b, t, 1024), kt, v.reshape(b, t * ATT_KV_HEADS, LANES), lam_p, sub_row, lam_init,
                      _pick(t, 512))
    r3 = lambda a: a.reshape(b, t, 1024)
    oh, state_p = _hgrn_prompt(r3(hq), r3(logf), r3(kk), r3(hv), r3(og), hgn_row, _pick(t, 1024))
    yp = tail(xp, oa.reshape(n, 1024), oh.reshape(n, 1024), sga, sgh, _pick(n, 512), _pick(n, 1024))
    k_prompt = jnp.transpose(kt.reshape(1, b, ATT_KV_HEADS, 2, ATT_DHALF, t), (0, 1, 5, 2, 3, 4))

    xs = x_sample.reshape(db, D_MODEL)
    q, ks, vs, hq, logf, kk, hv, og, sga, sgh = _inproj(
        xs, gin, w_in_bf, qn_row, kn_row, hg_lower_bound, seg, _pick(db, 1024), None)
    q5 = q.reshape(db, ATT_KV_HEADS, 2, 2, ATT_DHALF)
    eye_n = jnp.eye(ATT_KV_HEADS, dtype=BF16)
    eye_c = jnp.eye(2, dtype=BF16)
    wq = jnp.einsum('bngcd,nm,ce->bcngmed', q5, eye_n, eye_c).reshape(db, 16, 512)
    n_pool = cache_k.shape[1]
    page = cache_k.shape[2]
    kt_pages = jnp.transpose(cache_k[0], (0, 2, 3, 4, 1)).reshape(n_pool, 512, page)
    v_pages = cache_v[0].reshape(n_pool, page * ATT_KV_HEADS, ATT_DV)
    oa_s = _attn_sample(wq, ks.reshape(db, 1, 512), vs.reshape(db, 1, 512), kt_pages, v_pages,
                        page_table, lam_p, sub_row, lam_init, _pick(page_table.shape[1], 16))
    row = lambda a: a.astype(F32).reshape(db, HG_HEADS, HG_DK)
    oh_s, state_s = _hgrn_sample(state_hgrn[0], row(hq), row(logf), row(kk), row(hv), row(og), hgn_row)
    ys = tail(xs, oa_s.reshape(db, 1024).astype(BF16), oh_s.reshape(db, 1024).astype(BF16), sga, sgh,
              _pick(db, 512), _pick(db, 1024))

    return (yp.reshape(b, t, D_MODEL), ys.reshape(db, 1, D_MODEL),
            k_prompt, v.reshape(1, b, t, ATT_KV_HEADS, ATT_DV),
            state_p.reshape(1, b, HG_HEADS, HG_DK, HG_DV),
            ks.reshape(1, db, 1, ATT_KV_HEADS, 2, ATT_DHALF), vs.reshape(1, db, 1, ATT_KV_HEADS, ATT_DV),
            state_s.reshape(1, db, HG_HEADS, HG_DK, HG_DV))
```

```python
import functools
import math

import numpy as np
import jax
import jax.numpy as jnp
from jax import lax
from jax.experimental import pallas as pl
from jax.experimental.pallas import tpu as pltpu

F32 = jnp.float32
BF16 = jnp.bfloat16

D_MODEL = 1024
ATT_HEADS = 8
ATT_KV_HEADS = 4
ATT_DHALF = 64
ATT_DV = 128
HG_HEADS = 8
HG_DK = 128
HG_DV = 128
N_GROUPS = 4
EXP_PER_GROUP = 8
N_EXPERTS = 32
D_FF = 512
EPS = 1e-6
LANES = 128
VMEM_LIMIT = 56 * 1024 * 1024

LOG2E = 1.4426950408889634
Q_SCALE = (ATT_DHALF ** -0.5) * LOG2E

MOE_TILE = 512
HG_CHUNK = 64
HG_LEVELS = 6


def _dot(a, b):
    return jnp.dot(a, b, preferred_element_type=F32)


def _dot_nt(a, b):
    return lax.dot_general(a, b, (((1,), (1,)), ((), ())), preferred_element_type=F32)


def _dot_tn(a, b):
    return lax.dot_general(a, b, (((0,), (0,)), ((), ())), preferred_element_type=F32)


def _split3(x):
    hi = x.astype(BF16)
    r1 = x - hi.astype(F32)
    mid = r1.astype(BF16)
    lo = (r1 - mid.astype(F32)).astype(BF16)
    return hi, mid, lo


def _seg_rms(z, seg, gain_row, scale):
    outs = []
    for i in range(z.shape[1] // LANES):
        zi = z[:, i * LANES:(i + 1) * LANES]
        z2 = zi * zi
        hi = z2.astype(BF16)
        lo = (z2 - hi.astype(F32)).astype(BF16)
        ss = _dot(hi, seg) + _dot(lo, seg)
        y = zi * lax.rsqrt(ss * (1.0 / ATT_DHALF) + EPS) * gain_row
        if scale != 1.0:
            y = y * scale
        outs.append(y)
    return jnp.concatenate(outs, axis=1)


def _params(*sem):
    return pltpu.CompilerParams(dimension_semantics=sem, vmem_limit_bytes=VMEM_LIMIT)


def _rms_kernel(x_ref, g_ref, h_ref):
    x = x_ref[...]
    ms = jnp.mean(x * x, axis=-1, keepdims=True)
    h_ref[...] = (x * lax.rsqrt(ms + EPS) * g_ref[...]).astype(BF16)


def _rms_cast(x, g, tm):
    n = x.shape[0]
    return pl.pallas_call(
        _rms_kernel,
        grid=(n // tm,),
        in_specs=[pl.BlockSpec((tm, D_MODEL), lambda i: (i, 0)), pl.BlockSpec((1, D_MODEL), lambda i: (0, 0))],
        out_specs=pl.BlockSpec((tm, D_MODEL), lambda i: (i, 0)),
        out_shape=jax.ShapeDtypeStruct((n, D_MODEL), BF16),
        compiler_params=_params("parallel"),
        name="rms_in",
    )(x, g)


def _proj_kernel(h_ref, w_ref, *refs, mode, n_aux):
    aux, outs = refs[:n_aux], refs[n_aux:]
    z = _dot(h_ref[...], w_ref[...])
    if mode == "q":
        outs[0][...] = _seg_rms(z, aux[0][...], aux[1][...], Q_SCALE).astype(BF16)
    elif mode in ("kv", "kv_t"):
        k = _seg_rms(z[:, :512], aux[0][...], aux[1][...], 1.0)
        if mode == "kv_t":
            outs[0][0] = k.T
            tm = z.shape[0]
            for n in range(ATT_KV_HEADS):
                outs[1][pl.ds(n, tm, stride=ATT_KV_HEADS), :] = z[:, 512 + n * LANES:512 + (n + 1) * LANES]
        else:
            outs[0][...] = k
            outs[1][...] = z[:, 512:]
    elif mode == "hq":
        outs[0][...] = (z * (HG_DK ** -0.5)).astype(BF16)
    elif mode == "hf":
        lbp = aux[0][...]
        m = jnp.max(lbp, axis=0, keepdims=True)
        e = jnp.exp(lbp - m)
        lb = e[0:1] / jnp.sum(e, axis=0, keepdims=True)
        f = lb + (1.0 - lb) * jax.nn.sigmoid(z)
        outs[0][...] = jnp.log(f)
        outs[1][...] = (1.0 - lb) * jax.nn.sigmoid(-z)
    elif mode == "hv":
        outs[0][...] = z.astype(BF16)
    elif mode == "silu":
        outs[0][...] = z * jax.nn.sigmoid(z)
    elif mode == "sigmoid":
        outs[0][...] = jax.nn.sigmoid(z)


def _proj(h, w, col, mode, aux, outs, tm, name):
    n = h.shape[0]
    aux_specs = [pl.BlockSpec(a.shape, lambda i: (0, 0)) for a in aux]
    res = pl.pallas_call(
        functools.partial(_proj_kernel, mode=mode, n_aux=len(aux)),
        grid=(n // tm,),
        in_specs=[pl.BlockSpec((tm, D_MODEL), lambda i: (i, 0)),
                  pl.BlockSpec((D_MODEL, 1024), lambda i, col=col: (0, col))] + aux_specs,
        out_specs=tuple(pl.BlockSpec(o[2], o[3]) for o in outs),
        out_shape=tuple(jax.ShapeDtypeStruct(o[0], o[1]) for o in outs),
        compiler_params=_params("parallel"),
        name=name,
    )(h, w, *aux)
    return res


def _inproj(x, gin, w, qn_row, kn_row, lbp, seg, tm, seq):
    n = x.shape[0]
    h = _rms_cast(x, gin, tm)
    tok = lambda width, dt: ((n, width), dt, (tm, width), lambda i: (i, 0))
    q, = _proj(h, w, 0, "q", [seg, qn_row], [tok(1024, BF16)], tm, "proj_q")
    if seq is None:
        k, v = _proj(h, w, 1, "kv", [seg, kn_row], [tok(512, F32), tok(512, F32)], tm, "proj_kv")
    else:
        b, t = seq
        tpb = t // tm
        k_out = ((b, 512, t), F32, (1, 512, tm), lambda i: (i // tpb, 0, i % tpb))
        v_out = ((n * ATT_KV_HEADS, LANES), F32, (tm * ATT_KV_HEADS, LANES), lambda i: (i, 0))
        k, v = _proj(h, w, 1, "kv_t", [seg, kn_row], [k_out, v_out], tm, "proj_kv")
    hq, = _proj(h, w, 2, "hq", [], [tok(1024, BF16)], tm, "proj_hq")
    logf, kk = _proj(h, w, 3, "hf", [lbp], [tok(1024, F32), tok(1024, F32)], tm, "proj_hf")
    hv, = _proj(h, w, 4, "hv", [], [tok(1024, BF16)], tm, "proj_hv")
    og, = _proj(h, w, 5, "silu", [], [tok(1024, F32)], tm, "proj_og")
    sga, = _proj(h, w, 6, "sigmoid", [], [tok(1024, F32)], tm, "proj_ga")
    sgh, = _proj(h, w, 7, "sigmoid", [], [tok(1024, F32)], tm, "proj_gh")
    return q, k, v, hq, logf, kk, hv, og, sga, sgh


def _diff_lambda(lp, lam_init):
    a = jnp.sum(lp[0:1] * lp[1:2], axis=-1, keepdims=True)
    b = jnp.sum(lp[2:3] * lp[3:4], axis=-1, keepdims=True)
    return jnp.exp(a) - jnp.exp(b) + lam_init


def _sub_norm(o, sub_row, lam_init):
    ms = jnp.mean(o * o, axis=-1, keepdims=True)
    return o * lax.rsqrt(ms + EPS) * sub_row * (1.0 - lam_init)


def _attn_kernel(q_ref, kt_ref, v_ref, lamp_ref, sub_ref, o_ref,
                 kb_scr, vb_scr, qs_scr, m_scr, acc_scr, *, tq, lam_init):
    qi = pl.program_id(2)

    @pl.when(qi == 0)
    def _():
        kb_scr[...] = kt_ref[0].astype(BF16)
        t_len = vb_scr.shape[0]
        vb_scr[:, 0:LANES] = v_ref[0, pl.ds(pl.program_id(1), t_len, stride=ATT_KV_HEADS), :].astype(BF16)
        vb_scr[:, LANES:2 * LANES] = jnp.ones((vb_scr.shape[0], LANES), BF16)

    q = q_ref[0].astype(F32)
    lane = lax.broadcasted_iota(jnp.int32, (tq, LANES), 1)
    for g in range(2):
        qg = q[:, g * LANES:(g + 1) * LANES]
        for c in range(2):
            keep = (lane < ATT_DHALF) if c == 0 else (lane >= ATT_DHALF)
            r = g * 2 + c
            qs_scr[r * tq:(r + 1) * tq, :] = jnp.where(keep, qg, 0.0).astype(BF16)
    m_scr[...] = jnp.full(m_scr.shape, -jnp.inf, F32)
    acc_scr[...] = jnp.zeros(acc_scr.shape, F32)

    def chunk(j, masked):
        off = pl.multiple_of(j * tq, tq)
        s = _dot(qs_scr[...], kb_scr[:, pl.ds(off, tq)])
        if masked:
            row = lax.rem(lax.broadcasted_iota(jnp.int32, s.shape, 0), tq)
            s = jnp.where(lax.broadcasted_iota(jnp.int32, s.shape, 1) <= row, s, -jnp.inf)
        m_prev = m_scr[...]
        m_new = jnp.maximum(m_prev, jnp.max(s, axis=-1, keepdims=True))
        alpha = jnp.exp2(m_prev - m_new)
        pr = jnp.exp2(s - jnp.concatenate([m_new] * (tq // LANES), axis=1))
        pv = _dot(pr.astype(BF16), vb_scr[pl.ds(off, tq), :])
        acc_scr[...] = jnp.concatenate([alpha, alpha], axis=1) * acc_scr[...] + pv
        m_scr[...] = m_new

    def body(j, carry):
        chunk(j, False)
        return carry

    lax.fori_loop(0, qi, body, 0)
    chunk(qi, True)

    lam = _diff_lambda(lamp_ref[...], lam_init)
    acc = acc_scr[...]
    out = acc[:, 0:LANES] / acc[:, LANES:2 * LANES]
    for g in range(2):
        o0 = out[(2 * g) * tq:(2 * g + 1) * tq]
        o1 = out[(2 * g + 1) * tq:(2 * g + 2) * tq]
        o_ref[0, :, g * LANES:(g + 1) * LANES] = _sub_norm(o0 - lam * o1, sub_ref[...], lam_init).astype(BF16)


def _attn_prompt(q, kt, v, lam_p, sub_row, lam_init, tq):
    b, t, _ = q.shape
    return pl.pallas_call(
        functools.partial(_attn_kernel, tq=tq, lam_init=lam_init),
        grid=(b, ATT_KV_HEADS, t // tq),
        in_specs=[
            pl.BlockSpec((1, tq, 256), lambda bi, n, qi: (bi, qi, n)),
            pl.BlockSpec((1, LANES, t), lambda bi, n, qi: (bi, n, 0)),
            pl.BlockSpec((1, t * ATT_KV_HEADS, LANES), lambda bi, n, qi: (bi, 0, 0)),
            pl.BlockSpec((4, ATT_DHALF), lambda bi, n, qi: (0, 0)),
            pl.BlockSpec((1, LANES), lambda bi, n, qi: (0, 0)),
        ],
        out_specs=pl.BlockSpec((1, tq, 256), lambda bi, n, qi: (bi, qi, n)),
        out_shape=jax.ShapeDtypeStruct((b, t, 1024), BF16),
        scratch_shapes=[pltpu.VMEM((LANES, t), BF16), pltpu.VMEM((t, 2 * LANES), BF16),
                        pltpu.VMEM((4 * tq, LANES), BF16), pltpu.VMEM((4 * tq, LANES), F32),
                        pltpu.VMEM((4 * tq, 2 * LANES), F32)],
        compiler_params=_params("parallel", "parallel", "arbitrary"),
        name="attn_prompt",
    )(q, kt, v, lam_p, sub_row)


def _hgrn_tables(c, levels):
    t = np.arange(c)[:, None]
    u = np.arange(c)[None, :]
    w = [(u <= t), (u > t)]
    masks = [(t == u)]
    for l in range(1, levels + 1):
        n = 1 << l
        half = n // 2
        mid = (t // n) * n + half
        hi = t >= mid
        w.append(np.where(hi, (u > mid) & (u <= t), (u > t) & (u <= mid)))
        masks.append((t // n == u // n) & (t % n >= half) & (u % n < half))
    w = np.concatenate(w, axis=0).astype(np.float32)
    w3 = np.concatenate([w, w, w], axis=1)
    if len(masks) % 2:
        masks.append(np.zeros_like(masks[0]))
    pairs = [np.concatenate([masks[i], masks[i + 1]], axis=1) for i in range(0, len(masks), 2)]
    return w3, np.stack(pairs).astype(np.float32)


def _hgrn_kernel(q_ref, g_ref, k_ref, v_ref, og_ref, norm_ref, w_ref, mask_ref, o_ref, s_ref,
                 st_scr, *, c, levels, n_chunks):
    ti = pl.program_id(2)

    @pl.when(ti == 0)
    def _():
        st_scr[...] = jnp.zeros(st_scr.shape, F32)

    w = w_ref[...]
    zeros = jnp.zeros((c, LANES), BF16)
    chunk_rows = [slice(ci * c, (ci + 1) * c) for ci in range(n_chunks)]
    g3 = jnp.concatenate([jnp.concatenate(_split3(g_ref[0, rows, :]), axis=0) for rows in chunk_rows], axis=1)
    e_all = jnp.exp(_dot(w, g3))

    parts = []
    for ci, rows in enumerate(chunk_rows):
        q = q_ref[0, rows, :].astype(F32)
        k = k_ref[0, rows, :]
        v = v_ref[0, rows, :]
        e = e_all[:, ci * LANES:(ci + 1) * LANES]
        e_b = e[0:c]
        e_k = e[c:2 * c]
        qs = [q.astype(BF16)]
        ks = [k.astype(BF16)]
        for l in range(1, levels + 1):
            e_l = e[(l + 1) * c:(l + 2) * c]
            qs.append((q * e_l).astype(BF16))
            ks.append((k * e_l).astype(BF16))
        if len(qs) % 2:
            qs.append(zeros)
            ks.append(zeros)
        a2 = None
        for p in range(len(qs) // 2):
            ql = jnp.concatenate([qs[2 * p], qs[2 * p + 1]], axis=1)
            kbd = jnp.concatenate([jnp.concatenate([ks[2 * p], zeros], axis=1),
                                   jnp.concatenate([zeros, ks[2 * p + 1]], axis=1)], axis=0)
            term = mask_ref[p] * _dot_nt(ql, kbd)
            a2 = term if a2 is None else a2 + term
        o_intra = _dot(a2.astype(BF16), jnp.concatenate([v, v], axis=0))
        st_add = _dot_tn(v, (k * e_k).astype(BF16))
        parts.append(((q * e_b).astype(BF16), e_b[c - 1:c, :], o_intra, st_add))

    st = st_scr[...]
    for rows, (q_b, decay, o_intra, st_add) in zip(chunk_rows, parts):
        o = _dot_nt(q_b, st.astype(BF16)) + o_intra
        st = st * decay + st_add
        ms = jnp.mean(o * o, axis=-1, keepdims=True)
        y = o * lax.rsqrt(ms + EPS) * norm_ref[...] * og_ref[0, rows, :]
        o_ref[0, rows, :] = y.astype(BF16)
    st_scr[...] = st

    @pl.when(ti == pl.num_programs(2) - 1)
    def _():
        s_ref[0, 0] = st.T


def _hgrn_prompt(hq, logf, kk, hv, og, norm_row, ct):
    b, t, _ = hq.shape
    c, levels = HG_CHUNK, HG_LEVELS
    w_np, m_np = _hgrn_tables(c, levels)
    w = jnp.asarray(w_np, BF16)
    masks = jnp.asarray(m_np, F32)
    tok = pl.BlockSpec((1, ct, LANES), lambda bi, h, ti: (bi, ti, h))
    return pl.pallas_call(
        functools.partial(_hgrn_kernel, c=c, levels=levels, n_chunks=ct // c),
        grid=(b, HG_HEADS, t // ct),
        in_specs=[tok, tok, tok, tok, tok,
                  pl.BlockSpec((1, LANES), lambda bi, h, ti: (0, 0)),
                  pl.BlockSpec(w.shape, lambda bi, h, ti: (0, 0)),
                  pl.BlockSpec(masks.shape, lambda bi, h, ti: (0, 0, 0))],
        out_specs=(tok, pl.BlockSpec((1, 1, HG_DK, HG_DV), lambda bi, h, ti: (bi, h, 0, 0))),
        out_shape=(jax.ShapeDtypeStruct((b, t, 1024), BF16),
                   jax.ShapeDtypeStruct((b, HG_HEADS, HG_DK, HG_DV), F32)),
        scratch_shapes=[pltpu.VMEM((HG_DV, HG_DK), F32)],
        compiler_params=_params("parallel", "parallel", "arbitrary"),
        name="hgrn_prompt",
    )(hq, logf, kk, hv, og, norm_row, w, masks)


def _dec_attn_kernel(pt_ref, wq_ref, kn_ref, vn_ref, lamp_ref, sub_ref, *refs, pages, lam_init):
    k_refs = refs[:pages]
    v_refs = refs[pages:2 * pages]
    o_ref = refs[2 * pages]
    m_scr, l_scr, acc_scr = refs[2 * pages + 1:]
    j = pl.program_id(1)
    page = k_refs[0].shape[2]

    @pl.when(j == 0)
    def _():
        m_scr[...] = jnp.full(m_scr.shape, -jnp.inf, F32)
        l_scr[...] = jnp.zeros(l_scr.shape, F32)
        acc_scr[...] = jnp.zeros(acc_scr.shape, F32)

    rown = lax.rem(lax.broadcasted_iota(jnp.int32, (16, LANES), 0), 8) // 2
    wq = wq_ref[0]
    s = jnp.concatenate([_dot(wq, k_refs[i][0].astype(BF16)) for i in range(pages)], axis=1)
    m_prev = m_scr[...]
    m_new = jnp.maximum(m_prev, jnp.max(s, axis=-1, keepdims=True))
    alpha = jnp.exp2(m_prev - m_new)
    pr = jnp.exp2(s - m_new)
    l_scr[...] = alpha * l_scr[...] + jnp.sum(pr, axis=-1, keepdims=True)
    prb = pr.astype(BF16)
    pv = jnp.zeros((16, LANES), F32)
    for n in range(ATT_KV_HEADS):
        v_n = jnp.concatenate([v_refs[i][0, pl.ds(n, page, stride=ATT_KV_HEADS), :] for i in range(pages)],
                              axis=0).astype(BF16)
        pv = pv + jnp.where(rown == n, _dot(prb, v_n), 0.0)
    acc_scr[...] = alpha * acc_scr[...] + pv
    m_scr[...] = m_new

    @pl.when(j == pl.num_programs(1) - 1)
    def _():
        s_cur = jnp.sum(wq.astype(F32) * kn_ref[0], axis=-1, keepdims=True)
        m_old = m_scr[...]
        m_fin = jnp.maximum(m_old, s_cur)
        al = jnp.exp2(m_old - m_fin)
        pc = jnp.exp2(s_cur - m_fin)
        l_fin = al * l_scr[...] + pc
        vn = vn_ref[0]
        v_cur = jnp.zeros((16, LANES), F32)
        for n in range(ATT_KV_HEADS):
            v_cur = v_cur + jnp.where(rown == n, vn[:, n * LANES:(n + 1) * LANES], 0.0)
        out = (al * acc_scr[...] + pc * v_cur) / l_fin
        lam = _diff_lambda(lamp_ref[...], lam_init)
        o = out[0:8] - lam * out[8:16]
        o_ref[0] = _sub_norm(o, sub_ref[...], lam_init)


def _attn_sample(wq, k_new, v_new, kt_pages, v_pages, page_table, lam_p, sub_row, lam_init, pages):
    db, n_pages = page_table.shape
    pt_flat = page_table.reshape(-1)
    blk = kt_pages.shape[1:]

    def page_spec(i):
        return pl.BlockSpec((1,) + blk, lambda bi, j, pt, i=i: (pt[bi * n_pages + j * pages + i], 0, 0))

    row3 = lambda w: pl.BlockSpec((1, 1, w), lambda bi, j, pt: (bi, 0, 0))
    grid_spec = pltpu.PrefetchScalarGridSpec(
        num_scalar_prefetch=1,
        grid=(db, n_pages // pages),
        in_specs=[pl.BlockSpec((1, 16, 512), lambda bi, j, pt: (bi, 0, 0)), row3(512), row3(512),
                  pl.BlockSpec((4, ATT_DHALF), lambda bi, j, pt: (0, 0)),
                  pl.BlockSpec((1, LANES), lambda bi, j, pt: (0, 0))]
                 + [page_spec(i) for i in range(pages)] + [page_spec(i) for i in range(pages)],
        out_specs=pl.BlockSpec((1, 8, LANES), lambda bi, j, pt: (bi, 0, 0)),
        scratch_shapes=[pltpu.VMEM((16, 1), F32), pltpu.VMEM((16, 1), F32), pltpu.VMEM((16, LANES), F32)],
    )
    return pl.pallas_call(
        functools.partial(_dec_attn_kernel, pages=pages, lam_init=lam_init),
        grid_spec=grid_spec,
        out_shape=jax.ShapeDtypeStruct((db, 8, LANES), F32),
        compiler_params=_params("parallel", "arbitrary"),
        name="attn_sample",
    )(pt_flat, wq, k_new, v_new, lam_p, sub_row, *([kt_pages] * pages), *([v_pages] * pages))


def _hgrn_step_kernel(s_ref, q_ref, g_ref, k_ref, v_ref, og_ref, norm_ref, o_ref, sn_ref):
    pad = jnp.zeros((LANES - 3 * HG_HEADS, LANES), F32)
    cols = jnp.concatenate([q_ref[0], g_ref[0], k_ref[0], pad], axis=0).T
    v_all = v_ref[0]
    og_all = og_ref[0]
    outs = []
    for h in range(HG_HEADS):
        s = s_ref[0, h]
        qc = cols[:, h:h + 1]
        f = jnp.exp(cols[:, HG_HEADS + h:HG_HEADS + h + 1])
        kc = cols[:, 2 * HG_HEADS + h:2 * HG_HEADS + h + 1]
        v = v_all[h:h + 1]
        o = jnp.sum(s * (qc * f), axis=0, keepdims=True) + jnp.sum(qc * kc, axis=0, keepdims=True) * v
        sn_ref[0, h] = s * f + kc * v
        ms = jnp.mean(o * o, axis=-1, keepdims=True)
        outs.append(o * lax.rsqrt(ms + EPS) * norm_ref[...] * og_all[h:h + 1])
    o_ref[0] = jnp.concatenate(outs, axis=0)


def _hgrn_sample(state, q, g, k, v, og, norm_row):
    db = state.shape[0]
    row = pl.BlockSpec((1, HG_HEADS, HG_DK), lambda bi: (bi, 0, 0))
    st = pl.BlockSpec((1, HG_HEADS, HG_DK, HG_DV), lambda bi: (bi, 0, 0, 0))
    return pl.pallas_call(
        _hgrn_step_kernel,
        grid=(db,),
        in_specs=[st, row, row, row, row, row, pl.BlockSpec((1, LANES), lambda bi: (0, 0))],
        out_specs=(row, st),
        out_shape=(jax.ShapeDtypeStruct((db, HG_HEADS, HG_DV), F32),
                   jax.ShapeDtypeStruct(state.shape, F32)),
        compiler_params=_params("parallel"),
        name="hgrn_sample",
    )(state, q, g, k, v, og, norm_row)


def _lane_first(cond, lane):
    return jnp.min(jnp.where(cond, lane, LANES), axis=-1, keepdims=True)


def _route(logits):
    lane = lax.broadcasted_iota(jnp.int32, logits.shape, 1)
    gmask = (lane >= N_EXPERTS) & (lane < N_EXPERTS + N_GROUPS)
    lg = jnp.where(gmask, logits, -jnp.inf)
    mg = jnp.max(lg, axis=-1, keepdims=True)
    eg = jnp.exp(lg - mg)
    pg = eg / jnp.sum(eg, axis=-1, keepdims=True)
    pg_top = jnp.max(pg, axis=-1, keepdims=True)
    gsel = _lane_first(gmask & (pg == pg_top), lane) - N_EXPERTS
    emask = (lane >= gsel * EXP_PER_GROUP) & (lane < (gsel + 1) * EXP_PER_GROUP)
    le = jnp.where(emask, logits, -jnp.inf)
    me = jnp.max(le, axis=-1, keepdims=True)
    ee = jnp.exp(le - me)
    pe = ee / jnp.sum(ee, axis=-1, keepdims=True)
    pe = jnp.where(emask, pe, -1.0)
    p1 = jnp.max(pe, axis=-1, keepdims=True)
    e1 = _lane_first(pe == p1, lane)
    pe2 = jnp.where(lane == e1, -1.0, pe)
    p2 = jnp.max(pe2, axis=-1, keepdims=True)
    e2 = _lane_first(pe2 == p2, lane)
    den = p1 + p2
    w1 = p1 / den * pg_top
    w2 = p2 / den * pg_top
    comb = jnp.where(lane == e1, w1, 0.0) + jnp.where(lane == e2, w2, 0.0)
    picks = jnp.where(lane == 0, e1.astype(F32), jnp.where(lane == 1, e2.astype(F32),
                      jnp.where(lane == 2, w1, jnp.where(lane == 3, w2, 0.0))))
    return comb, picks


ROW_TILES = D_MODEL // LANES


def _merge_kernel(oa_ref, oh_ref, sga_ref, sgh_ref, x_ref, wpa_ref, wph_ref, wo_ref, gf_ref,
                  wr_ref, br_ref, x1_ref, xn_ref, route_ref, *, sparse):
    ya = _dot(oa_ref[...], wpa_ref[...])
    yh = _dot(oh_ref[...], wph_ref[...])
    mixed = (sga_ref[...] * ya + sgh_ref[...] * yh).astype(BF16)
    x1 = x_ref[...] + _dot(mixed, wo_ref[...])
    x1_ref[...] = x1
    ms = jnp.mean(x1 * x1, axis=-1, keepdims=True)
    xn_f = x1 * lax.rsqrt(ms + EPS) * gf_ref[...]
    xn = xn_f.astype(BF16)
    comb, picks = _route(_dot(xn, wr_ref[...]) + br_ref[...])
    if sparse:
        tm = x1.shape[0]
        for s in range(ROW_TILES):
            xn_ref[pl.ds(s, tm, stride=ROW_TILES), :] = xn_f[:, s * LANES:(s + 1) * LANES]
        route_ref[...] = picks
    else:
        xn_ref[...] = xn
        route_ref[...] = comb


def _merge(oa, oh, sga, sgh, x, wpa, wph, wo, gf, wr, br, tm, sparse):
    n = x.shape[0]
    tok = pl.BlockSpec((tm, 1024), lambda i: (i, 0))
    wsp = pl.BlockSpec((1024, 1024), lambda i: (0, 0))
    if sparse:
        xn_spec = pl.BlockSpec((tm * ROW_TILES, LANES), lambda i: (i, 0))
        xn_shape = jax.ShapeDtypeStruct((n * ROW_TILES, LANES), F32)
    else:
        xn_spec, xn_shape = tok, jax.ShapeDtypeStruct((n, 1024), BF16)
    return pl.pallas_call(
        functools.partial(_merge_kernel, sparse=sparse),
        grid=(n // tm,),
        in_specs=[tok, tok, tok, tok, tok, wsp, wsp, wsp,
                  pl.BlockSpec((1, 1024), lambda i: (0, 0)),
                  pl.BlockSpec((1024, LANES), lambda i: (0, 0)),
                  pl.BlockSpec((1, LANES), lambda i: (0, 0))],
        out_specs=(tok, xn_spec, pl.BlockSpec((tm, LANES), lambda i: (i, 0))),
        out_shape=(jax.ShapeDtypeStruct((n, 1024), F32), xn_shape,
                   jax.ShapeDtypeStruct((n, LANES), F32)),
        compiler_params=_params("parallel"),
        name="merge",
    )(oa, oh, sga, sgh, x, wpa, wph, wo, gf, wr, br)


def _moe_kernel(xn_ref, comb_ref, x1_ref, wg_ref, wu_ref, wd_ref, y_ref, acc_scr):
    e = pl.program_id(1)

    @pl.when(e == 0)
    def _():
        acc_scr[...] = jnp.zeros(acc_scr.shape, F32)

    comb = comb_ref[...]
    lane = lax.broadcasted_iota(jnp.int32, comb.shape, 1)
    c = jnp.sum(jnp.where(lane == e, comb, 0.0), axis=-1, keepdims=True)
    xn = xn_ref[...]
    hg = _dot(xn, wg_ref[0])
    hu = _dot(xn, wu_ref[0])
    hid = hg * jax.nn.sigmoid(hg) * hu
    acc_scr[...] += _dot((hid * c).astype(BF16), wd_ref[0])

    @pl.when(e == pl.num_programs(1) - 1)
    def _():
        y_ref[...] = x1_ref[...] + acc_scr[...]


def _moe(xn, comb, x1, wg, wu, wd, tm):
    n = xn.shape[0]
    return pl.pallas_call(
        _moe_kernel,
        grid=(n // tm, N_EXPERTS),
        in_specs=[pl.BlockSpec((tm, 1024), lambda i, e: (i, 0)),
                  pl.BlockSpec((tm, LANES), lambda i, e: (i, 0)),
                  pl.BlockSpec((tm, 1024), lambda i, e: (i, 0)),
                  pl.BlockSpec((1, 1024, D_FF), lambda i, e: (e, 0, 0)),
                  pl.BlockSpec((1, 1024, D_FF), lambda i, e: (e, 0, 0)),
                  pl.BlockSpec((1, D_FF, 1024), lambda i, e: (e, 0, 0))],
        out_specs=pl.BlockSpec((tm, 1024), lambda i, e: (i, 0)),
        out_shape=jax.ShapeDtypeStruct((n, 1024), F32),
        scratch_shapes=[pltpu.VMEM((tm, 1024), F32)],
        compiler_params=_params("parallel", "arbitrary"),
        name="moe",
    )(xn, comb, x1, wg, wu, wd)


def _row_copy(src_hbm, src_row, dst, dst_row, sem):
    return pltpu.make_async_copy(src_hbm.at[pl.ds(src_row * ROW_TILES, ROW_TILES)],
                                 dst.at[pl.ds(dst_row * ROW_TILES, ROW_TILES)], sem)


def _wait_rows(src_hbm, dst, sem):
    pltpu.make_async_copy(src_hbm.at[pl.ds(0, dst.shape[0])], dst, sem).wait()


def _rows_to_mat(buf, n_rows):
    return jnp.concatenate([buf[pl.ds(s, n_rows, stride=ROW_TILES), :] for s in range(ROW_TILES)], axis=1)


def _moe_scatter_kernel(p1_ref, p2_ref, xn_ref, xs_in, xs_hbm, sem):
    del xs_in
    i = pl.program_id(0)
    tm = p1_ref.shape[0]

    def wait_tile():
        for _ in range(2):
            pltpu.make_async_copy(xn_ref, xs_hbm.at[pl.ds(0, tm * ROW_TILES)], sem.at[0]).wait()

    def body(r, carry):
        src = xn_ref.at[pl.ds(r * ROW_TILES, ROW_TILES)]
        pltpu.make_async_copy(src, xs_hbm.at[pl.ds(p1_ref[r] * ROW_TILES, ROW_TILES)], sem.at[0]).start(priority=0)
        pltpu.make_async_copy(src, xs_hbm.at[pl.ds(p2_ref[r] * ROW_TILES, ROW_TILES)], sem.at[0]).start(priority=1)
        return carry

    lax.fori_loop(0, tm, body, 0)
    wait_tile()


def _moe_scatter(xg, pos1, pos2, rows, tm):
    n = pos1.shape[0]
    idx = pl.BlockSpec((tm,), lambda i: (i,), memory_space=pltpu.SMEM)
    xs0 = jnp.zeros((rows * ROW_TILES, LANES), F32)
    return pl.pallas_call(
        _moe_scatter_kernel,
        grid=(n // tm,),
        in_specs=[idx, idx, pl.BlockSpec((tm * ROW_TILES, LANES), lambda i: (i, 0)),
                  pl.BlockSpec(memory_space=pl.ANY)],
        out_specs=pl.BlockSpec(memory_space=pl.ANY),
        out_shape=jax.ShapeDtypeStruct(xs0.shape, F32),
        scratch_shapes=[pltpu.SemaphoreType.DMA((1,))],
        input_output_aliases={3: 0},
        compiler_params=_params("arbitrary"),
        name="moe_scatter",
    )(pos1, pos2, xg, xs0)


def _moe_ffn_kernel(te_ref, xs_ref, wg_ref, wu_ref, wd_ref, ys_ref, *, tp):
    del te_ref
    x = _rows_to_mat(xs_ref, tp).astype(BF16)
    hg = _dot(x, wg_ref[0])
    hu = _dot(x, wu_ref[0])
    hid = (hg * jax.nn.sigmoid(hg) * hu).astype(BF16)
    out = _dot(hid, wd_ref[0])
    for s in range(ROW_TILES):
        ys_ref[pl.ds(s, tp, stride=ROW_TILES), :] = out[:, s * LANES:(s + 1) * LANES]


def _moe_ffn(xs, tile_expert, wg, wu, wd, tp):
    n_tiles = tile_expert.shape[0]
    rows = pl.BlockSpec((tp * ROW_TILES, LANES), lambda i, te: (i, 0))
    wsp = lambda a, b: pl.BlockSpec((1, a, b), lambda i, te: (te[i], 0, 0))
    grid_spec = pltpu.PrefetchScalarGridSpec(
        num_scalar_prefetch=1,
        grid=(n_tiles,),
        in_specs=[rows, wsp(1024, D_FF), wsp(1024, D_FF), wsp(D_FF, 1024)],
        out_specs=rows,
    )
    return pl.pallas_call(
        functools.partial(_moe_ffn_kernel, tp=tp),
        grid_spec=grid_spec,
        out_shape=jax.ShapeDtypeStruct(xs.shape, F32),
        compiler_params=_params("parallel"),
        name="moe_ffn",
    )(tile_expert, xs, wg, wu, wd)


def _moe_combine_kernel(p1_ref, p1n_ref, p2_ref, p2n_ref, ys_hbm, x1_ref, w1_ref, w2_ref, y_ref, buf, sem, *, tm):
    i = pl.program_id(0)
    slot = lax.rem(i, 2)

    def gather(a_ref, b_ref, s):
        def body(r, carry):
            _row_copy(ys_hbm, a_ref[r], buf.at[s, 0], r, sem.at[s, 0]).start(priority=0)
            _row_copy(ys_hbm, b_ref[r], buf.at[s, 1], r, sem.at[s, 1]).start(priority=1)
            return carry
        lax.fori_loop(0, tm, body, 0)

    @pl.when(i == 0)
    def _():
        gather(p1_ref, p2_ref, 0)

    @pl.when(i + 1 < pl.num_programs(0))
    def _():
        gather(p1n_ref, p2n_ref, 1 - slot)

    _wait_rows(ys_hbm, buf.at[slot, 0], sem.at[slot, 0])
    _wait_rows(ys_hbm, buf.at[slot, 1], sem.at[slot, 1])
    y_ref[...] = x1_ref[...] + (w1_ref[...] * _rows_to_mat(buf.at[slot, 0], tm)
                                + w2_ref[...] * _rows_to_mat(buf.at[slot, 1], tm))


def _moe_combine(ys, pos1, pos2, w1, w2, x1, tm):
    n = x1.shape[0]
    n_tiles = n // tm
    idx = lambda off: pl.BlockSpec((tm,), lambda i: (jnp.minimum(i + off, n_tiles - 1),),
                                   memory_space=pltpu.SMEM)
    col = pl.BlockSpec((tm, 1), lambda i: (i, 0))
    return pl.pallas_call(
        functools.partial(_moe_combine_kernel, tm=tm),
        grid=(n_tiles,),
        in_specs=[idx(0), idx(1), idx(0), idx(1), pl.BlockSpec(memory_space=pl.ANY),
                  pl.BlockSpec((tm, 1024), lambda i: (i, 0)), col, col],
        out_specs=pl.BlockSpec((tm, 1024), lambda i: (i, 0)),
        out_shape=jax.ShapeDtypeStruct((n, 1024), F32),
        scratch_shapes=[pltpu.VMEM((2, 2, tm * ROW_TILES, LANES), F32), pltpu.SemaphoreType.DMA((2, 2))],
        compiler_params=_params("arbitrary"),
        name="moe_combine",
    )(pos1, pos1, pos2, pos2, ys, x1, w1, w2)


def _rank_kernel(picks_ref, tri_ref, r1_ref, r2_ref, cnt_ref, base_scr):
    @pl.when(pl.program_id(0) == 0)
    def _():
        base_scr[...] = jnp.zeros(base_scr.shape, F32)

    picks = picks_ref[...]
    lane = lax.broadcasted_iota(jnp.int32, picks.shape, 1)
    oh1 = lane == picks[:, 0:1].astype(jnp.int32)
    oh2 = lane == picks[:, 1:2].astype(jnp.int32)
    both = jnp.where(oh1 | oh2, 1.0, 0.0)
    before = _dot(tri_ref[...], both.astype(BF16)) + base_scr[...]
    r1_ref[...] = jnp.sum(jnp.where(oh1, before, 0.0), axis=-1, keepdims=True).astype(jnp.int32)
    r2_ref[...] = jnp.sum(jnp.where(oh2, before, 0.0), axis=-1, keepdims=True).astype(jnp.int32)
    base_scr[...] += jnp.sum(both, axis=0, keepdims=True)
    cnt_ref[...] = base_scr[...]


def _pos_kernel(picks_ref, r1_ref, r2_ref, start_ref, p1_ref, p2_ref):
    picks = picks_ref[...]
    lane = lax.broadcasted_iota(jnp.int32, picks.shape, 1)
    start = start_ref[...]
    for pick, r_ref, p_ref in ((0, r1_ref, p1_ref), (1, r2_ref, p2_ref)):
        onehot = lane == picks[:, pick:pick + 1].astype(jnp.int32)
        first = jnp.sum(jnp.where(onehot, start, 0.0), axis=-1, keepdims=True)
        p_ref[...] = r_ref[...] + first.astype(jnp.int32)


def _dispatch(picks, tp, tm):
    n = picks.shape[0]
    tri = jnp.asarray(np.tril(np.ones((tm, tm), np.float32), -1), BF16)
    col = pl.BlockSpec((tm, 1), lambda i: (i, 0))
    r1, r2, cnt = pl.pallas_call(
        _rank_kernel,
        grid=(n // tm,),
        in_specs=[pl.BlockSpec((tm, LANES), lambda i: (i, 0)), pl.BlockSpec((tm, tm), lambda i: (0, 0))],
        out_specs=(col, col, pl.BlockSpec((1, LANES), lambda i: (0, 0))),
        out_shape=(jax.ShapeDtypeStruct((n, 1), jnp.int32), jax.ShapeDtypeStruct((n, 1), jnp.int32),
                   jax.ShapeDtypeStruct((1, LANES), F32)),
        scratch_shapes=[pltpu.VMEM((1, LANES), F32)],
        compiler_params=_params("arbitrary"),
        name="moe_rank",
    )(picks, tri)
    counts = cnt[0, :N_EXPERTS].astype(jnp.int32)
    tiles = (counts + tp - 1) // tp
    tile_end = jnp.cumsum(tiles)
    row_start = jnp.zeros((1, LANES), F32).at[0, :N_EXPERTS].set(((tile_end - tiles) * tp).astype(F32))
    pos1, pos2 = pl.pallas_call(
        _pos_kernel,
        grid=(n // tm,),
        in_specs=[pl.BlockSpec((tm, LANES), lambda i: (i, 0)), col, col, pl.BlockSpec((1, LANES), lambda i: (0, 0))],
        out_specs=(col, col),
        out_shape=(jax.ShapeDtypeStruct((n, 1), jnp.int32), jax.ShapeDtypeStruct((n, 1), jnp.int32)),
        compiler_params=_params("parallel"),
        name="moe_pos",
    )(picks, r1, r2, row_start)
    n_tiles = (2 * n) // tp + N_EXPERTS
    tile_ids = jnp.arange(n_tiles, dtype=jnp.int32)
    tile_expert = jnp.minimum(jnp.sum((tile_end[None, :] <= tile_ids[:, None]).astype(jnp.int32), axis=1),
                              N_EXPERTS - 1)
    return pos1[:, 0], pos2[:, 0], tile_expert, n_tiles * tp


def _pick(n, pref):
    t = min(n, pref)
    while n % t:
        t //= 2
    return t


def kernel(x_prompt, x_sample, cache_k, cache_v, state_hgrn, page_table, rms_in, w_in, att_q_norm, att_k_norm, att_lambda, att_sub_norm, hg_lower_bound, hg_out_norm, w_branch_att, w_branch_hg, w_out, rms_ffn, w_router_group, b_router_group, w_router_expert, b_router_expert, w_exp_gate, w_exp_up, w_exp_down):
    depth = rms_in.shape[0]
    assert depth == 1 and hg_lower_bound.shape[0] == 2
    b, t, _ = x_prompt.shape
    db, ds, _ = x_sample.shape
    assert ds == 1
    lam_init = 0.8 - 0.6 * math.exp(-0.3 * 0)

    w_in_bf = w_in[0].astype(BF16)
    qn_row = jnp.tile(att_q_norm[0], 2).reshape(1, LANES)
    kn_row = jnp.tile(att_k_norm[0], 2).reshape(1, LANES)
    seg_np = (np.arange(LANES)[:, None] // ATT_DHALF) == (np.arange(LANES)[None, :] // ATT_DHALF)
    seg = jnp.asarray(seg_np.astype(np.float32), BF16)
    sub_row = att_sub_norm[0].reshape(1, LANES)
    hgn_row = hg_out_norm[0].reshape(1, LANES)
    wpa = w_branch_att[0].astype(BF16)
    wph = w_branch_hg[0].astype(BF16)
    wo = w_out[0].astype(BF16)
    wr = jnp.zeros((D_MODEL, LANES), F32)
    wr = wr.at[:, :N_EXPERTS].set(w_router_expert[0]).at[:, N_EXPERTS:N_EXPERTS + N_GROUPS].set(w_router_group[0])
    wr = wr.astype(BF16)
    br = jnp.zeros((1, LANES), F32)
    br = br.at[0, :N_EXPERTS].set(b_router_expert[0]).at[0, N_EXPERTS:N_EXPERTS + N_GROUPS].set(b_router_group[0])
    weg = w_exp_gate[0].astype(BF16)
    weu = w_exp_up[0].astype(BF16)
    wed = w_exp_down[0].astype(BF16)
    gin = rms_in[0].reshape(1, D_MODEL)
    gffn = rms_ffn[0].reshape(1, D_MODEL)
    lam_p = att_lambda[0]

    def tail(x2, oa, oh, sga, sgh, tm_merge, tm_moe):
        sparse = x2.shape[0] >= 16 * MOE_TILE
        x1, xn, route = _merge(oa, oh, sga, sgh, x2, wpa, wph, wo, gffn, wr, br, tm_merge, sparse)
        if not sparse:
            return _moe(xn, route, x1, weg, weu, wed, tm_moe)
        pos1, pos2, tile_expert, rows = _dispatch(route, MOE_TILE, _pick(x2.shape[0], 512))
        xs = _moe_scatter(xn, pos1, pos2, rows, _pick(x2.shape[0], 512))
        ys = _moe_ffn(xs, tile_expert, weg, weu, wed, MOE_TILE)
        return _moe_combine(ys, pos1, pos2, route[:, 2:3], route[:, 3:4], x1, _pick(x2.shape[0], 256))

    n = b * t
    xp = x_prompt.reshape(n, D_MODEL)
    q, kt, v, hq, logf, kk, hv, og, sga, sgh = _inproj(
        xp, gin, w_in_bf, qn_row, kn_row, hg_lower_bound, seg, _pick(t, 1024), (b, t))
    oa = _attn_prompt(q.reshape(b, t, 1024), kt, v.reshape(b, t * ATT_KV_HEADS, LANES), lam_p, sub_row, lam_init,
                      _pick(t, 512))
    r3 = lambda a: a.reshape(b, t, 1024)
    oh, state_p = _hgrn_prompt(r3(hq), r3(logf), r3(kk), r3(hv), r3(og), hgn_row, _pick(t, 1024))
    yp = tail(xp, oa.reshape(n, 1024), oh.reshape(n, 1024), sga, sgh, _pick(n, 512), _pick(n, 1024))
    k_prompt = jnp.transpose(kt.reshape(1, b, ATT_KV_HEADS, 2, ATT_DHALF, t), (0, 1, 5, 2, 3, 4))

    xs = x_sample.reshape(db, D_MODEL)
    q, ks, vs, hq, logf, kk, hv, og, sga, sgh = _inproj(
        xs, gin, w_in_bf, qn_row, kn_row, hg_lower_bound, seg, _pick(db, 1024), None)
    q5 = q.reshape(db, ATT_KV_HEADS, 2, 2, ATT_DHALF)
    eye_n = jnp.eye(ATT_KV_HEADS, dtype=BF16)
    eye_c = jnp.eye(2, dtype=BF16)
    wq = jnp.einsum('bngcd,nm,ce->bcngmed', q5, eye_n, eye_c).reshape(db, 16, 512)
    n_pool = cache_k.shape[1]
    page = cache_k.shape[2]
    kt_pages = jnp.transpose(cache_k[0], (0, 2, 3, 4, 1)).reshape(n_pool, 512, page)
    v_pages = cache_v[0].reshape(n_pool, page * ATT_KV_HEADS, ATT_DV)
    oa_s = _attn_sample(wq, ks.reshape(db, 1, 512), vs.reshape(db, 1, 512), kt_pages, v_pages,
                        page_table, lam_p, sub_row, lam_init, _pick(page_table.shape[1], 16))
    row = lambda a: a.astype(F32).reshape(db, HG_HEADS, HG_DK)
    oh_s, state_s = _hgrn_sample(state_hgrn[0], row(hq), row(logf), row(kk), row(hv), row(og), hgn_row)
    ys = tail(xs, oa_s.reshape(db, 1024).astype(BF16), oh_s.reshape(db, 1024).astype(BF16), sga, sgh,
              _pick(db, 512), _pick(db, 1024))

    return (yp.reshape(b, t, D_MODEL), ys.reshape(db, 1, D_MODEL),
            k_prompt, v.reshape(1, b, t, ATT_KV_HEADS, ATT_DV),
            state_p.reshape(1, b, HG_HEADS, HG_DK, HG_DV),
            ks.reshape(1, db, 1, ATT_KV_HEADS, 2, ATT_DHALF), vs.reshape(1, db, 1, ATT_KV_HEADS, ATT_DV),
            state_s.reshape(1, db, HG_HEADS, HG_DK, HG_DV))
```

```python
import functools
import math

import numpy as np
import jax
import jax.numpy as jnp
from jax import lax
from jax.experimental import pallas as pl
from jax.experimental.pallas import tpu as pltpu

F32 = jnp.float32
BF16 = jnp.bfloat16

D_MODEL = 1024
ATT_HEADS = 8
ATT_KV_HEADS = 4
ATT_DHALF = 64
ATT_DV = 128
HG_HEADS = 8
HG_DK = 128
HG_DV = 128
N_GROUPS = 4
EXP_PER_GROUP = 8
N_EXPERTS = 32
D_FF = 512
EPS = 1e-6
LANES = 128
VMEM_LIMIT = 56 * 1024 * 1024

LOG2E = 1.4426950408889634
Q_SCALE = (ATT_DHALF ** -0.5) * LOG2E

MOE_TILE = 512
HG_CHUNK = 64
HG_LEVELS = 6


def _dot(a, b):
    return jnp.dot(a, b, preferred_element_type=F32)


def _dot_nt(a, b):
    return lax.dot_general(a, b, (((1,), (1,)), ((), ())), preferred_element_type=F32)


def _dot_tn(a, b):
    return lax.dot_general(a, b, (((0,), (0,)), ((), ())), preferred_element_type=F32)


def _split3(x):
    hi = x.astype(BF16)
    r1 = x - hi.astype(F32)
    mid = r1.astype(BF16)
    lo = (r1 - mid.astype(F32)).astype(BF16)
    return hi, mid, lo


def _seg_rms(z, seg, gain_row, scale):
    outs = []
    for i in range(z.shape[1] // LANES):
        zi = z[:, i * LANES:(i + 1) * LANES]
        z2 = zi * zi
        hi = z2.astype(BF16)
        lo = (z2 - hi.astype(F32)).astype(BF16)
        ss = _dot(hi, seg) + _dot(lo, seg)
        y = zi * lax.rsqrt(ss * (1.0 / ATT_DHALF) + EPS) * gain_row
        if scale != 1.0:
            y = y * scale
        outs.append(y)
    return jnp.concatenate(outs, axis=1)


def _params(*sem):
    return pltpu.CompilerParams(dimension_semantics=sem, vmem_limit_bytes=VMEM_LIMIT)


def _rms_kernel(x_ref, g_ref, h_ref):
    x = x_ref[...]
    ms = jnp.mean(x * x, axis=-1, keepdims=True)
    h_ref[...] = (x * lax.rsqrt(ms + EPS) * g_ref[...]).astype(BF16)


def _rms_cast(x, g, tm):
    n = x.shape[0]
    return pl.pallas_call(
        _rms_kernel,
        grid=(n // tm,),
        in_specs=[pl.BlockSpec((tm, D_MODEL), lambda i: (i, 0)), pl.BlockSpec((1, D_MODEL), lambda i: (0, 0))],
        out_specs=pl.BlockSpec((tm, D_MODEL), lambda i: (i, 0)),
        out_shape=jax.ShapeDtypeStruct((n, D_MODEL), BF16),
        compiler_params=_params("parallel"),
        name="rms_in",
    )(x, g)


def _proj_kernel(h_ref, w_ref, *refs, mode, n_aux):
    aux, outs = refs[:n_aux], refs[n_aux:]
    z = _dot(h_ref[...], w_ref[...])
    if mode == "q":
        outs[0][...] = _seg_rms(z, aux[0][...], aux[1][...], Q_SCALE).astype(BF16)
    elif mode in ("kv", "kv_t"):
        k = _seg_rms(z[:, :512], aux[0][...], aux[1][...], 1.0)
        if mode == "kv_t":
            outs[0][0] = k.T
            tm = z.shape[0]
            for n in range(ATT_KV_HEADS):
                outs[1][pl.ds(n, tm, stride=ATT_KV_HEADS), :] = z[:, 512 + n * LANES:512 + (n + 1) * LANES]
        else:
            outs[0][...] = k
            outs[1][...] = z[:, 512:]
    elif mode == "hq":
        outs[0][...] = (z * (HG_DK ** -0.5)).astype(BF16)
    elif mode == "hf":
        lbp = aux[0][...]
        m = jnp.max(lbp, axis=0, keepdims=True)
        e = jnp.exp(lbp - m)
        lb = e[0:1] / jnp.sum(e, axis=0, keepdims=True)
        sig = jax.nn.sigmoid(z)
        f = lb + (1.0 - lb) * sig
        outs[0][...] = jnp.log(f)
        outs[1][...] = (1.0 - lb) * (1.0 - sig)
    elif mode == "hv":
        outs[0][...] = z.astype(BF16)
    elif mode == "silu":
        outs[0][...] = z * jax.nn.sigmoid(z)
    elif mode == "sigmoid":
        outs[0][...] = jax.nn.sigmoid(z)


def _proj(h, w, col, mode, aux, outs, tm, name):
    n = h.shape[0]
    aux_specs = [pl.BlockSpec(a.shape, lambda i: (0, 0)) for a in aux]
    res = pl.pallas_call(
        functools.partial(_proj_kernel, mode=mode, n_aux=len(aux)),
        grid=(n // tm,),
        in_specs=[pl.BlockSpec((tm, D_MODEL), lambda i: (i, 0)),
                  pl.BlockSpec((D_MODEL, 1024), lambda i, col=col: (0, col))] + aux_specs,
        out_specs=tuple(pl.BlockSpec(o[2], o[3]) for o in outs),
        out_shape=tuple(jax.ShapeDtypeStruct(o[0], o[1]) for o in outs),
        compiler_params=_params("parallel"),
        name=name,
    )(h, w, *aux)
    return res


def _inproj(x, gin, w, qn_row, kn_row, lbp, seg, tm, seq):
    n = x.shape[0]
    h = _rms_cast(x, gin, tm)
    tok = lambda width, dt: ((n, width), dt, (tm, width), lambda i: (i, 0))
    q, = _proj(h, w, 0, "q", [seg, qn_row], [tok(1024, BF16)], tm, "proj_q")
    if seq is None:
        k, v = _proj(h, w, 1, "kv", [seg, kn_row], [tok(512, F32), tok(512, F32)], tm, "proj_kv")
    else:
        b, t = seq
        tpb = t // tm
        k_out = ((b, 512, t), F32, (1, 512, tm), lambda i: (i // tpb, 0, i % tpb))
        v_out = ((n * ATT_KV_HEADS, LANES), F32, (tm * ATT_KV_HEADS, LANES), lambda i: (i, 0))
        k, v = _proj(h, w, 1, "kv_t", [seg, kn_row], [k_out, v_out], tm, "proj_kv")
    hq, = _proj(h, w, 2, "hq", [], [tok(1024, BF16)], tm, "proj_hq")
    logf, kk = _proj(h, w, 3, "hf", [lbp], [tok(1024, F32), tok(1024, F32)], tm, "proj_hf")
    hv, = _proj(h, w, 4, "hv", [], [tok(1024, BF16)], tm, "proj_hv")
    og, = _proj(h, w, 5, "silu", [], [tok(1024, F32)], tm, "proj_og")
    sga, = _proj(h, w, 6, "sigmoid", [], [tok(1024, F32)], tm, "proj_ga")
    sgh, = _proj(h, w, 7, "sigmoid", [], [tok(1024, F32)], tm, "proj_gh")
    return q, k, v, hq, logf, kk, hv, og, sga, sgh


def _diff_lambda(lp, lam_init):
    a = jnp.sum(lp[0:1] * lp[1:2], axis=-1, keepdims=True)
    b = jnp.sum(lp[2:3] * lp[3:4], axis=-1, keepdims=True)
    return jnp.exp(a) - jnp.exp(b) + lam_init


def _sub_norm(o, sub_row, lam_init):
    ms = jnp.mean(o * o, axis=-1, keepdims=True)
    return o * lax.rsqrt(ms + EPS) * sub_row * (1.0 - lam_init)


def _attn_kernel(q_ref, kt_ref, v_ref, lamp_ref, sub_ref, o_ref,
                 kb_scr, vb_scr, qs_scr, m_scr, acc_scr, *, tq, lam_init):
    qi = pl.program_id(2)

    @pl.when(qi == 0)
    def _():
        kb_scr[...] = kt_ref[0].astype(BF16)
        t_len = vb_scr.shape[0]
        vb_scr[:, 0:LANES] = v_ref[0, pl.ds(pl.program_id(1), t_len, stride=ATT_KV_HEADS), :].astype(BF16)
        vb_scr[:, LANES:2 * LANES] = jnp.ones((vb_scr.shape[0], LANES), BF16)

    q = q_ref[0].astype(F32)
    lane = lax.broadcasted_iota(jnp.int32, (tq, LANES), 1)
    for g in range(2):
        qg = q[:, g * LANES:(g + 1) * LANES]
        for c in range(2):
            keep = (lane < ATT_DHALF) if c == 0 else (lane >= ATT_DHALF)
            r = g * 2 + c
            qs_scr[r * tq:(r + 1) * tq, :] = jnp.where(keep, qg, 0.0).astype(BF16)
    def chunk(j, masked, first):
        off = pl.multiple_of(j * tq, tq)
        s = _dot(qs_scr[...], kb_scr[:, pl.ds(off, tq)])
        if masked:
            row = lax.rem(lax.broadcasted_iota(jnp.int32, s.shape, 0), tq)
            s = jnp.where(lax.broadcasted_iota(jnp.int32, s.shape, 1) <= row, s, -jnp.inf)
        s_max = jnp.max(s, axis=-1, keepdims=True)
        if first:
            m_new = jnp.broadcast_to(s_max, m_scr.shape)
        else:
            m_prev = m_scr[...]
            m_new = jnp.maximum(m_prev, s_max)
            alpha = jnp.exp2(m_prev - m_new)
        pr = jnp.exp2(s - jnp.concatenate([m_new] * (tq // LANES), axis=1))
        pv = _dot(pr.astype(BF16), vb_scr[pl.ds(off, tq), :])
        acc_scr[...] = pv if first else jnp.concatenate([alpha, alpha], axis=1) * acc_scr[...] + pv
        m_scr[...] = m_new

    def body(j, carry):
        chunk(j, False, False)
        return carry

    @pl.when(qi == 0)
    def _():
        chunk(0, True, True)

    @pl.when(qi > 0)
    def _():
        chunk(0, False, True)
        lax.fori_loop(1, qi, body, 0)
        chunk(qi, True, False)

    lam = _diff_lambda(lamp_ref[...], lam_init)
    acc = acc_scr[...]
    out = acc[:, 0:LANES] / acc[:, LANES:2 * LANES]
    for g in range(2):
        o0 = out[(2 * g) * tq:(2 * g + 1) * tq]
        o1 = out[(2 * g + 1) * tq:(2 * g + 2) * tq]
        o_ref[0, :, g * LANES:(g + 1) * LANES] = _sub_norm(o0 - lam * o1, sub_ref[...], lam_init).astype(BF16)


def _attn_prompt(q, kt, v, lam_p, sub_row, lam_init, tq):
    b, t, _ = q.shape
    return pl.pallas_call(
        functools.partial(_attn_kernel, tq=tq, lam_init=lam_init),
        grid=(b, ATT_KV_HEADS, t // tq),
        in_specs=[
            pl.BlockSpec((1, tq, 256), lambda bi, n, qi: (bi, qi, n)),
            pl.BlockSpec((1, LANES, t), lambda bi, n, qi: (bi, n, 0)),
            pl.BlockSpec((1, t * ATT_KV_HEADS, LANES), lambda bi, n, qi: (bi, 0, 0)),
            pl.BlockSpec((4, ATT_DHALF), lambda bi, n, qi: (0, 0)),
            pl.BlockSpec((1, LANES), lambda bi, n, qi: (0, 0)),
        ],
        out_specs=pl.BlockSpec((1, tq, 256), lambda bi, n, qi: (bi, qi, n)),
        out_shape=jax.ShapeDtypeStruct((b, t, 1024), BF16),
        scratch_shapes=[pltpu.VMEM((LANES, t), BF16), pltpu.VMEM((t, 2 * LANES), BF16),
                        pltpu.VMEM((4 * tq, LANES), BF16), pltpu.VMEM((4 * tq, LANES), F32),
                        pltpu.VMEM((4 * tq, 2 * LANES), F32)],
        compiler_params=_params("parallel", "parallel", "arbitrary"),
        name="attn_prompt",
    )(q, kt, v, lam_p, sub_row)


def _hgrn_tables(c, levels):
    t = np.arange(c)[:, None]
    u = np.arange(c)[None, :]
    w = [(u <= t), (u > t)]
    masks = [(t == u)]
    for l in range(1, levels + 1):
        n = 1 << l
        half = n // 2
        mid = (t // n) * n + half
        hi = t >= mid
        w.append(np.where(hi, (u > mid) & (u <= t), (u > t) & (u <= mid)))
        masks.append((t // n == u // n) & (t % n >= half) & (u % n < half))
    w = np.concatenate(w, axis=0).astype(np.float32)
    w3 = np.concatenate([w, w, w], axis=1)
    if len(masks) % 2:
        masks.append(np.zeros_like(masks[0]))
    pairs = [np.concatenate([masks[i], masks[i + 1]], axis=1) for i in range(0, len(masks), 2)]
    return w3, np.stack(pairs).astype(np.float32)


def _hgrn_kernel(q_ref, g_ref, k_ref, v_ref, og_ref, norm_ref, w_ref, mask_ref, o_ref, s_ref,
                 st_scr, *, c, levels, n_chunks):
    ti = pl.program_id(2)

    @pl.when(ti == 0)
    def _():
        st_scr[...] = jnp.zeros(st_scr.shape, F32)

    w = w_ref[...]
    zeros = jnp.zeros((c, LANES), BF16)
    chunk_rows = [slice(ci * c, (ci + 1) * c) for ci in range(n_chunks)]
    g3 = jnp.concatenate([jnp.concatenate(_split3(g_ref[0, rows, :]), axis=0) for rows in chunk_rows], axis=1)
    e_all = jnp.exp(_dot(w, g3))

    parts = []
    for ci, rows in enumerate(chunk_rows):
        q = q_ref[0, rows, :].astype(F32)
        k = k_ref[0, rows, :]
        v = v_ref[0, rows, :]
        e = e_all[:, ci * LANES:(ci + 1) * LANES]
        e_b = e[0:c]
        e_k = e[c:2 * c]
        qs = [q.astype(BF16)]
        ks = [k.astype(BF16)]
        for l in range(1, levels + 1):
            e_l = e[(l + 1) * c:(l + 2) * c]
            qs.append((q * e_l).astype(BF16))
            ks.append((k * e_l).astype(BF16))
        if len(qs) % 2:
            qs.append(zeros)
            ks.append(zeros)
        a2 = None
        for p in range(len(qs) // 2):
            ql = jnp.concatenate([qs[2 * p], qs[2 * p + 1]], axis=1)
            kbd = jnp.concatenate([jnp.concatenate([ks[2 * p], zeros], axis=1),
                                   jnp.concatenate([zeros, ks[2 * p + 1]], axis=1)], axis=0)
            term = mask_ref[p] * _dot_nt(ql, kbd)
            a2 = term if a2 is None else a2 + term
        o_intra = _dot(a2.astype(BF16), jnp.concatenate([v, v], axis=0))
        st_add = _dot_tn(v, (k * e_k).astype(BF16))
        parts.append(((q * e_b).astype(BF16), e_b[c - 1:c, :], o_intra, st_add))

    st = st_scr[...]
    for rows, (q_b, decay, o_intra, st_add) in zip(chunk_rows, parts):
        o = _dot_nt(q_b, st.astype(BF16)) + o_intra
        st = st * decay + st_add
        ms = jnp.mean(o * o, axis=-1, keepdims=True)
        y = o * lax.rsqrt(ms + EPS) * norm_ref[...] * og_ref[0, rows, :]
        o_ref[0, rows, :] = y.astype(BF16)
    st_scr[...] = st

    @pl.when(ti == pl.num_programs(2) - 1)
    def _():
        s_ref[0, 0] = st.T


def _hgrn_prompt(hq, logf, kk, hv, og, norm_row, ct):
    b, t, _ = hq.shape
    c, levels = HG_CHUNK, HG_LEVELS
    w_np, m_np = _hgrn_tables(c, levels)
    w = jnp.asarray(w_np, BF16)
    masks = jnp.asarray(m_np, F32)
    tok = pl.BlockSpec((1, ct, LANES), lambda bi, h, ti: (bi, ti, h))
    return pl.pallas_call(
        functools.partial(_hgrn_kernel, c=c, levels=levels, n_chunks=ct // c),
        grid=(b, HG_HEADS, t // ct),
        in_specs=[tok, tok, tok, tok, tok,
                  pl.BlockSpec((1, LANES), lambda bi, h, ti: (0, 0)),
                  pl.BlockSpec(w.shape, lambda bi, h, ti: (0, 0)),
                  pl.BlockSpec(masks.shape, lambda bi, h, ti: (0, 0, 0))],
        out_specs=(tok, pl.BlockSpec((1, 1, HG_DK, HG_DV), lambda bi, h, ti: (bi, h, 0, 0))),
        out_shape=(jax.ShapeDtypeStruct((b, t, 1024), BF16),
                   jax.ShapeDtypeStruct((b, HG_HEADS, HG_DK, HG_DV), F32)),
        scratch_shapes=[pltpu.VMEM((HG_DV, HG_DK), F32)],
        compiler_params=_params("parallel", "parallel", "arbitrary"),
        name="hgrn_prompt",
    )(hq, logf, kk, hv, og, norm_row, w, masks)


def _dec_attn_kernel(pt_ref, wq_ref, kn_ref, vn_ref, lamp_ref, sub_ref, *refs, pages, lam_init):
    k_refs = refs[:pages]
    v_refs = refs[pages:2 * pages]
    o_ref = refs[2 * pages]
    m_scr, l_scr, acc_scr = refs[2 * pages + 1:]
    j = pl.program_id(1)
    page = k_refs[0].shape[2]

    @pl.when(j == 0)
    def _():
        m_scr[...] = jnp.full(m_scr.shape, -jnp.inf, F32)
        l_scr[...] = jnp.zeros(l_scr.shape, F32)
        acc_scr[...] = jnp.zeros(acc_scr.shape, F32)

    rown = lax.rem(lax.broadcasted_iota(jnp.int32, (16, LANES), 0), 8) // 2
    wq = wq_ref[0]
    s = jnp.concatenate([_dot(wq, k_refs[i][0].astype(BF16)) for i in range(pages)], axis=1)
    m_prev = m_scr[...]
    m_new = jnp.maximum(m_prev, jnp.max(s, axis=-1, keepdims=True))
    alpha = jnp.exp2(m_prev - m_new)
    pr = jnp.exp2(s - m_new)
    l_scr[...] = alpha * l_scr[...] + jnp.sum(pr, axis=-1, keepdims=True)
    prb = pr.astype(BF16)
    pv = jnp.zeros((16, LANES), F32)
    for n in range(ATT_KV_HEADS):
        v_n = jnp.concatenate([v_refs[i][0, pl.ds(n, page, stride=ATT_KV_HEADS), :] for i in range(pages)],
                              axis=0).astype(BF16)
        pv = pv + jnp.where(rown == n, _dot(prb, v_n), 0.0)
    acc_scr[...] = alpha * acc_scr[...] + pv
    m_scr[...] = m_new

    @pl.when(j == pl.num_programs(1) - 1)
    def _():
        s_cur = jnp.sum(wq.astype(F32) * kn_ref[0], axis=-1, keepdims=True)
        m_old = m_scr[...]
        m_fin = jnp.maximum(m_old, s_cur)
        al = jnp.exp2(m_old - m_fin)
        pc = jnp.exp2(s_cur - m_fin)
        l_fin = al * l_scr[...] + pc
        vn = vn_ref[0]
        v_cur = jnp.zeros((16, LANES), F32)
        for n in range(ATT_KV_HEADS):
            v_cur = v_cur + jnp.where(rown == n, vn[:, n * LANES:(n + 1) * LANES], 0.0)
        out = (al * acc_scr[...] + pc * v_cur) / l_fin
        lam = _diff_lambda(lamp_ref[...], lam_init)
        o = out[0:8] - lam * out[8:16]
        o_ref[0] = _sub_norm(o, sub_ref[...], lam_init)


def _attn_sample(wq, k_new, v_new, kt_pages, v_pages, page_table, lam_p, sub_row, lam_init, pages):
    db, n_pages = page_table.shape
    pt_flat = page_table.reshape(-1)
    blk = kt_pages.shape[1:]

    def page_spec(i):
        return pl.BlockSpec((1,) + blk, lambda bi, j, pt, i=i: (pt[bi * n_pages + j * pages + i], 0, 0))

    row3 = lambda w: pl.BlockSpec((1, 1, w), lambda bi, j, pt: (bi, 0, 0))
    grid_spec = pltpu.PrefetchScalarGridSpec(
        num_scalar_prefetch=1,
        grid=(db, n_pages // pages),
        in_specs=[pl.BlockSpec((1, 16, 512), lambda bi, j, pt: (bi, 0, 0)), row3(512), row3(512),
                  pl.BlockSpec((4, ATT_DHALF), lambda bi, j, pt: (0, 0)),
                  pl.BlockSpec((1, LANES), lambda bi, j, pt: (0, 0))]
                 + [page_spec(i) for i in range(pages)] + [page_spec(i) for i in range(pages)],
        out_specs=pl.BlockSpec((1, 8, LANES), lambda bi, j, pt: (bi, 0, 0)),
        scratch_shapes=[pltpu.VMEM((16, 1), F32), pltpu.VMEM((16, 1), F32), pltpu.VMEM((16, LANES), F32)],
    )
    return pl.pallas_call(
        functools.partial(_dec_attn_kernel, pages=pages, lam_init=lam_init),
        grid_spec=grid_spec,
        out_shape=jax.ShapeDtypeStruct((db, 8, LANES), F32),
        compiler_params=_params("parallel", "arbitrary"),
        name="attn_sample",
    )(pt_flat, wq, k_new, v_new, lam_p, sub_row, *([kt_pages] * pages), *([v_pages] * pages))


def _hgrn_step_kernel(s_ref, q_ref, g_ref, k_ref, v_ref, og_ref, norm_ref, o_ref, sn_ref):
    pad = jnp.zeros((LANES - 3 * HG_HEADS, LANES), F32)
    cols = jnp.concatenate([q_ref[0], g_ref[0], k_ref[0], pad], axis=0).T
    v_all = v_ref[0]
    og_all = og_ref[0]
    outs = []
    for h in range(HG_HEADS):
        s = s_ref[0, h]
        qc = cols[:, h:h + 1]
        f = jnp.exp(cols[:, HG_HEADS + h:HG_HEADS + h + 1])
        kc = cols[:, 2 * HG_HEADS + h:2 * HG_HEADS + h + 1]
        v = v_all[h:h + 1]
        o = jnp.sum(s * (qc * f), axis=0, keepdims=True) + jnp.sum(qc * kc, axis=0, keepdims=True) * v
        sn_ref[0, h] = s * f + kc * v
        ms = jnp.mean(o * o, axis=-1, keepdims=True)
        outs.append(o * lax.rsqrt(ms + EPS) * norm_ref[...] * og_all[h:h + 1])
    o_ref[0] = jnp.concatenate(outs, axis=0)


def _hgrn_sample(state, q, g, k, v, og, norm_row):
    db = state.shape[0]
    row = pl.BlockSpec((1, HG_HEADS, HG_DK), lambda bi: (bi, 0, 0))
    st = pl.BlockSpec((1, HG_HEADS, HG_DK, HG_DV), lambda bi: (bi, 0, 0, 0))
    return pl.pallas_call(
        _hgrn_step_kernel,
        grid=(db,),
        in_specs=[st, row, row, row, row, row, pl.BlockSpec((1, LANES), lambda bi: (0, 0))],
        out_specs=(row, st),
        out_shape=(jax.ShapeDtypeStruct((db, HG_HEADS, HG_DV), F32),
                   jax.ShapeDtypeStruct(state.shape, F32)),
        compiler_params=_params("parallel"),
        name="hgrn_sample",
    )(state, q, g, k, v, og, norm_row)


def _lane_first(cond, lane):
    return jnp.min(jnp.where(cond, lane, LANES), axis=-1, keepdims=True)


def _route(logits):
    lane = lax.broadcasted_iota(jnp.int32, logits.shape, 1)
    gmask = (lane >= N_EXPERTS) & (lane < N_EXPERTS + N_GROUPS)
    lg = jnp.where(gmask, logits, -jnp.inf)
    mg = jnp.max(lg, axis=-1, keepdims=True)
    eg = jnp.exp(lg - mg)
    pg = eg / jnp.sum(eg, axis=-1, keepdims=True)
    pg_top = jnp.max(pg, axis=-1, keepdims=True)
    gsel = _lane_first(gmask & (pg == pg_top), lane) - N_EXPERTS
    emask = (lane >= gsel * EXP_PER_GROUP) & (lane < (gsel + 1) * EXP_PER_GROUP)
    le = jnp.where(emask, logits, -jnp.inf)
    me = jnp.max(le, axis=-1, keepdims=True)
    ee = jnp.exp(le - me)
    pe = ee / jnp.sum(ee, axis=-1, keepdims=True)
    pe = jnp.where(emask, pe, -1.0)
    p1 = jnp.max(pe, axis=-1, keepdims=True)
    e1 = _lane_first(pe == p1, lane)
    pe2 = jnp.where(lane == e1, -1.0, pe)
    p2 = jnp.max(pe2, axis=-1, keepdims=True)
    e2 = _lane_first(pe2 == p2, lane)
    den = p1 + p2
    w1 = p1 / den * pg_top
    w2 = p2 / den * pg_top
    comb = jnp.where(lane == e1, w1, 0.0) + jnp.where(lane == e2, w2, 0.0)
    picks = jnp.where(lane == 0, e1.astype(F32), jnp.where(lane == 1, e2.astype(F32),
                      jnp.where(lane == 2, w1, jnp.where(lane == 3, w2, 0.0))))
    return comb, picks


ROW_TILES = D_MODEL // LANES


def _merge_kernel(oa_ref, oh_ref, sga_ref, sgh_ref, x_ref, wpa_ref, wph_ref, wo_ref, gf_ref,
                  wr_ref, br_ref, x1_ref, xn_ref, route_ref, *, sparse):
    ya = _dot(oa_ref[...], wpa_ref[...])
    yh = _dot(oh_ref[...], wph_ref[...])
    mixed = (sga_ref[...] * ya + sgh_ref[...] * yh).astype(BF16)
    x1 = x_ref[...] + _dot(mixed, wo_ref[...])
    x1_ref[...] = x1
    ms = jnp.mean(x1 * x1, axis=-1, keepdims=True)
    xn_f = x1 * lax.rsqrt(ms + EPS) * gf_ref[...]
    xn = xn_f.astype(BF16)
    comb, picks = _route(_dot(xn, wr_ref[...]) + br_ref[...])
    if sparse:
        tm = x1.shape[0]
        for s in range(ROW_TILES):
            xn_ref[pl.ds(s, tm, stride=ROW_TILES), :] = xn_f[:, s * LANES:(s + 1) * LANES]
        route_ref[...] = picks
    else:
        xn_ref[...] = xn
        route_ref[...] = comb


def _merge(oa, oh, sga, sgh, x, wpa, wph, wo, gf, wr, br, tm, sparse):
    n = x.shape[0]
    tok = pl.BlockSpec((tm, 1024), lambda i: (i, 0))
    wsp = pl.BlockSpec((1024, 1024), lambda i: (0, 0))
    if sparse:
        xn_spec = pl.BlockSpec((tm * ROW_TILES, LANES), lambda i: (i, 0))
        xn_shape = jax.ShapeDtypeStruct((n * ROW_TILES, LANES), F32)
    else:
        xn_spec, xn_shape = tok, jax.ShapeDtypeStruct((n, 1024), BF16)
    return pl.pallas_call(
        functools.partial(_merge_kernel, sparse=sparse),
        grid=(n // tm,),
        in_specs=[tok, tok, tok, tok, tok, wsp, wsp, wsp,
                  pl.BlockSpec((1, 1024), lambda i: (0, 0)),
                  pl.BlockSpec((1024, LANES), lambda i: (0, 0)),
                  pl.BlockSpec((1, LANES), lambda i: (0, 0))],
        out_specs=(tok, xn_spec, pl.BlockSpec((tm, LANES), lambda i: (i, 0))),
        out_shape=(jax.ShapeDtypeStruct((n, 1024), F32), xn_shape,
                   jax.ShapeDtypeStruct((n, LANES), F32)),
        compiler_params=_params("parallel"),
        name="merge",
    )(oa, oh, sga, sgh, x, wpa, wph, wo, gf, wr, br)


def _moe_kernel(xn_ref, comb_ref, x1_ref, wg_ref, wu_ref, wd_ref, y_ref, acc_scr):
    e = pl.program_id(1)

    @pl.when(e == 0)
    def _():
        acc_scr[...] = jnp.zeros(acc_scr.shape, F32)

    comb = comb_ref[...]
    lane = lax.broadcasted_iota(jnp.int32, comb.shape, 1)
    c = jnp.sum(jnp.where(lane == e, comb, 0.0), axis=-1, keepdims=True)
    xn = xn_ref[...]
    hg = _dot(xn, wg_ref[0])
    hu = _dot(xn, wu_ref[0])
    hid = hg * jax.nn.sigmoid(hg) * hu
    acc_scr[...] += _dot((hid * c).astype(BF16), wd_ref[0])

    @pl.when(e == pl.num_programs(1) - 1)
    def _():
        y_ref[...] = x1_ref[...] + acc_scr[...]


def _moe(xn, comb, x1, wg, wu, wd, tm):
    n = xn.shape[0]
    return pl.pallas_call(
        _moe_kernel,
        grid=(n // tm, N_EXPERTS),
        in_specs=[pl.BlockSpec((tm, 1024), lambda i, e: (i, 0)),
                  pl.BlockSpec((tm, LANES), lambda i, e: (i, 0)),
                  pl.BlockSpec((tm, 1024), lambda i, e: (i, 0)),
                  pl.BlockSpec((1, 1024, D_FF), lambda i, e: (e, 0, 0)),
                  pl.BlockSpec((1, 1024, D_FF), lambda i, e: (e, 0, 0)),
                  pl.BlockSpec((1, D_FF, 1024), lambda i, e: (e, 0, 0))],
        out_specs=pl.BlockSpec((tm, 1024), lambda i, e: (i, 0)),
        out_shape=jax.ShapeDtypeStruct((n, 1024), F32),
        scratch_shapes=[pltpu.VMEM((tm, 1024), F32)],
        compiler_params=_params("parallel", "arbitrary"),
        name="moe",
    )(xn, comb, x1, wg, wu, wd)


def _row_copy(src_hbm, src_row, dst, dst_row, sem):
    return pltpu.make_async_copy(src_hbm.at[pl.ds(src_row * ROW_TILES, ROW_TILES)],
                                 dst.at[pl.ds(dst_row * ROW_TILES, ROW_TILES)], sem)


def _wait_rows(src_hbm, dst, sem):
    pltpu.make_async_copy(src_hbm.at[pl.ds(0, dst.shape[0])], dst, sem).wait()


def _rows_to_mat(buf, n_rows):
    return jnp.concatenate([buf[pl.ds(s, n_rows, stride=ROW_TILES), :] for s in range(ROW_TILES)], axis=1)


def _moe_scatter_kernel(p1_ref, p2_ref, xn_ref, xs_in, xs_hbm, sem):
    del xs_in
    i = pl.program_id(0)
    tm = p1_ref.shape[0]

    def wait_tile():
        for _ in range(2):
            pltpu.make_async_copy(xn_ref, xs_hbm.at[pl.ds(0, tm * ROW_TILES)], sem.at[0]).wait()

    def body(r, carry):
        src = xn_ref.at[pl.ds(r * ROW_TILES, ROW_TILES)]
        pltpu.make_async_copy(src, xs_hbm.at[pl.ds(p1_ref[r] * ROW_TILES, ROW_TILES)], sem.at[0]).start(priority=0)
        pltpu.make_async_copy(src, xs_hbm.at[pl.ds(p2_ref[r] * ROW_TILES, ROW_TILES)], sem.at[0]).start(priority=1)
        return carry

    lax.fori_loop(0, tm, body, 0)
    wait_tile()


def _moe_scatter(xg, pos1, pos2, rows, tm):
    n = pos1.shape[0]
    idx = pl.BlockSpec((tm,), lambda i: (i,), memory_space=pltpu.SMEM)
    xs0 = jnp.zeros((rows * ROW_TILES, LANES), F32)
    return pl.pallas_call(
        _moe_scatter_kernel,
        grid=(n // tm,),
        in_specs=[idx, idx, pl.BlockSpec((tm * ROW_TILES, LANES), lambda i: (i, 0)),
                  pl.BlockSpec(memory_space=pl.ANY)],
        out_specs=pl.BlockSpec(memory_space=pl.ANY),
        out_shape=jax.ShapeDtypeStruct(xs0.shape, F32),
        scratch_shapes=[pltpu.SemaphoreType.DMA((1,))],
        input_output_aliases={3: 0},
        compiler_params=_params("arbitrary"),
        name="moe_scatter",
    )(pos1, pos2, xg, xs0)


def _moe_ffn_kernel(te_ref, xs_ref, wg_ref, wu_ref, wd_ref, ys_ref, *, tp):
    del te_ref
    x = _rows_to_mat(xs_ref, tp).astype(BF16)
    hg = _dot(x, wg_ref[0])
    hu = _dot(x, wu_ref[0])
    hid = (hg * jax.nn.sigmoid(hg) * hu).astype(BF16)
    out = _dot(hid, wd_ref[0])
    for s in range(ROW_TILES):
        ys_ref[pl.ds(s, tp, stride=ROW_TILES), :] = out[:, s * LANES:(s + 1) * LANES]


def _moe_ffn(xs, tile_expert, wg, wu, wd, tp):
    n_tiles = tile_expert.shape[0]
    rows = pl.BlockSpec((tp * ROW_TILES, LANES), lambda i, te: (i, 0))
    wsp = lambda a, b: pl.BlockSpec((1, a, b), lambda i, te: (te[i], 0, 0))
    grid_spec = pltpu.PrefetchScalarGridSpec(
        num_scalar_prefetch=1,
        grid=(n_tiles,),
        in_specs=[rows, wsp(1024, D_FF), wsp(1024, D_FF), wsp(D_FF, 1024)],
        out_specs=rows,
    )
    return pl.pallas_call(
        functools.partial(_moe_ffn_kernel, tp=tp),
        grid_spec=grid_spec,
        out_shape=jax.ShapeDtypeStruct(xs.shape, F32),
        compiler_params=_params("parallel"),
        name="moe_ffn",
    )(tile_expert, xs, wg, wu, wd)


def _moe_combine_kernel(p1_ref, p1n_ref, p2_ref, p2n_ref, ys_hbm, x1_ref, w1_ref, w2_ref, y_ref, buf, sem, *, tm):
    i = pl.program_id(0)
    slot = lax.rem(i, 2)

    def gather(a_ref, b_ref, s):
        def body(r, carry):
            _row_copy(ys_hbm, a_ref[r], buf.at[s, 0], r, sem.at[s, 0]).start(priority=0)
            _row_copy(ys_hbm, b_ref[r], buf.at[s, 1], r, sem.at[s, 1]).start(priority=1)
            return carry
        lax.fori_loop(0, tm, body, 0)

    @pl.when(i == 0)
    def _():
        gather(p1_ref, p2_ref, 0)

    @pl.when(i + 1 < pl.num_programs(0))
    def _():
        gather(p1n_ref, p2n_ref, 1 - slot)

    _wait_rows(ys_hbm, buf.at[slot, 0], sem.at[slot, 0])
    _wait_rows(ys_hbm, buf.at[slot, 1], sem.at[slot, 1])
    y_ref[...] = x1_ref[...] + (w1_ref[...] * _rows_to_mat(buf.at[slot, 0], tm)
                                + w2_ref[...] * _rows_to_mat(buf.at[slot, 1], tm))


def _moe_combine(ys, pos1, pos2, w1, w2, x1, tm):
    n = x1.shape[0]
    n_tiles = n // tm
    idx = lambda off: pl.BlockSpec((tm,), lambda i: (jnp.minimum(i + off, n_tiles - 1),),
                                   memory_space=pltpu.SMEM)
    col = pl.BlockSpec((tm, 1), lambda i: (i, 0))
    return pl.pallas_call(
        functools.partial(_moe_combine_kernel, tm=tm),
        grid=(n_tiles,),
        in_specs=[idx(0), idx(1), idx(0), idx(1), pl.BlockSpec(memory_space=pl.ANY),
                  pl.BlockSpec((tm, 1024), lambda i: (i, 0)), col, col],
        out_specs=pl.BlockSpec((tm, 1024), lambda i: (i, 0)),
        out_shape=jax.ShapeDtypeStruct((n, 1024), F32),
        scratch_shapes=[pltpu.VMEM((2, 2, tm * ROW_TILES, LANES), F32), pltpu.SemaphoreType.DMA((2, 2))],
        compiler_params=_params("arbitrary"),
        name="moe_combine",
    )(pos1, pos1, pos2, pos2, ys, x1, w1, w2)


def _rank_kernel(picks_ref, tri_ref, rank_ref, cnt_ref, base_scr):
    @pl.when(pl.program_id(0) == 0)
    def _():
        base_scr[...] = jnp.zeros(base_scr.shape, F32)

    picks = picks_ref[...]
    lane = lax.broadcasted_iota(jnp.int32, picks.shape, 1)
    chosen = (lane == picks[:, 0:1].astype(jnp.int32)) | (lane == picks[:, 1:2].astype(jnp.int32))
    both = jnp.where(chosen, 1.0, 0.0)
    before = _dot(tri_ref[...], both.astype(BF16)) + base_scr[...]
    rank_ref[...] = jnp.where(chosen, before, 0.0)
    base_scr[...] += jnp.sum(both, axis=0, keepdims=True)
    cnt_ref[...] = base_scr[...]


def _pos_kernel(picks_ref, rank_ref, start_ref, pos_ref):
    picks = picks_ref[...]
    lane = lax.broadcasted_iota(jnp.int32, picks.shape, 1)
    where_to = rank_ref[...] + start_ref[...]
    pos = [jnp.sum(jnp.where(lane == picks[:, p:p + 1].astype(jnp.int32), where_to, 0.0), axis=-1, keepdims=True)
           for p in range(2)]
    pos_ref[...] = jnp.where(lane == 0, pos[0], jnp.where(lane == 1, pos[1], 0.0))


def _dispatch(picks, tp, tm):
    n = picks.shape[0]
    tri = jnp.asarray(np.tril(np.ones((tm, tm), np.float32), -1), BF16)
    tok = pl.BlockSpec((tm, LANES), lambda i: (i, 0))
    one = pl.BlockSpec((1, LANES), lambda i: (0, 0))
    rank, cnt = pl.pallas_call(
        _rank_kernel,
        grid=(n // tm,),
        in_specs=[tok, pl.BlockSpec((tm, tm), lambda i: (0, 0))],
        out_specs=(tok, one),
        out_shape=(jax.ShapeDtypeStruct((n, LANES), F32), jax.ShapeDtypeStruct((1, LANES), F32)),
        scratch_shapes=[pltpu.VMEM((1, LANES), F32)],
        compiler_params=_params("arbitrary"),
        name="moe_rank",
    )(picks, tri)
    counts = cnt[0, :N_EXPERTS].astype(jnp.int32)
    tiles = (counts + tp - 1) // tp
    tile_end = jnp.cumsum(tiles)
    row_start = jnp.zeros((1, LANES), F32).at[0, :N_EXPERTS].set(((tile_end - tiles) * tp).astype(F32))
    pos = pl.pallas_call(
        _pos_kernel,
        grid=(n // tm,),
        in_specs=[tok, tok, one],
        out_specs=tok,
        out_shape=jax.ShapeDtypeStruct((n, LANES), F32),
        compiler_params=_params("parallel"),
        name="moe_pos",
    )(picks, rank, row_start)
    n_tiles = (2 * n) // tp + N_EXPERTS
    tile_ids = jnp.arange(n_tiles, dtype=jnp.int32)
    tile_expert = jnp.minimum(jnp.sum((tile_end[None, :] <= tile_ids[:, None]).astype(jnp.int32), axis=1),
                              N_EXPERTS - 1)
    return pos[:, 0].astype(jnp.int32), pos[:, 1].astype(jnp.int32), tile_expert, n_tiles * tp


def _pick(n, pref):
    t = min(n, pref)
    while n % t:
        t //= 2
    return t


def kernel(x_prompt, x_sample, cache_k, cache_v, state_hgrn, page_table, rms_in, w_in, att_q_norm, att_k_norm, att_lambda, att_sub_norm, hg_lower_bound, hg_out_norm, w_branch_att, w_branch_hg, w_out, rms_ffn, w_router_group, b_router_group, w_router_expert, b_router_expert, w_exp_gate, w_exp_up, w_exp_down):
    depth = rms_in.shape[0]
    assert depth == 1 and hg_lower_bound.shape[0] == 2
    b, t, _ = x_prompt.shape
    db, ds, _ = x_sample.shape
    assert ds == 1
    lam_init = 0.8 - 0.6 * math.exp(-0.3 * 0)

    w_in_bf = w_in[0].astype(BF16)
    qn_row = jnp.tile(att_q_norm[0], 2).reshape(1, LANES)
    kn_row = jnp.tile(att_k_norm[0], 2).reshape(1, LANES)
    seg_np = (np.arange(LANES)[:, None] // ATT_DHALF) == (np.arange(LANES)[None, :] // ATT_DHALF)
    seg = jnp.asarray(seg_np.astype(np.float32), BF16)
    sub_row = att_sub_norm[0].reshape(1, LANES)
    hgn_row = hg_out_norm[0].reshape(1, LANES)
    wpa = w_branch_att[0].astype(BF16)
    wph = w_branch_hg[0].astype(BF16)
    wo = w_out[0].astype(BF16)
    wr = jnp.zeros((D_MODEL, LANES), F32)
    wr = wr.at[:, :N_EXPERTS].set(w_router_expert[0]).at[:, N_EXPERTS:N_EXPERTS + N_GROUPS].set(w_router_group[0])
    wr = wr.astype(BF16)
    br = jnp.zeros((1, LANES), F32)
    br = br.at[0, :N_EXPERTS].set(b_router_expert[0]).at[0, N_EXPERTS:N_EXPERTS + N_GROUPS].set(b_router_group[0])
    weg = w_exp_gate[0].astype(BF16)
    weu = w_exp_up[0].astype(BF16)
    wed = w_exp_down[0].astype(BF16)
    gin = rms_in[0].reshape(1, D_MODEL)
    gffn = rms_ffn[0].reshape(1, D_MODEL)
    lam_p = att_lambda[0]

    def tail(x2, oa, oh, sga, sgh, tm_merge, tm_moe):
        sparse = x2.shape[0] >= 16 * MOE_TILE
        x1, xn, route = _merge(oa, oh, sga, sgh, x2, wpa, wph, wo, gffn, wr, br, tm_merge, sparse)
        if not sparse:
            return _moe(xn, route, x1, weg, weu, wed, tm_moe)
        pos1, pos2, tile_expert, rows = _dispatch(route, MOE_TILE, _pick(x2.shape[0], 512))
        xs = _moe_scatter(xn, pos1, pos2, rows, _pick(x2.shape[0], 512))
        ys = _moe_ffn(xs, tile_expert, weg, weu, wed, MOE_TILE)
        return _moe_combine(ys, pos1, pos2, route[:, 2:3], route[:, 3:4], x1, _pick(x2.shape[0], 256))

    n = b * t
    xp = x_prompt.reshape(n, D_MODEL)
    q, kt, v, hq, logf, kk, hv, og, sga, sgh = _inproj(
        xp, gin, w_in_bf, qn_row, kn_row, hg_lower_bound, seg, _pick(t, 1024), (b, t))
    oa = _attn_prompt(q.reshape(b, t, 1024), kt, v.reshape(b, t * ATT_KV_HEADS, LANES), lam_p, sub_row, lam_init,
                      _pick(t, 512))
    r3 = lambda a: a.reshape(b, t, 1024)
    oh, state_p = _hgrn_prompt(r3(hq), r3(logf), r3(kk), r3(hv), r3(og), hgn_row, _pick(t, 1024))
    yp = tail(xp, oa.reshape(n, 1024), oh.reshape(n, 1024), sga, sgh, _pick(n, 512), _pick(n, 1024))
    k_prompt = jnp.transpose(kt.reshape(1, b, ATT_KV_HEADS, 2, ATT_DHALF, t), (0, 1, 5, 2, 3, 4))

    xs = x_sample.reshape(db, D_MODEL)
    q, ks, vs, hq, logf, kk, hv, og, sga, sgh = _inproj(
        xs, gin, w_in_bf, qn_row, kn_row, hg_lower_bound, seg, _pick(db, 1024), None)
    q5 = q.reshape(db, ATT_KV_HEADS, 2, 2, ATT_DHALF)
    eye_n = jnp.eye(ATT_KV_HEADS, dtype=BF16)
    eye_c = jnp.eye(2, dtype=BF16)
    wq = jnp.einsum('bngcd,nm,ce->bcngmed', q5, eye_n, eye_c).reshape(db, 16, 512)
    n_pool = cache_k.shape[1]
    page = cache_k.shape[2]
    kt_pages = jnp.transpose(cache_k[0], (0, 2, 3, 4, 1)).reshape(n_pool, 512, page)
    v_pages = cache_v[0].reshape(n_pool, page * ATT_KV_HEADS, ATT_DV)
    oa_s = _attn_sample(wq, ks.reshape(db, 1, 512), vs.reshape(db, 1, 512), kt_pages, v_pages,
                        page_table, lam_p, sub_row, lam_init, _pick(page_table.shape[1], 16))
    row = lambda a: a.astype(F32).reshape(db, HG_HEADS, HG_DK)
    oh_s, state_s = _hgrn_sample(state_hgrn[0], row(hq), row(logf), row(kk), row(hv), row(og), hgn_row)
    ys = tail(xs, oa_s.reshape(db, 1024).astype(BF16), oh_s.reshape(db, 1024).astype(BF16), sga, sgh,
              _pick(db, 512), _pick(db, 1024))

    return (yp.reshape(b, t, D_MODEL), ys.reshape(db, 1, D_MODEL),
            k_prompt, v.reshape(1, b, t, ATT_KV_HEADS, ATT_DV),
            state_p.reshape(1, b, HG_HEADS, HG_DK, HG_DV),
            ks.reshape(1, db, 1, ATT_KV_HEADS, 2, ATT_DHALF), vs.reshape(1, db, 1, ATT_KV_HEADS, ATT_DV),
            state_s.reshape(1, db, HG_HEADS, HG_DK, HG_DV))
```

```python
import functools
import math

import numpy as np
import jax
import jax.numpy as jnp
from jax import lax
from jax.experimental import pallas as pl
from jax.experimental.pallas import tpu as pltpu

F32 = jnp.float32
BF16 = jnp.bfloat16

D_MODEL = 1024
ATT_HEADS = 8
ATT_KV_HEADS = 4
ATT_DHALF = 64
ATT_DV = 128
HG_HEADS = 8
HG_DK = 128
HG_DV = 128
N_GROUPS = 4
EXP_PER_GROUP = 8
N_EXPERTS = 32
D_FF = 512
EPS = 1e-6
LANES = 128
VMEM_LIMIT = 56 * 1024 * 1024

LOG2E = 1.4426950408889634
Q_SCALE = (ATT_DHALF ** -0.5) * LOG2E

MOE_TILE = 512
HG_CHUNK = 64
HG_LEVELS = 6


def _dot(a, b):
    return jnp.dot(a, b, preferred_element_type=F32)


def _dot_nt(a, b):
    return lax.dot_general(a, b, (((1,), (1,)), ((), ())), preferred_element_type=F32)


def _dot_tn(a, b):
    return lax.dot_general(a, b, (((0,), (0,)), ((), ())), preferred_element_type=F32)


def _split3(x):
    hi = x.astype(BF16)
    r1 = x - hi.astype(F32)
    mid = r1.astype(BF16)
    lo = (r1 - mid.astype(F32)).astype(BF16)
    return hi, mid, lo


def _seg_rms(z, seg, gain_row, scale):
    outs = []
    for i in range(z.shape[1] // LANES):
        zi = z[:, i * LANES:(i + 1) * LANES]
        z2 = zi * zi
        hi = z2.astype(BF16)
        lo = (z2 - hi.astype(F32)).astype(BF16)
        ss = _dot(hi, seg) + _dot(lo, seg)
        y = zi * lax.rsqrt(ss * (1.0 / ATT_DHALF) + EPS) * gain_row
        if scale != 1.0:
            y = y * scale
        outs.append(y)
    return jnp.concatenate(outs, axis=1)


def _params(*sem):
    return pltpu.CompilerParams(dimension_semantics=sem, vmem_limit_bytes=VMEM_LIMIT)


def _rms_kernel(x_ref, g_ref, h_ref):
    x = x_ref[...]
    ms = jnp.mean(x * x, axis=-1, keepdims=True)
    h_ref[...] = (x * lax.rsqrt(ms + EPS) * g_ref[...]).astype(BF16)


def _rms_cast(x, g, tm):
    n = x.shape[0]
    return pl.pallas_call(
        _rms_kernel,
        grid=(n // tm,),
        in_specs=[pl.BlockSpec((tm, D_MODEL), lambda i: (i, 0)), pl.BlockSpec((1, D_MODEL), lambda i: (0, 0))],
        out_specs=pl.BlockSpec((tm, D_MODEL), lambda i: (i, 0)),
        out_shape=jax.ShapeDtypeStruct((n, D_MODEL), BF16),
        compiler_params=_params("parallel"),
        name="rms_in",
    )(x, g)


def _proj_kernel(h_ref, w_ref, *refs, mode, n_aux):
    aux, outs = refs[:n_aux], refs[n_aux:]
    z = _dot(h_ref[...], w_ref[...])
    if mode == "q":
        outs[0][...] = _seg_rms(z, aux[0][...], aux[1][...], Q_SCALE).astype(BF16)
    elif mode in ("kv", "kv_t"):
        k = _seg_rms(z[:, :512], aux[0][...], aux[1][...], 1.0)
        if mode == "kv_t":
            outs[0][0] = k.T
            tm = z.shape[0]
            for n in range(ATT_KV_HEADS):
                outs[1][pl.ds(n, tm, stride=ATT_KV_HEADS), :] = z[:, 512 + n * LANES:512 + (n + 1) * LANES]
        else:
            outs[0][...] = k
            outs[1][...] = z[:, 512:]
    elif mode == "hq":
        outs[0][...] = (z * (HG_DK ** -0.5)).astype(BF16)
    elif mode == "hf":
        lbp = aux[0][...]
        m = jnp.max(lbp, axis=0, keepdims=True)
        e = jnp.exp(lbp - m)
        lb = e[0:1] / jnp.sum(e, axis=0, keepdims=True)
        sig = jax.nn.sigmoid(z)
        f = lb + (1.0 - lb) * sig
        outs[0][...] = jnp.log(f)
        outs[1][...] = (1.0 - lb) * (1.0 - sig)
    elif mode == "hv":
        outs[0][...] = z.astype(BF16)
    elif mode == "silu":
        outs[0][...] = z * jax.nn.sigmoid(z)
    elif mode == "sigmoid":
        outs[0][...] = jax.nn.sigmoid(z)


def _proj(h, w, col, mode, aux, outs, tm, name):
    n = h.shape[0]
    aux_specs = [pl.BlockSpec(a.shape, lambda i: (0, 0)) for a in aux]
    res = pl.pallas_call(
        functools.partial(_proj_kernel, mode=mode, n_aux=len(aux)),
        grid=(n // tm,),
        in_specs=[pl.BlockSpec((tm, D_MODEL), lambda i: (i, 0)),
                  pl.BlockSpec((D_MODEL, 1024), lambda i, col=col: (0, col))] + aux_specs,
        out_specs=tuple(pl.BlockSpec(o[2], o[3]) for o in outs),
        out_shape=tuple(jax.ShapeDtypeStruct(o[0], o[1]) for o in outs),
        compiler_params=_params("parallel"),
        name=name,
    )(h, w, *aux)
    return res


def _inproj(x, gin, w, qn_row, kn_row, lbp, seg, tm, seq):
    n = x.shape[0]
    h = _rms_cast(x, gin, tm)
    tok = lambda width, dt: ((n, width), dt, (tm, width), lambda i: (i, 0))
    q, = _proj(h, w, 0, "q", [seg, qn_row], [tok(1024, BF16)], tm, "proj_q")
    if seq is None:
        k, v = _proj(h, w, 1, "kv", [seg, kn_row], [tok(512, F32), tok(512, F32)], tm, "proj_kv")
    else:
        b, t = seq
        tpb = t // tm
        k_out = ((b, 512, t), F32, (1, 512, tm), lambda i: (i // tpb, 0, i % tpb))
        v_out = ((n * ATT_KV_HEADS, LANES), F32, (tm * ATT_KV_HEADS, LANES), lambda i: (i, 0))
        k, v = _proj(h, w, 1, "kv_t", [seg, kn_row], [k_out, v_out], tm, "proj_kv")
    hq, = _proj(h, w, 2, "hq", [], [tok(1024, BF16)], tm, "proj_hq")
    logf, kk = _proj(h, w, 3, "hf", [lbp], [tok(1024, F32), tok(1024, F32)], tm, "proj_hf")
    hv, = _proj(h, w, 4, "hv", [], [tok(1024, BF16)], tm, "proj_hv")
    og, = _proj(h, w, 5, "silu", [], [tok(1024, F32)], tm, "proj_og")
    sga, = _proj(h, w, 6, "sigmoid", [], [tok(1024, F32)], tm, "proj_ga")
    sgh, = _proj(h, w, 7, "sigmoid", [], [tok(1024, F32)], tm, "proj_gh")
    return q, k, v, hq, logf, kk, hv, og, sga, sgh


def _diff_lambda(lp, lam_init):
    a = jnp.sum(lp[0:1] * lp[1:2], axis=-1, keepdims=True)
    b = jnp.sum(lp[2:3] * lp[3:4], axis=-1, keepdims=True)
    return jnp.exp(a) - jnp.exp(b) + lam_init


def _sub_norm(o, sub_row, lam_init):
    ms = jnp.mean(o * o, axis=-1, keepdims=True)
    return o * lax.rsqrt(ms + EPS) * sub_row * (1.0 - lam_init)


def _attn_kernel(q_ref, kt_ref, v_ref, lamp_ref, sub_ref, o_ref,
                 kb_scr, vb_scr, qs_scr, m_scr, acc_scr, *, tq, lam_init):
    qi = pl.program_id(2)

    @pl.when(qi == 0)
    def _():
        kb_scr[...] = kt_ref[0].astype(BF16)
        t_len = vb_scr.shape[0]
        vb_scr[:, 0:LANES] = v_ref[0, pl.ds(pl.program_id(1), t_len, stride=ATT_KV_HEADS), :].astype(BF16)
        vb_scr[:, LANES:2 * LANES] = jnp.ones((vb_scr.shape[0], LANES), BF16)

    q = q_ref[0].astype(F32)
    lane = lax.broadcasted_iota(jnp.int32, (tq, LANES), 1)
    for g in range(2):
        qg = q[:, g * LANES:(g + 1) * LANES]
        for c in range(2):
            keep = (lane < ATT_DHALF) if c == 0 else (lane >= ATT_DHALF)
            r = g * 2 + c
            qs_scr[r * tq:(r + 1) * tq, :] = jnp.where(keep, qg, 0.0).astype(BF16)
    def chunk(j, masked, first):
        off = pl.multiple_of(j * tq, tq)
        s = _dot(qs_scr[...], kb_scr[:, pl.ds(off, tq)])
        if masked:
            row = lax.rem(lax.broadcasted_iota(jnp.int32, s.shape, 0), tq)
            s = jnp.where(lax.broadcasted_iota(jnp.int32, s.shape, 1) <= row, s, -jnp.inf)
        s_max = jnp.max(s, axis=-1, keepdims=True)
        if first:
            m_new = jnp.broadcast_to(s_max, m_scr.shape)
        else:
            m_prev = m_scr[...]
            m_new = jnp.maximum(m_prev, s_max)
            alpha = jnp.exp2(m_prev - m_new)
        pr = jnp.exp2(s - jnp.concatenate([m_new] * (tq // LANES), axis=1))
        pv = _dot(pr.astype(BF16), vb_scr[pl.ds(off, tq), :])
        acc_scr[...] = pv if first else jnp.concatenate([alpha, alpha], axis=1) * acc_scr[...] + pv
        m_scr[...] = m_new

    def body(j, carry):
        chunk(j, False, False)
        return carry

    @pl.when(qi == 0)
    def _():
        chunk(0, True, True)

    @pl.when(qi > 0)
    def _():
        chunk(0, False, True)
        lax.fori_loop(1, qi, body, 0)
        chunk(qi, True, False)

    lam = _diff_lambda(lamp_ref[...], lam_init)
    acc = acc_scr[...]
    out = acc[:, 0:LANES] / acc[:, LANES:2 * LANES]
    for g in range(2):
        o0 = out[(2 * g) * tq:(2 * g + 1) * tq]
        o1 = out[(2 * g + 1) * tq:(2 * g + 2) * tq]
        o_ref[0, :, g * LANES:(g + 1) * LANES] = _sub_norm(o0 - lam * o1, sub_ref[...], lam_init).astype(BF16)


def _attn_prompt(q, kt, v, lam_p, sub_row, lam_init, tq):
    b, t, _ = q.shape
    return pl.pallas_call(
        functools.partial(_attn_kernel, tq=tq, lam_init=lam_init),
        grid=(b, ATT_KV_HEADS, t // tq),
        in_specs=[
            pl.BlockSpec((1, tq, 256), lambda bi, n, qi: (bi, qi, n)),
            pl.BlockSpec((1, LANES, t), lambda bi, n, qi: (bi, n, 0)),
            pl.BlockSpec((1, t * ATT_KV_HEADS, LANES), lambda bi, n, qi: (bi, 0, 0)),
            pl.BlockSpec((4, ATT_DHALF), lambda bi, n, qi: (0, 0)),
            pl.BlockSpec((1, LANES), lambda bi, n, qi: (0, 0)),
        ],
        out_specs=pl.BlockSpec((1, tq, 256), lambda bi, n, qi: (bi, qi, n)),
        out_shape=jax.ShapeDtypeStruct((b, t, 1024), BF16),
        scratch_shapes=[pltpu.VMEM((LANES, t), BF16), pltpu.VMEM((t, 2 * LANES), BF16),
                        pltpu.VMEM((4 * tq, LANES), BF16), pltpu.VMEM((4 * tq, LANES), F32),
                        pltpu.VMEM((4 * tq, 2 * LANES), F32)],
        compiler_params=_params("parallel", "parallel", "arbitrary"),
        name="attn_prompt",
    )(q, kt, v, lam_p, sub_row)


def _hgrn_tables(c, levels):
    t = np.arange(c)[:, None]
    u = np.arange(c)[None, :]
    w = [(u <= t), (u > t)]
    masks = [(t == u)]
    for l in range(1, levels + 1):
        n = 1 << l
        half = n // 2
        mid = (t // n) * n + half
        hi = t >= mid
        w.append(np.where(hi, (u > mid) & (u <= t), (u > t) & (u <= mid)))
        masks.append((t // n == u // n) & (t % n >= half) & (u % n < half))
    w = np.concatenate(w, axis=0).astype(np.float32)
    w3 = np.concatenate([w, w, w], axis=1)
    if len(masks) % 2:
        masks.append(np.zeros_like(masks[0]))
    pairs = [np.concatenate([masks[i], masks[i + 1]], axis=1) for i in range(0, len(masks), 2)]
    return w3, np.stack(pairs).astype(np.float32)


def _hgrn_kernel(q_ref, g_ref, k_ref, v_ref, og_ref, norm_ref, w_ref, mask_ref, o_ref, s_ref,
                 st_scr, *, c, levels, n_chunks):
    ti = pl.program_id(2)

    @pl.when(ti == 0)
    def _():
        st_scr[...] = jnp.zeros(st_scr.shape, F32)

    w = w_ref[...]
    zeros = jnp.zeros((c, LANES), BF16)
    chunk_rows = [slice(ci * c, (ci + 1) * c) for ci in range(n_chunks)]
    g3 = jnp.concatenate([jnp.concatenate(_split3(g_ref[0, rows, :]), axis=0) for rows in chunk_rows], axis=1)
    e_all = jnp.exp(_dot(w, g3))

    parts = []
    for ci, rows in enumerate(chunk_rows):
        q = q_ref[0, rows, :].astype(F32)
        k = k_ref[0, rows, :]
        v = v_ref[0, rows, :]
        e = e_all[:, ci * LANES:(ci + 1) * LANES]
        e_b = e[0:c]
        e_k = e[c:2 * c]
        qs = [q.astype(BF16)]
        ks = [k.astype(BF16)]
        for l in range(1, levels + 1):
            e_l = e[(l + 1) * c:(l + 2) * c]
            qs.append((q * e_l).astype(BF16))
            ks.append((k * e_l).astype(BF16))
        if len(qs) % 2:
            qs.append(zeros)
            ks.append(zeros)
        a2 = None
        for p in range(len(qs) // 2):
            ql = jnp.concatenate([qs[2 * p], qs[2 * p + 1]], axis=1)
            kbd = jnp.concatenate([jnp.concatenate([ks[2 * p], zeros], axis=1),
                                   jnp.concatenate([zeros, ks[2 * p + 1]], axis=1)], axis=0)
            term = mask_ref[p] * _dot_nt(ql, kbd)
            a2 = term if a2 is None else a2 + term
        o_intra = _dot(a2.astype(BF16), jnp.concatenate([v, v], axis=0))
        st_add = _dot_tn(v, (k * e_k).astype(BF16))
        parts.append(((q * e_b).astype(BF16), e_b[c - 1:c, :], o_intra, st_add))

    st = st_scr[...]
    for rows, (q_b, decay, o_intra, st_add) in zip(chunk_rows, parts):
        o = _dot_nt(q_b, st.astype(BF16)) + o_intra
        st = st * decay + st_add
        ms = jnp.mean(o * o, axis=-1, keepdims=True)
        y = o * lax.rsqrt(ms + EPS) * norm_ref[...] * og_ref[0, rows, :]
        o_ref[0, rows, :] = y.astype(BF16)
    st_scr[...] = st

    @pl.when(ti == pl.num_programs(2) - 1)
    def _():
        s_ref[0, 0] = st.T


def _hgrn_prompt(hq, logf, kk, hv, og, norm_row, ct):
    b, t, _ = hq.shape
    c, levels = HG_CHUNK, HG_LEVELS
    w_np, m_np = _hgrn_tables(c, levels)
    w = jnp.asarray(w_np, BF16)
    masks = jnp.asarray(m_np, F32)
    tok = pl.BlockSpec((1, ct, LANES), lambda bi, h, ti: (bi, ti, h))
    return pl.pallas_call(
        functools.partial(_hgrn_kernel, c=c, levels=levels, n_chunks=ct // c),
        grid=(b, HG_HEADS, t // ct),
        in_specs=[tok, tok, tok, tok, tok,
                  pl.BlockSpec((1, LANES), lambda bi, h, ti: (0, 0)),
                  pl.BlockSpec(w.shape, lambda bi, h, ti: (0, 0)),
                  pl.BlockSpec(masks.shape, lambda bi, h, ti: (0, 0, 0))],
        out_specs=(tok, pl.BlockSpec((1, 1, HG_DK, HG_DV), lambda bi, h, ti: (bi, h, 0, 0))),
        out_shape=(jax.ShapeDtypeStruct((b, t, 1024), BF16),
                   jax.ShapeDtypeStruct((b, HG_HEADS, HG_DK, HG_DV), F32)),
        scratch_shapes=[pltpu.VMEM((HG_DV, HG_DK), F32)],
        compiler_params=_params("parallel", "parallel", "arbitrary"),
        name="hgrn_prompt",
    )(hq, logf, kk, hv, og, norm_row, w, masks)


def _dec_attn_kernel(pt_ref, wq_ref, kn_ref, vn_ref, lamp_ref, sub_ref, *refs, pages, lam_init):
    k_refs = refs[:pages]
    v_refs = refs[pages:2 * pages]
    o_ref = refs[2 * pages]
    m_scr, l_scr, acc_scr = refs[2 * pages + 1:]
    j = pl.program_id(1)
    page = k_refs[0].shape[2]

    @pl.when(j == 0)
    def _():
        m_scr[...] = jnp.full(m_scr.shape, -jnp.inf, F32)
        l_scr[...] = jnp.zeros(l_scr.shape, F32)
        acc_scr[...] = jnp.zeros(acc_scr.shape, F32)

    rown = lax.rem(lax.broadcasted_iota(jnp.int32, (16, LANES), 0), 8) // 2
    wq = wq_ref[0]
    s = jnp.concatenate([_dot(wq, k_refs[i][0].astype(BF16)) for i in range(pages)], axis=1)
    m_prev = m_scr[...]
    m_new = jnp.maximum(m_prev, jnp.max(s, axis=-1, keepdims=True))
    alpha = jnp.exp2(m_prev - m_new)
    pr = jnp.exp2(s - m_new)
    l_scr[...] = alpha * l_scr[...] + jnp.sum(pr, axis=-1, keepdims=True)
    prb = pr.astype(BF16)
    pv = jnp.zeros((16, LANES), F32)
    for n in range(ATT_KV_HEADS):
        v_n = jnp.concatenate([v_refs[i][0, pl.ds(n, page, stride=ATT_KV_HEADS), :] for i in range(pages)],
                              axis=0).astype(BF16)
        pv = pv + jnp.where(rown == n, _dot(prb, v_n), 0.0)
    acc_scr[...] = alpha * acc_scr[...] + pv
    m_scr[...] = m_new

    @pl.when(j == pl.num_programs(1) - 1)
    def _():
        s_cur = jnp.sum(wq.astype(F32) * kn_ref[0], axis=-1, keepdims=True)
        m_old = m_scr[...]
        m_fin = jnp.maximum(m_old, s_cur)
        al = jnp.exp2(m_old - m_fin)
        pc = jnp.exp2(s_cur - m_fin)
        l_fin = al * l_scr[...] + pc
        vn = vn_ref[0]
        v_cur = jnp.zeros((16, LANES), F32)
        for n in range(ATT_KV_HEADS):
            v_cur = v_cur + jnp.where(rown == n, vn[:, n * LANES:(n + 1) * LANES], 0.0)
        out = (al * acc_scr[...] + pc * v_cur) / l_fin
        lam = _diff_lambda(lamp_ref[...], lam_init)
        o = out[0:8] - lam * out[8:16]
        o_ref[0] = _sub_norm(o, sub_ref[...], lam_init)


def _attn_sample(wq, k_new, v_new, kt_pages, v_pages, page_table, lam_p, sub_row, lam_init, pages):
    db, n_pages = page_table.shape
    pt_flat = page_table.reshape(-1)
    blk = kt_pages.shape[1:]

    def page_spec(i):
        return pl.BlockSpec((1,) + blk, lambda bi, j, pt, i=i: (pt[bi * n_pages + j * pages + i], 0, 0))

    row3 = lambda w: pl.BlockSpec((1, 1, w), lambda bi, j, pt: (bi, 0, 0))
    grid_spec = pltpu.PrefetchScalarGridSpec(
        num_scalar_prefetch=1,
        grid=(db, n_pages // pages),
        in_specs=[pl.BlockSpec((1, 16, 512), lambda bi, j, pt: (bi, 0, 0)), row3(512), row3(512),
                  pl.BlockSpec((4, ATT_DHALF), lambda bi, j, pt: (0, 0)),
                  pl.BlockSpec((1, LANES), lambda bi, j, pt: (0, 0))]
                 + [page_spec(i) for i in range(pages)] + [page_spec(i) for i in range(pages)],
        out_specs=pl.BlockSpec((1, 8, LANES), lambda bi, j, pt: (bi, 0, 0)),
        scratch_shapes=[pltpu.VMEM((16, 1), F32), pltpu.VMEM((16, 1), F32), pltpu.VMEM((16, LANES), F32)],
    )
    return pl.pallas_call(
        functools.partial(_dec_attn_kernel, pages=pages, lam_init=lam_init),
        grid_spec=grid_spec,
        out_shape=jax.ShapeDtypeStruct((db, 8, LANES), F32),
        compiler_params=_params("parallel", "arbitrary"),
        name="attn_sample",
    )(pt_flat, wq, k_new, v_new, lam_p, sub_row, *([kt_pages] * pages), *([v_pages] * pages))


def _hgrn_step_kernel(s_ref, q_ref, g_ref, k_ref, v_ref, og_ref, norm_ref, o_ref, sn_ref):
    pad = jnp.zeros((LANES - 3 * HG_HEADS, LANES), F32)
    for bi in range(s_ref.shape[0]):
        cols = jnp.concatenate([q_ref[bi], g_ref[bi], k_ref[bi], pad], axis=0).T
        v_all = v_ref[bi]
        og_all = og_ref[bi]
        outs = []
        for h in range(HG_HEADS):
            s = s_ref[bi, h]
            qc = cols[:, h:h + 1]
            f = jnp.exp(cols[:, HG_HEADS + h:HG_HEADS + h + 1])
            kc = cols[:, 2 * HG_HEADS + h:2 * HG_HEADS + h + 1]
            v = v_all[h:h + 1]
            o = jnp.sum(s * (qc * f), axis=0, keepdims=True) + jnp.sum(qc * kc, axis=0, keepdims=True) * v
            sn_ref[bi, h] = s * f + kc * v
            ms = jnp.mean(o * o, axis=-1, keepdims=True)
            outs.append(o * lax.rsqrt(ms + EPS) * norm_ref[...] * og_all[h:h + 1])
        o_ref[bi] = jnp.concatenate(outs, axis=0)


def _hgrn_sample(state, q, g, k, v, og, norm_row):
    db = state.shape[0]
    nb = _pick(db, 2)
    row = pl.BlockSpec((nb, HG_HEADS, HG_DK), lambda bi: (bi, 0, 0))
    st = pl.BlockSpec((nb, HG_HEADS, HG_DK, HG_DV), lambda bi: (bi, 0, 0, 0))
    return pl.pallas_call(
        _hgrn_step_kernel,
        grid=(db // nb,),
        in_specs=[st, row, row, row, row, row, pl.BlockSpec((1, LANES), lambda bi: (0, 0))],
        out_specs=(row, st),
        out_shape=(jax.ShapeDtypeStruct((db, HG_HEADS, HG_DV), F32),
                   jax.ShapeDtypeStruct(state.shape, F32)),
        compiler_params=_params("parallel"),
        name="hgrn_sample",
    )(state, q, g, k, v, og, norm_row)


def _lane_first(cond, lane):
    return jnp.min(jnp.where(cond, lane, LANES), axis=-1, keepdims=True)


def _route(logits):
    lane = lax.broadcasted_iota(jnp.int32, logits.shape, 1)
    gmask = (lane >= N_EXPERTS) & (lane < N_EXPERTS + N_GROUPS)
    lg = jnp.where(gmask, logits, -jnp.inf)
    mg = jnp.max(lg, axis=-1, keepdims=True)
    eg = jnp.exp(lg - mg)
    pg = eg / jnp.sum(eg, axis=-1, keepdims=True)
    pg_top = jnp.max(pg, axis=-1, keepdims=True)
    gsel = _lane_first(gmask & (pg == pg_top), lane) - N_EXPERTS
    emask = (lane >= gsel * EXP_PER_GROUP) & (lane < (gsel + 1) * EXP_PER_GROUP)
    le = jnp.where(emask, logits, -jnp.inf)
    me = jnp.max(le, axis=-1, keepdims=True)
    ee = jnp.exp(le - me)
    pe = ee / jnp.sum(ee, axis=-1, keepdims=True)
    pe = jnp.where(emask, pe, -1.0)
    p1 = jnp.max(pe, axis=-1, keepdims=True)
    e1 = _lane_first(pe == p1, lane)
    pe2 = jnp.where(lane == e1, -1.0, pe)
    p2 = jnp.max(pe2, axis=-1, keepdims=True)
    e2 = _lane_first(pe2 == p2, lane)
    den = p1 + p2
    w1 = p1 / den * pg_top
    w2 = p2 / den * pg_top
    comb = jnp.where(lane == e1, w1, 0.0) + jnp.where(lane == e2, w2, 0.0)
    picks = jnp.where(lane == 0, e1.astype(F32), jnp.where(lane == 1, e2.astype(F32),
                      jnp.where(lane == 2, w1, jnp.where(lane == 3, w2, 0.0))))
    return comb, picks


ROW_TILES = D_MODEL // LANES


def _merge_kernel(oa_ref, oh_ref, sga_ref, sgh_ref, x_ref, wpa_ref, wph_ref, wo_ref, gf_ref,
                  wr_ref, br_ref, x1_ref, xn_ref, route_ref, *, sparse):
    ya = _dot(oa_ref[...], wpa_ref[...])
    yh = _dot(oh_ref[...], wph_ref[...])
    mixed = (sga_ref[...] * ya + sgh_ref[...] * yh).astype(BF16)
    x1 = x_ref[...] + _dot(mixed, wo_ref[...])
    x1_ref[...] = x1
    ms = jnp.mean(x1 * x1, axis=-1, keepdims=True)
    xn_f = x1 * lax.rsqrt(ms + EPS) * gf_ref[...]
    xn = xn_f.astype(BF16)
    comb, picks = _route(_dot(xn, wr_ref[...]) + br_ref[...])
    if sparse:
        tm = x1.shape[0]
        for s in range(ROW_TILES):
            xn_ref[pl.ds(s, tm, stride=ROW_TILES), :] = xn_f[:, s * LANES:(s + 1) * LANES]
        route_ref[...] = picks
    else:
        xn_ref[...] = xn
        route_ref[...] = comb


def _merge(oa, oh, sga, sgh, x, wpa, wph, wo, gf, wr, br, tm, sparse):
    n = x.shape[0]
    tok = pl.BlockSpec((tm, 1024), lambda i: (i, 0))
    wsp = pl.BlockSpec((1024, 1024), lambda i: (0, 0))
    if sparse:
        xn_spec = pl.BlockSpec((tm * ROW_TILES, LANES), lambda i: (i, 0))
        xn_shape = jax.ShapeDtypeStruct((n * ROW_TILES, LANES), F32)
    else:
        xn_spec, xn_shape = tok, jax.ShapeDtypeStruct((n, 1024), BF16)
    return pl.pallas_call(
        functools.partial(_merge_kernel, sparse=sparse),
        grid=(n // tm,),
        in_specs=[tok, tok, tok, tok, tok, wsp, wsp, wsp,
                  pl.BlockSpec((1, 1024), lambda i: (0, 0)),
                  pl.BlockSpec((1024, LANES), lambda i: (0, 0)),
                  pl.BlockSpec((1, LANES), lambda i: (0, 0))],
        out_specs=(tok, xn_spec, pl.BlockSpec((tm, LANES), lambda i: (i, 0))),
        out_shape=(jax.ShapeDtypeStruct((n, 1024), F32), xn_shape,
                   jax.ShapeDtypeStruct((n, LANES), F32)),
        compiler_params=_params("parallel"),
        name="merge",
    )(oa, oh, sga, sgh, x, wpa, wph, wo, gf, wr, br)


def _moe_kernel(xn_ref, comb_ref, x1_ref, wg_ref, wu_ref, wd_ref, y_ref, acc_scr):
    e = pl.program_id(1)

    @pl.when(e == 0)
    def _():
        acc_scr[...] = jnp.zeros(acc_scr.shape, F32)

    comb = comb_ref[...]
    lane = lax.broadcasted_iota(jnp.int32, comb.shape, 1)
    c = jnp.sum(jnp.where(lane == e, comb, 0.0), axis=-1, keepdims=True)
    xn = xn_ref[...]
    hg = _dot(xn, wg_ref[0])
    hu = _dot(xn, wu_ref[0])
    hid = hg * jax.nn.sigmoid(hg) * hu
    acc_scr[...] += _dot((hid * c).astype(BF16), wd_ref[0])

    @pl.when(e == pl.num_programs(1) - 1)
    def _():
        y_ref[...] = x1_ref[...] + acc_scr[...]


def _moe(xn, comb, x1, wg, wu, wd, tm):
    n = xn.shape[0]
    return pl.pallas_call(
        _moe_kernel,
        grid=(n // tm, N_EXPERTS),
        in_specs=[pl.BlockSpec((tm, 1024), lambda i, e: (i, 0)),
                  pl.BlockSpec((tm, LANES), lambda i, e: (i, 0)),
                  pl.BlockSpec((tm, 1024), lambda i, e: (i, 0)),
                  pl.BlockSpec((1, 1024, D_FF), lambda i, e: (e, 0, 0)),
                  pl.BlockSpec((1, 1024, D_FF), lambda i, e: (e, 0, 0)),
                  pl.BlockSpec((1, D_FF, 1024), lambda i, e: (e, 0, 0))],
        out_specs=pl.BlockSpec((tm, 1024), lambda i, e: (i, 0)),
        out_shape=jax.ShapeDtypeStruct((n, 1024), F32),
        scratch_shapes=[pltpu.VMEM((tm, 1024), F32)],
        compiler_params=_params("parallel", "arbitrary"),
        name="moe",
    )(xn, comb, x1, wg, wu, wd)


def _row_copy(src_hbm, src_row, dst, dst_row, sem):
    return pltpu.make_async_copy(src_hbm.at[pl.ds(src_row * ROW_TILES, ROW_TILES)],
                                 dst.at[pl.ds(dst_row * ROW_TILES, ROW_TILES)], sem)


def _wait_rows(src_hbm, dst, sem):
    pltpu.make_async_copy(src_hbm.at[pl.ds(0, dst.shape[0])], dst, sem).wait()


def _rows_to_mat(buf, n_rows):
    return jnp.concatenate([buf[pl.ds(s, n_rows, stride=ROW_TILES), :] for s in range(ROW_TILES)], axis=1)


def _moe_scatter_kernel(p1_ref, p2_ref, xn_ref, xs_in, xs_hbm, sem):
    del xs_in
    i = pl.program_id(0)
    tm = p1_ref.shape[0]

    def wait_tile():
        for _ in range(2):
            pltpu.make_async_copy(xn_ref, xs_hbm.at[pl.ds(0, tm * ROW_TILES)], sem.at[0]).wait()

    def body(r, carry):
        src = xn_ref.at[pl.ds(r * ROW_TILES, ROW_TILES)]
        pltpu.make_async_copy(src, xs_hbm.at[pl.ds(p1_ref[r] * ROW_TILES, ROW_TILES)], sem.at[0]).start(priority=0)
        pltpu.make_async_copy(src, xs_hbm.at[pl.ds(p2_ref[r] * ROW_TILES, ROW_TILES)], sem.at[0]).start(priority=1)
        return carry

    lax.fori_loop(0, tm, body, 0)
    wait_tile()


def _moe_scatter(xg, pos1, pos2, rows, tm):
    n = pos1.shape[0]
    idx = pl.BlockSpec((tm,), lambda i: (i,), memory_space=pltpu.SMEM)
    xs0 = jnp.zeros((rows * ROW_TILES, LANES), F32)
    return pl.pallas_call(
        _moe_scatter_kernel,
        grid=(n // tm,),
        in_specs=[idx, idx, pl.BlockSpec((tm * ROW_TILES, LANES), lambda i: (i, 0)),
                  pl.BlockSpec(memory_space=pl.ANY)],
        out_specs=pl.BlockSpec(memory_space=pl.ANY),
        out_shape=jax.ShapeDtypeStruct(xs0.shape, F32),
        scratch_shapes=[pltpu.SemaphoreType.DMA((1,))],
        input_output_aliases={3: 0},
        compiler_params=_params("arbitrary"),
        name="moe_scatter",
    )(pos1, pos2, xg, xs0)


def _moe_ffn_kernel(te_ref, xs_ref, wg_ref, wu_ref, wd_ref, ys_ref, *, tp):
    del te_ref
    x = _rows_to_mat(xs_ref, tp).astype(BF16)
    hg = _dot(x, wg_ref[0])
    hu = _dot(x, wu_ref[0])
    hid = (hg * jax.nn.sigmoid(hg) * hu).astype(BF16)
    out = _dot(hid, wd_ref[0])
    for s in range(ROW_TILES):
        ys_ref[pl.ds(s, tp, stride=ROW_TILES), :] = out[:, s * LANES:(s + 1) * LANES]


def _moe_ffn(xs, tile_expert, wg, wu, wd, tp):
    n_tiles = tile_expert.shape[0]
    rows = pl.BlockSpec((tp * ROW_TILES, LANES), lambda i, te: (i, 0))
    wsp = lambda a, b: pl.BlockSpec((1, a, b), lambda i, te: (te[i], 0, 0))
    grid_spec = pltpu.PrefetchScalarGridSpec(
        num_scalar_prefetch=1,
        grid=(n_tiles,),
        in_specs=[rows, wsp(1024, D_FF), wsp(1024, D_FF), wsp(D_FF, 1024)],
        out_specs=rows,
    )
    return pl.pallas_call(
        functools.partial(_moe_ffn_kernel, tp=tp),
        grid_spec=grid_spec,
        out_shape=jax.ShapeDtypeStruct(xs.shape, F32),
        compiler_params=_params("parallel"),
        name="moe_ffn",
    )(tile_expert, xs, wg, wu, wd)


def _moe_combine_kernel(p1_ref, p1n_ref, p2_ref, p2n_ref, ys_hbm, x1_ref, w1_ref, w2_ref, y_ref, buf, sem, *, tm):
    i = pl.program_id(0)
    slot = lax.rem(i, 2)

    def gather(a_ref, b_ref, s):
        def body(r, carry):
            _row_copy(ys_hbm, a_ref[r], buf.at[s, 0], r, sem.at[s, 0]).start(priority=0)
            _row_copy(ys_hbm, b_ref[r], buf.at[s, 1], r, sem.at[s, 1]).start(priority=1)
            return carry
        lax.fori_loop(0, tm, body, 0)

    @pl.when(i == 0)
    def _():
        gather(p1_ref, p2_ref, 0)

    @pl.when(i + 1 < pl.num_programs(0))
    def _():
        gather(p1n_ref, p2n_ref, 1 - slot)

    _wait_rows(ys_hbm, buf.at[slot, 0], sem.at[slot, 0])
    _wait_rows(ys_hbm, buf.at[slot, 1], sem.at[slot, 1])
    y_ref[...] = x1_ref[...] + (w1_ref[...] * _rows_to_mat(buf.at[slot, 0], tm)
                                + w2_ref[...] * _rows_to_mat(buf.at[slot, 1], tm))


def _moe_combine(ys, pos1, pos2, w1, w2, x1, tm):
    n = x1.shape[0]
    n_tiles = n // tm
    idx = lambda off: pl.BlockSpec((tm,), lambda i: (jnp.minimum(i + off, n_tiles - 1),),
                                   memory_space=pltpu.SMEM)
    col = pl.BlockSpec((tm, 1), lambda i: (i, 0))
    return pl.pallas_call(
        functools.partial(_moe_combine_kernel, tm=tm),
        grid=(n_tiles,),
        in_specs=[idx(0), idx(1), idx(0), idx(1), pl.BlockSpec(memory_space=pl.ANY),
                  pl.BlockSpec((tm, 1024), lambda i: (i, 0)), col, col],
        out_specs=pl.BlockSpec((tm, 1024), lambda i: (i, 0)),
        out_shape=jax.ShapeDtypeStruct((n, 1024), F32),
        scratch_shapes=[pltpu.VMEM((2, 2, tm * ROW_TILES, LANES), F32), pltpu.SemaphoreType.DMA((2, 2))],
        compiler_params=_params("arbitrary"),
        name="moe_combine",
    )(pos1, pos1, pos2, pos2, ys, x1, w1, w2)


def _rank_kernel(picks_ref, tri_ref, rank_ref, cnt_ref, base_scr):
    @pl.when(pl.program_id(0) == 0)
    def _():
        base_scr[...] = jnp.zeros(base_scr.shape, F32)

    picks = picks_ref[...]
    lane = lax.broadcasted_iota(jnp.int32, picks.shape, 1)
    chosen = (lane == picks[:, 0:1].astype(jnp.int32)) | (lane == picks[:, 1:2].astype(jnp.int32))
    both = jnp.where(chosen, 1.0, 0.0)
    before = _dot(tri_ref[...], both.astype(BF16)) + base_scr[...]
    rank_ref[...] = jnp.where(chosen, before, 0.0)
    base_scr[...] += jnp.sum(both, axis=0, keepdims=True)
    cnt_ref[...] = base_scr[...]


def _pos_kernel(picks_ref, rank_ref, start_ref, pos_ref):
    picks = picks_ref[...]
    lane = lax.broadcasted_iota(jnp.int32, picks.shape, 1)
    where_to = rank_ref[...] + start_ref[...]
    pos = [jnp.sum(jnp.where(lane == picks[:, p:p + 1].astype(jnp.int32), where_to, 0.0), axis=-1, keepdims=True)
           for p in range(2)]
    pos_ref[...] = jnp.where(lane == 0, pos[0], jnp.where(lane == 1, pos[1], 0.0))


def _dispatch(picks, tp, tm):
    n = picks.shape[0]
    tri = jnp.asarray(np.tril(np.ones((tm, tm), np.float32), -1), BF16)
    tok = pl.BlockSpec((tm, LANES), lambda i: (i, 0))
    one = pl.BlockSpec((1, LANES), lambda i: (0, 0))
    rank, cnt = pl.pallas_call(
        _rank_kernel,
        grid=(n // tm,),
        in_specs=[tok, pl.BlockSpec((tm, tm), lambda i: (0, 0))],
        out_specs=(tok, one),
        out_shape=(jax.ShapeDtypeStruct((n, LANES), F32), jax.ShapeDtypeStruct((1, LANES), F32)),
        scratch_shapes=[pltpu.VMEM((1, LANES), F32)],
        compiler_params=_params("arbitrary"),
        name="moe_rank",
    )(picks, tri)
    counts = cnt[0, :N_EXPERTS].astype(jnp.int32)
    tiles = (counts + tp - 1) // tp
    tile_end = jnp.cumsum(tiles)
    row_start = jnp.zeros((1, LANES), F32).at[0, :N_EXPERTS].set(((tile_end - tiles) * tp).astype(F32))
    tp_pos = _pick(n, 4 * tm)
    tok_pos = pl.BlockSpec((tp_pos, LANES), lambda i: (i, 0))
    pos = pl.pallas_call(
        _pos_kernel,
        grid=(n // tp_pos,),
        in_specs=[tok_pos, tok_pos, one],
        out_specs=tok_pos,
        out_shape=jax.ShapeDtypeStruct((n, LANES), F32),
        compiler_params=_params("parallel"),
        name="moe_pos",
    )(picks, rank, row_start)
    n_tiles = (2 * n) // tp + N_EXPERTS
    tile_ids = jnp.arange(n_tiles, dtype=jnp.int32)
    tile_expert = jnp.minimum(jnp.sum((tile_end[None, :] <= tile_ids[:, None]).astype(jnp.int32), axis=1),
                              N_EXPERTS - 1)
    return pos[:, 0].astype(jnp.int32), pos[:, 1].astype(jnp.int32), tile_expert, n_tiles * tp


def _pick(n, pref):
    t = min(n, pref)
    while n % t:
        t //= 2
    return t


def kernel(x_prompt, x_sample, cache_k, cache_v, state_hgrn, page_table, rms_in, w_in, att_q_norm, att_k_norm, att_lambda, att_sub_norm, hg_lower_bound, hg_out_norm, w_branch_att, w_branch_hg, w_out, rms_ffn, w_router_group, b_router_group, w_router_expert, b_router_expert, w_exp_gate, w_exp_up, w_exp_down):
    depth = rms_in.shape[0]
    assert depth == 1 and hg_lower_bound.shape[0] == 2
    b, t, _ = x_prompt.shape
    db, ds, _ = x_sample.shape
    assert ds == 1
    lam_init = 0.8 - 0.6 * math.exp(-0.3 * 0)

    w_in_bf = w_in[0].astype(BF16)
    qn_row = jnp.tile(att_q_norm[0], 2).reshape(1, LANES)
    kn_row = jnp.tile(att_k_norm[0], 2).reshape(1, LANES)
    seg_np = (np.arange(LANES)[:, None] // ATT_DHALF) == (np.arange(LANES)[None, :] // ATT_DHALF)
    seg = jnp.asarray(seg_np.astype(np.float32), BF16)
    sub_row = att_sub_norm[0].reshape(1, LANES)
    hgn_row = hg_out_norm[0].reshape(1, LANES)
    wpa = w_branch_att[0].astype(BF16)
    wph = w_branch_hg[0].astype(BF16)
    wo = w_out[0].astype(BF16)
    wr = jnp.zeros((D_MODEL, LANES), F32)
    wr = wr.at[:, :N_EXPERTS].set(w_router_expert[0]).at[:, N_EXPERTS:N_EXPERTS + N_GROUPS].set(w_router_group[0])
    wr = wr.astype(BF16)
    br = jnp.zeros((1, LANES), F32)
    br = br.at[0, :N_EXPERTS].set(b_router_expert[0]).at[0, N_EXPERTS:N_EXPERTS + N_GROUPS].set(b_router_group[0])
    weg = w_exp_gate[0].astype(BF16)
    weu = w_exp_up[0].astype(BF16)
    wed = w_exp_down[0].astype(BF16)
    gin = rms_in[0].reshape(1, D_MODEL)
    gffn = rms_ffn[0].reshape(1, D_MODEL)
    lam_p = att_lambda[0]

    def tail(x2, oa, oh, sga, sgh, tm_merge, tm_moe):
        sparse = x2.shape[0] >= 16 * MOE_TILE
        x1, xn, route = _merge(oa, oh, sga, sgh, x2, wpa, wph, wo, gffn, wr, br, tm_merge, sparse)
        if not sparse:
            return _moe(xn, route, x1, weg, weu, wed, tm_moe)
        pos1, pos2, tile_expert, rows = _dispatch(route, MOE_TILE, _pick(x2.shape[0], 1024))
        xs = _moe_scatter(xn, pos1, pos2, rows, _pick(x2.shape[0], 512))
        ys = _moe_ffn(xs, tile_expert, weg, weu, wed, MOE_TILE)
        return _moe_combine(ys, pos1, pos2, route[:, 2:3], route[:, 3:4], x1, _pick(x2.shape[0], 512))

    n = b * t
    xp = x_prompt.reshape(n, D_MODEL)
    q, kt, v, hq, logf, kk, hv, og, sga, sgh = _inproj(
        xp, gin, w_in_bf, qn_row, kn_row, hg_lower_bound, seg, _pick(t, 1024), (b, t))
    oa = _attn_prompt(q.reshape(b, t, 1024), kt, v.reshape(b, t * ATT_KV_HEADS, LANES), lam_p, sub_row, lam_init,
                      _pick(t, 512))
    r3 = lambda a: a.reshape(b, t, 1024)
    oh, state_p = _hgrn_prompt(r3(hq), r3(logf), r3(kk), r3(hv), r3(og), hgn_row, _pick(t, 1024))
    yp = tail(xp, oa.reshape(n, 1024), oh.reshape(n, 1024), sga, sgh, _pick(n, 512), _pick(n, 1024))
    k_prompt = jnp.transpose(kt.reshape(1, b, ATT_KV_HEADS, 2, ATT_DHALF, t), (0, 1, 5, 2, 3, 4))

    xs = x_sample.reshape(db, D_MODEL)
    q, ks, vs, hq, logf, kk, hv, og, sga, sgh = _inproj(
        xs, gin, w_in_bf, qn_row, kn_row, hg_lower_bound, seg, _pick(db, 1024), None)
    q5 = q.reshape(db, ATT_KV_HEADS, 2, 2, ATT_DHALF)
    eye_n = jnp.eye(ATT_KV_HEADS, dtype=BF16)
    eye_c = jnp.eye(2, dtype=BF16)
    wq = jnp.einsum('bngcd,nm,ce->bcngmed', q5, eye_n, eye_c).reshape(db, 16, 512)
    n_pool = cache_k.shape[1]
    page = cache_k.shape[2]
    kt_pages = jnp.transpose(cache_k[0], (0, 2, 3, 4, 1)).reshape(n_pool, 512, page)
    v_pages = cache_v[0].reshape(n_pool, page * ATT_KV_HEADS, ATT_DV)
    oa_s = _attn_sample(wq, ks.reshape(db, 1, 512), vs.reshape(db, 1, 512), kt_pages, v_pages,
                        page_table, lam_p, sub_row, lam_init, _pick(page_table.shape[1], 16))
    row = lambda a: a.astype(F32).reshape(db, HG_HEADS, HG_DK)
    oh_s, state_s = _hgrn_sample(state_hgrn[0], row(hq), row(logf), row(kk), row(hv), row(og), hgn_row)
    ys = tail(xs, oa_s.reshape(db, 1024).astype(BF16), oh_s.reshape(db, 1024).astype(BF16), sga, sgh,
              _pick(db, 512), _pick(db, 1024))

    return (yp.reshape(b, t, D_MODEL), ys.reshape(db, 1, D_MODEL),
            k_prompt, v.reshape(1, b, t, ATT_KV_HEADS, ATT_DV),
            state_p.reshape(1, b, HG_HEADS, HG_DK, HG_DV),
            ks.reshape(1, db, 1, ATT_KV_HEADS, 2, ATT_DHALF), vs.reshape(1, db, 1, ATT_KV_HEADS, ATT_DV),
            state_s.reshape(1, db, HG_HEADS, HG_DK, HG_DV))
```

```python
import functools
import math

import numpy as np
import jax
import jax.numpy as jnp
from jax import lax
from jax.experimental import pallas as pl
from jax.experimental.pallas import tpu as pltpu

F32 = jnp.float32
BF16 = jnp.bfloat16

D_MODEL = 1024
ATT_HEADS = 8
ATT_KV_HEADS = 4
ATT_DHALF = 64
ATT_DV = 128
HG_HEADS = 8
HG_DK = 128
HG_DV = 128
N_GROUPS = 4
EXP_PER_GROUP = 8
N_EXPERTS = 32
D_FF = 512
EPS = 1e-6
LANES = 128
VMEM_LIMIT = 56 * 1024 * 1024

LOG2E = 1.4426950408889634
Q_SCALE = (ATT_DHALF ** -0.5) * LOG2E

MOE_TILE = 512
HG_CHUNK = 64
HG_LEVELS = 6


def _dot(a, b):
    return jnp.dot(a, b, preferred_element_type=F32)


def _dot_nt(a, b):
    return lax.dot_general(a, b, (((1,), (1,)), ((), ())), preferred_element_type=F32)


def _dot_tn(a, b):
    return lax.dot_general(a, b, (((0,), (0,)), ((), ())), preferred_element_type=F32)


def _split3(x):
    hi = x.astype(BF16)
    r1 = x - hi.astype(F32)
    mid = r1.astype(BF16)
    lo = (r1 - mid.astype(F32)).astype(BF16)
    return hi, mid, lo


def _seg_rms(z, seg, gain_row, scale):
    outs = []
    for i in range(z.shape[1] // LANES):
        zi = z[:, i * LANES:(i + 1) * LANES]
        z2 = zi * zi
        hi = z2.astype(BF16)
        lo = (z2 - hi.astype(F32)).astype(BF16)
        ss = _dot(hi, seg) + _dot(lo, seg)
        y = zi * lax.rsqrt(ss * (1.0 / ATT_DHALF) + EPS) * gain_row
        if scale != 1.0:
            y = y * scale
        outs.append(y)
    return jnp.concatenate(outs, axis=1)


def _params(*sem):
    return pltpu.CompilerParams(dimension_semantics=sem, vmem_limit_bytes=VMEM_LIMIT)


def _rms_kernel(x_ref, g_ref, h_ref):
    x = x_ref[...]
    ms = jnp.mean(x * x, axis=-1, keepdims=True)
    h_ref[...] = (x * lax.rsqrt(ms + EPS) * g_ref[...]).astype(BF16)


def _rms_cast(x, g, tm):
    n = x.shape[0]
    return pl.pallas_call(
        _rms_kernel,
        grid=(n // tm,),
        in_specs=[pl.BlockSpec((tm, D_MODEL), lambda i: (i, 0)), pl.BlockSpec((1, D_MODEL), lambda i: (0, 0))],
        out_specs=pl.BlockSpec((tm, D_MODEL), lambda i: (i, 0)),
        out_shape=jax.ShapeDtypeStruct((n, D_MODEL), BF16),
        compiler_params=_params("parallel"),
        name="rms_in",
    )(x, g)


def _proj_kernel(h_ref, w_ref, *refs, mode, n_aux):
    aux, outs = refs[:n_aux], refs[n_aux:]
    z = _dot(h_ref[...], w_ref[...])
    if mode == "q":
        outs[0][...] = _seg_rms(z, aux[0][...], aux[1][...], Q_SCALE).astype(BF16)
    elif mode in ("kv", "kv_t"):
        k = _seg_rms(z[:, :512], aux[0][...], aux[1][...], 1.0)
        if mode == "kv_t":
            outs[0][0] = k.T
            tm = z.shape[0]
            for n in range(ATT_KV_HEADS):
                outs[1][pl.ds(n, tm, stride=ATT_KV_HEADS), :] = z[:, 512 + n * LANES:512 + (n + 1) * LANES]
        else:
            outs[0][...] = k
            outs[1][...] = z[:, 512:]
    elif mode == "hq":
        outs[0][...] = (z * (HG_DK ** -0.5)).astype(BF16)
    elif mode == "hf":
        lbp = aux[0][...]
        m = jnp.max(lbp, axis=0, keepdims=True)
        e = jnp.exp(lbp - m)
        lb = e[0:1] / jnp.sum(e, axis=0, keepdims=True)
        sig = jax.nn.sigmoid(z)
        f = lb + (1.0 - lb) * sig
        outs[0][...] = jnp.log(f)
        outs[1][...] = (1.0 - lb) * (1.0 - sig)
    elif mode == "hv":
        outs[0][...] = z.astype(BF16)
    elif mode == "silu":
        outs[0][...] = z * jax.nn.sigmoid(z)
    elif mode == "sigmoid":
        outs[0][...] = jax.nn.sigmoid(z)


def _proj(h, w, col, mode, aux, outs, tm, name):
    n = h.shape[0]
    aux_specs = [pl.BlockSpec(a.shape, lambda i: (0, 0)) for a in aux]
    res = pl.pallas_call(
        functools.partial(_proj_kernel, mode=mode, n_aux=len(aux)),
        grid=(n // tm,),
        in_specs=[pl.BlockSpec((tm, D_MODEL), lambda i: (i, 0)),
                  pl.BlockSpec((D_MODEL, 1024), lambda i, col=col: (0, col))] + aux_specs,
        out_specs=tuple(pl.BlockSpec(o[2], o[3]) for o in outs),
        out_shape=tuple(jax.ShapeDtypeStruct(o[0], o[1]) for o in outs),
        compiler_params=_params("parallel"),
        name=name,
    )(h, w, *aux)
    return res


def _inproj(x, gin, w, qn_row, kn_row, lbp, seg, tm, seq):
    n = x.shape[0]
    h = _rms_cast(x, gin, tm)
    tok = lambda width, dt: ((n, width), dt, (tm, width), lambda i: (i, 0))
    q, = _proj(h, w, 0, "q", [seg, qn_row], [tok(1024, BF16)], tm, "proj_q")
    if seq is None:
        k, v = _proj(h, w, 1, "kv", [seg, kn_row], [tok(512, F32), tok(512, F32)], tm, "proj_kv")
    else:
        b, t = seq
        tpb = t // tm
        k_out = ((b, 512, t), F32, (1, 512, tm), lambda i: (i // tpb, 0, i % tpb))
        v_out = ((n * ATT_KV_HEADS, LANES), F32, (tm * ATT_KV_HEADS, LANES), lambda i: (i, 0))
        k, v = _proj(h, w, 1, "kv_t", [seg, kn_row], [k_out, v_out], tm, "proj_kv")
    hq, = _proj(h, w, 2, "hq", [], [tok(1024, BF16)], tm, "proj_hq")
    logf, kk = _proj(h, w, 3, "hf", [lbp], [tok(1024, F32), tok(1024, F32)], tm, "proj_hf")
    hv, = _proj(h, w, 4, "hv", [], [tok(1024, BF16)], tm, "proj_hv")
    og, = _proj(h, w, 5, "silu", [], [tok(1024, F32)], tm, "proj_og")
    sga, = _proj(h, w, 6, "sigmoid", [], [tok(1024, F32)], tm, "proj_ga")
    sgh, = _proj(h, w, 7, "sigmoid", [], [tok(1024, F32)], tm, "proj_gh")
    return q, k, v, hq, logf, kk, hv, og, sga, sgh


def _diff_lambda(lp, lam_init):
    a = jnp.sum(lp[0:1] * lp[1:2], axis=-1, keepdims=True)
    b = jnp.sum(lp[2:3] * lp[3:4], axis=-1, keepdims=True)
    return jnp.exp(a) - jnp.exp(b) + lam_init


def _sub_norm(o, sub_row, lam_init):
    ms = jnp.mean(o * o, axis=-1, keepdims=True)
    return o * lax.rsqrt(ms + EPS) * sub_row * (1.0 - lam_init)


def _attn_kernel(q_ref, kt_ref, v_ref, lamp_ref, sub_ref, o_ref,
                 kb_scr, vb_scr, qs_scr, m_scr, acc_scr, *, tq, lam_init):
    qi = pl.program_id(2)

    @pl.when(qi == 0)
    def _():
        kb_scr[...] = kt_ref[0].astype(BF16)
        t_len = vb_scr.shape[0]
        vb_scr[:, 0:LANES] = v_ref[0, pl.ds(pl.program_id(1), t_len, stride=ATT_KV_HEADS), :].astype(BF16)
        vb_scr[:, LANES:2 * LANES] = jnp.ones((vb_scr.shape[0], LANES), BF16)

    q = q_ref[0].astype(F32)
    lane = lax.broadcasted_iota(jnp.int32, (tq, LANES), 1)
    for g in range(2):
        qg = q[:, g * LANES:(g + 1) * LANES]
        for c in range(2):
            keep = (lane < ATT_DHALF) if c == 0 else (lane >= ATT_DHALF)
            r = g * 2 + c
            qs_scr[r * tq:(r + 1) * tq, :] = jnp.where(keep, qg, 0.0).astype(BF16)
    def chunk(j, masked, first):
        off = pl.multiple_of(j * tq, tq)
        s = _dot(qs_scr[...], kb_scr[:, pl.ds(off, tq)])
        if masked:
            row = lax.rem(lax.broadcasted_iota(jnp.int32, s.shape, 0), tq)
            s = jnp.where(lax.broadcasted_iota(jnp.int32, s.shape, 1) <= row, s, -jnp.inf)
        s_max = jnp.max(s, axis=-1, keepdims=True)
        if first:
            m_new = jnp.broadcast_to(s_max, m_scr.shape)
        else:
            m_prev = m_scr[...]
            m_new = jnp.maximum(m_prev, s_max)
            alpha = jnp.exp2(m_prev - m_new)
        pr = jnp.exp2(s - jnp.concatenate([m_new] * (tq // LANES), axis=1))
        pv = _dot(pr.astype(BF16), vb_scr[pl.ds(off, tq), :])
        acc_scr[...] = pv if first else jnp.concatenate([alpha, alpha], axis=1) * acc_scr[...] + pv
        m_scr[...] = m_new

    def body(j, carry):
        chunk(j, False, False)
        return carry

    @pl.when(qi == 0)
    def _():
        chunk(0, True, True)

    @pl.when(qi > 0)
    def _():
        chunk(0, False, True)
        lax.fori_loop(1, qi, body, 0)
        chunk(qi, True, False)

    lam = _diff_lambda(lamp_ref[...], lam_init)
    acc = acc_scr[...]
    out = acc[:, 0:LANES] / acc[:, LANES:2 * LANES]
    for g in range(2):
        o0 = out[(2 * g) * tq:(2 * g + 1) * tq]
        o1 = out[(2 * g + 1) * tq:(2 * g + 2) * tq]
        o_ref[0, :, g * LANES:(g + 1) * LANES] = _sub_norm(o0 - lam * o1, sub_ref[...], lam_init).astype(BF16)


def _attn_prompt(q, kt, v, lam_p, sub_row, lam_init, tq):
    b, t, _ = q.shape
    return pl.pallas_call(
        functools.partial(_attn_kernel, tq=tq, lam_init=lam_init),
        grid=(b, ATT_KV_HEADS, t // tq),
        in_specs=[
            pl.BlockSpec((1, tq, 256), lambda bi, n, qi: (bi, qi, n)),
            pl.BlockSpec((1, LANES, t), lambda bi, n, qi: (bi, n, 0)),
            pl.BlockSpec((1, t * ATT_KV_HEADS, LANES), lambda bi, n, qi: (bi, 0, 0)),
            pl.BlockSpec((4, ATT_DHALF), lambda bi, n, qi: (0, 0)),
            pl.BlockSpec((1, LANES), lambda bi, n, qi: (0, 0)),
        ],
        out_specs=pl.BlockSpec((1, tq, 256), lambda bi, n, qi: (bi, qi, n)),
        out_shape=jax.ShapeDtypeStruct((b, t, 1024), BF16),
        scratch_shapes=[pltpu.VMEM((LANES, t), BF16), pltpu.VMEM((t, 2 * LANES), BF16),
                        pltpu.VMEM((4 * tq, LANES), BF16), pltpu.VMEM((4 * tq, LANES), F32),
                        pltpu.VMEM((4 * tq, 2 * LANES), F32)],
        compiler_params=_params("parallel", "parallel", "arbitrary"),
        name="attn_prompt",
    )(q, kt, v, lam_p, sub_row)


def _hgrn_tables(c, levels):
    t = np.arange(c)[:, None]
    u = np.arange(c)[None, :]
    w = [(u <= t), (u > t)]
    masks = [(t == u)]
    for l in range(1, levels + 1):
        n = 1 << l
        half = n // 2
        mid = (t // n) * n + half
        hi = t >= mid
        w.append(np.where(hi, (u > mid) & (u <= t), (u > t) & (u <= mid)))
        masks.append((t // n == u // n) & (t % n >= half) & (u % n < half))
    w = np.concatenate(w, axis=0).astype(np.float32)
    w3 = np.concatenate([w, w, w], axis=1)
    if len(masks) % 2:
        masks.append(np.zeros_like(masks[0]))
    pairs = [np.concatenate([masks[i], masks[i + 1]], axis=1) for i in range(0, len(masks), 2)]
    return w3, np.stack(pairs).astype(np.float32)


def _hgrn_kernel(q_ref, g_ref, k_ref, v_ref, og_ref, norm_ref, w_ref, mask_ref, o_ref, s_ref,
                 st_scr, *, c, levels, n_chunks):
    ti = pl.program_id(2)

    @pl.when(ti == 0)
    def _():
        st_scr[...] = jnp.zeros(st_scr.shape, F32)

    w = w_ref[...]
    zeros = jnp.zeros((c, LANES), BF16)
    chunk_rows = [slice(ci * c, (ci + 1) * c) for ci in range(n_chunks)]
    g3 = jnp.concatenate([jnp.concatenate(_split3(g_ref[0, rows, :]), axis=0) for rows in chunk_rows], axis=1)
    e_all = jnp.exp(_dot(w, g3))

    parts = []
    for ci, rows in enumerate(chunk_rows):
        q = q_ref[0, rows, :].astype(F32)
        k = k_ref[0, rows, :]
        v = v_ref[0, rows, :]
        e = e_all[:, ci * LANES:(ci + 1) * LANES]
        e_b = e[0:c]
        e_k = e[c:2 * c]
        qs = [q.astype(BF16)]
        ks = [k.astype(BF16)]
        for l in range(1, levels + 1):
            e_l = e[(l + 1) * c:(l + 2) * c]
            qs.append((q * e_l).astype(BF16))
            ks.append((k * e_l).astype(BF16))
        if len(qs) % 2:
            qs.append(zeros)
            ks.append(zeros)
        a2 = None
        for p in range(len(qs) // 2):
            ql = jnp.concatenate([qs[2 * p], qs[2 * p + 1]], axis=1)
            kbd = jnp.concatenate([jnp.concatenate([ks[2 * p], zeros], axis=1),
                                   jnp.concatenate([zeros, ks[2 * p + 1]], axis=1)], axis=0)
            term = mask_ref[p] * _dot_nt(ql, kbd)
            a2 = term if a2 is None else a2 + term
        o_intra = _dot(a2.astype(BF16), jnp.concatenate([v, v], axis=0))
        st_add = _dot_tn(v, (k * e_k).astype(BF16))
        parts.append(((q * e_b).astype(BF16), e_b[c - 1:c, :], o_intra, st_add))

    st = st_scr[...]
    for rows, (q_b, decay, o_intra, st_add) in zip(chunk_rows, parts):
        o = _dot_nt(q_b, st.astype(BF16)) + o_intra
        st = st * decay + st_add
        ms = jnp.mean(o * o, axis=-1, keepdims=True)
        y = o * lax.rsqrt(ms + EPS) * norm_ref[...] * og_ref[0, rows, :]
        o_ref[0, rows, :] = y.astype(BF16)
    st_scr[...] = st

    @pl.when(ti == pl.num_programs(2) - 1)
    def _():
        s_ref[0, 0] = st.T


def _hgrn_prompt(hq, logf, kk, hv, og, norm_row, ct):
    b, t, _ = hq.shape
    c, levels = HG_CHUNK, HG_LEVELS
    w_np, m_np = _hgrn_tables(c, levels)
    w = jnp.asarray(w_np, BF16)
    masks = jnp.asarray(m_np, F32)
    tok = pl.BlockSpec((1, ct, LANES), lambda bi, h, ti: (bi, ti, h))
    return pl.pallas_call(
        functools.partial(_hgrn_kernel, c=c, levels=levels, n_chunks=ct // c),
        grid=(b, HG_HEADS, t // ct),
        in_specs=[tok, tok, tok, tok, tok,
                  pl.BlockSpec((1, LANES), lambda bi, h, ti: (0, 0)),
                  pl.BlockSpec(w.shape, lambda bi, h, ti: (0, 0)),
                  pl.BlockSpec(masks.shape, lambda bi, h, ti: (0, 0, 0))],
        out_specs=(tok, pl.BlockSpec((1, 1, HG_DK, HG_DV), lambda bi, h, ti: (bi, h, 0, 0))),
        out_shape=(jax.ShapeDtypeStruct((b, t, 1024), BF16),
                   jax.ShapeDtypeStruct((b, HG_HEADS, HG_DK, HG_DV), F32)),
        scratch_shapes=[pltpu.VMEM((HG_DV, HG_DK), F32)],
        compiler_params=_params("parallel", "parallel", "arbitrary"),
        name="hgrn_prompt",
    )(hq, logf, kk, hv, og, norm_row, w, masks)


def _dec_attn_kernel(pt_ref, wq_ref, kn_ref, vn_ref, lamp_ref, sub_ref, *refs, pages, lam_init):
    k_refs = refs[:pages]
    v_refs = refs[pages:2 * pages]
    o_ref = refs[2 * pages]
    m_scr, l_scr, acc_scr = refs[2 * pages + 1:]
    j = pl.program_id(1)
    page = k_refs[0].shape[2]

    @pl.when(j == 0)
    def _():
        m_scr[...] = jnp.full(m_scr.shape, -jnp.inf, F32)
        l_scr[...] = jnp.zeros(l_scr.shape, F32)
        acc_scr[...] = jnp.zeros(acc_scr.shape, F32)

    rown = lax.rem(lax.broadcasted_iota(jnp.int32, (16, LANES), 0), 8) // 2
    wq = wq_ref[0]
    s = jnp.concatenate([_dot(wq, k_refs[i][0].astype(BF16)) for i in range(pages)], axis=1)
    m_prev = m_scr[...]
    m_new = jnp.maximum(m_prev, jnp.max(s, axis=-1, keepdims=True))
    alpha = jnp.exp2(m_prev - m_new)
    pr = jnp.exp2(s - m_new)
    l_scr[...] = alpha * l_scr[...] + jnp.sum(pr, axis=-1, keepdims=True)
    prb = pr.astype(BF16)
    pv = jnp.zeros((16, LANES), F32)
    for n in range(ATT_KV_HEADS):
        v_n = jnp.concatenate([v_refs[i][0, pl.ds(n, page, stride=ATT_KV_HEADS), :] for i in range(pages)],
                              axis=0).astype(BF16)
        pv = pv + jnp.where(rown == n, _dot(prb, v_n), 0.0)
    acc_scr[...] = alpha * acc_scr[...] + pv
    m_scr[...] = m_new

    @pl.when(j == pl.num_programs(1) - 1)
    def _():
        s_cur = jnp.sum(wq.astype(F32) * kn_ref[0], axis=-1, keepdims=True)
        m_old = m_scr[...]
        m_fin = jnp.maximum(m_old, s_cur)
        al = jnp.exp2(m_old - m_fin)
        pc = jnp.exp2(s_cur - m_fin)
        l_fin = al * l_scr[...] + pc
        vn = vn_ref[0]
        v_cur = jnp.zeros((16, LANES), F32)
        for n in range(ATT_KV_HEADS):
            v_cur = v_cur + jnp.where(rown == n, vn[:, n * LANES:(n + 1) * LANES], 0.0)
        out = (al * acc_scr[...] + pc * v_cur) / l_fin
        lam = _diff_lambda(lamp_ref[...], lam_init)
        o = out[0:8] - lam * out[8:16]
        o_ref[0] = _sub_norm(o, sub_ref[...], lam_init)


def _attn_sample(wq, k_new, v_new, kt_pages, v_pages, page_table, lam_p, sub_row, lam_init, pages):
    db, n_pages = page_table.shape
    pt_flat = page_table.reshape(-1)
    blk = kt_pages.shape[1:]

    def page_spec(i):
        return pl.BlockSpec((1,) + blk, lambda bi, j, pt, i=i: (pt[bi * n_pages + j * pages + i], 0, 0))

    row3 = lambda w: pl.BlockSpec((1, 1, w), lambda bi, j, pt: (bi, 0, 0))
    grid_spec = pltpu.PrefetchScalarGridSpec(
        num_scalar_prefetch=1,
        grid=(db, n_pages // pages),
        in_specs=[pl.BlockSpec((1, 16, 512), lambda bi, j, pt: (bi, 0, 0)), row3(512), row3(512),
                  pl.BlockSpec((4, ATT_DHALF), lambda bi, j, pt: (0, 0)),
                  pl.BlockSpec((1, LANES), lambda bi, j, pt: (0, 0))]
                 + [page_spec(i) for i in range(pages)] + [page_spec(i) for i in range(pages)],
        out_specs=pl.BlockSpec((1, 8, LANES), lambda bi, j, pt: (bi, 0, 0)),
        scratch_shapes=[pltpu.VMEM((16, 1), F32), pltpu.VMEM((16, 1), F32), pltpu.VMEM((16, LANES), F32)],
    )
    return pl.pallas_call(
        functools.partial(_dec_attn_kernel, pages=pages, lam_init=lam_init),
        grid_spec=grid_spec,
        out_shape=jax.ShapeDtypeStruct((db, 8, LANES), F32),
        compiler_params=_params("parallel", "arbitrary"),
        name="attn_sample",
    )(pt_flat, wq, k_new, v_new, lam_p, sub_row, *([kt_pages] * pages), *([v_pages] * pages))


def _hgrn_step_kernel(s_ref, q_ref, g_ref, k_ref, v_ref, og_ref, norm_ref, o_ref, sn_ref):
    pad = jnp.zeros((LANES - 3 * HG_HEADS, LANES), F32)
    for bi in range(s_ref.shape[0]):
        cols = jnp.concatenate([q_ref[bi], g_ref[bi], k_ref[bi], pad], axis=0).T
        v_all = v_ref[bi]
        og_all = og_ref[bi]
        outs = []
        for h in range(HG_HEADS):
            s = s_ref[bi, h]
            qc = cols[:, h:h + 1]
            f = jnp.exp(cols[:, HG_HEADS + h:HG_HEADS + h + 1])
            kc = cols[:, 2 * HG_HEADS + h:2 * HG_HEADS + h + 1]
            v = v_all[h:h + 1]
            o = jnp.sum(s * (qc * f), axis=0, keepdims=True) + jnp.sum(qc * kc, axis=0, keepdims=True) * v
            sn_ref[bi, h] = s * f + kc * v
            ms = jnp.mean(o * o, axis=-1, keepdims=True)
            outs.append(o * lax.rsqrt(ms + EPS) * norm_ref[...] * og_all[h:h + 1])
        o_ref[bi] = jnp.concatenate(outs, axis=0)


def _hgrn_sample(state, q, g, k, v, og, norm_row):
    db = state.shape[0]
    nb = _pick(db, 4)
    row = pl.BlockSpec((nb, HG_HEADS, HG_DK), lambda bi: (bi, 0, 0))
    st = pl.BlockSpec((nb, HG_HEADS, HG_DK, HG_DV), lambda bi: (bi, 0, 0, 0))
    return pl.pallas_call(
        _hgrn_step_kernel,
        grid=(db // nb,),
        in_specs=[st, row, row, row, row, row, pl.BlockSpec((1, LANES), lambda bi: (0, 0))],
        out_specs=(row, st),
        out_shape=(jax.ShapeDtypeStruct((db, HG_HEADS, HG_DV), F32),
                   jax.ShapeDtypeStruct(state.shape, F32)),
        compiler_params=_params("parallel"),
        name="hgrn_sample",
    )(state, q, g, k, v, og, norm_row)


def _lane_first(cond, lane):
    return jnp.min(jnp.where(cond, lane, LANES), axis=-1, keepdims=True)


def _route(logits):
    lane = lax.broadcasted_iota(jnp.int32, logits.shape, 1)
    gmask = (lane >= N_EXPERTS) & (lane < N_EXPERTS + N_GROUPS)
    lg = jnp.where(gmask, logits, -jnp.inf)
    mg = jnp.max(lg, axis=-1, keepdims=True)
    eg = jnp.exp(lg - mg)
    pg = eg / jnp.sum(eg, axis=-1, keepdims=True)
    pg_top = jnp.max(pg, axis=-1, keepdims=True)
    gsel = _lane_first(gmask & (pg == pg_top), lane) - N_EXPERTS
    emask = (lane >= gsel * EXP_PER_GROUP) & (lane < (gsel + 1) * EXP_PER_GROUP)
    le = jnp.where(emask, logits, -jnp.inf)
    me = jnp.max(le, axis=-1, keepdims=True)
    ee = jnp.exp(le - me)
    pe = ee / jnp.sum(ee, axis=-1, keepdims=True)
    pe = jnp.where(emask, pe, -1.0)
    p1 = jnp.max(pe, axis=-1, keepdims=True)
    e1 = _lane_first(pe == p1, lane)
    pe2 = jnp.where(lane == e1, -1.0, pe)
    p2 = jnp.max(pe2, axis=-1, keepdims=True)
    e2 = _lane_first(pe2 == p2, lane)
    den = p1 + p2
    w1 = p1 / den * pg_top
    w2 = p2 / den * pg_top
    comb = jnp.where(lane == e1, w1, 0.0) + jnp.where(lane == e2, w2, 0.0)
    picks = jnp.where(lane == 0, e1.astype(F32), jnp.where(lane == 1, e2.astype(F32),
                      jnp.where(lane == 2, w1, jnp.where(lane == 3, w2, 0.0))))
    return comb, picks


ROW_TILES = D_MODEL // LANES


def _merge_kernel(oa_ref, oh_ref, sga_ref, sgh_ref, x_ref, wpa_ref, wph_ref, wo_ref, gf_ref,
                  wr_ref, br_ref, x1_ref, xn_ref, route_ref, *, sparse):
    ya = _dot(oa_ref[...], wpa_ref[...])
    yh = _dot(oh_ref[...], wph_ref[...])
    mixed = (sga_ref[...] * ya + sgh_ref[...] * yh).astype(BF16)
    x1 = x_ref[...] + _dot(mixed, wo_ref[...])
    x1_ref[...] = x1
    ms = jnp.mean(x1 * x1, axis=-1, keepdims=True)
    xn_f = x1 * lax.rsqrt(ms + EPS) * gf_ref[...]
    xn = xn_f.astype(BF16)
    comb, picks = _route(_dot(xn, wr_ref[...]) + br_ref[...])
    if sparse:
        tm = x1.shape[0]
        for s in range(ROW_TILES):
            xn_ref[pl.ds(s, tm, stride=ROW_TILES), :] = xn_f[:, s * LANES:(s + 1) * LANES]
        route_ref[...] = picks
    else:
        xn_ref[...] = xn
        route_ref[...] = comb


def _merge(oa, oh, sga, sgh, x, wpa, wph, wo, gf, wr, br, tm, sparse):
    n = x.shape[0]
    tok = pl.BlockSpec((tm, 1024), lambda i: (i, 0))
    wsp = pl.BlockSpec((1024, 1024), lambda i: (0, 0))
    if sparse:
        xn_spec = pl.BlockSpec((tm * ROW_TILES, LANES), lambda i: (i, 0))
        xn_shape = jax.ShapeDtypeStruct((n * ROW_TILES, LANES), F32)
    else:
        xn_spec, xn_shape = tok, jax.ShapeDtypeStruct((n, 1024), BF16)
    return pl.pallas_call(
        functools.partial(_merge_kernel, sparse=sparse),
        grid=(n // tm,),
        in_specs=[tok, tok, tok, tok, tok, wsp, wsp, wsp,
                  pl.BlockSpec((1, 1024), lambda i: (0, 0)),
                  pl.BlockSpec((1024, LANES), lambda i: (0, 0)),
                  pl.BlockSpec((1, LANES), lambda i: (0, 0))],
        out_specs=(tok, xn_spec, pl.BlockSpec((tm, LANES), lambda i: (i, 0))),
        out_shape=(jax.ShapeDtypeStruct((n, 1024), F32), xn_shape,
                   jax.ShapeDtypeStruct((n, LANES), F32)),
        compiler_params=_params("parallel"),
        name="merge",
    )(oa, oh, sga, sgh, x, wpa, wph, wo, gf, wr, br)


def _moe_kernel(xn_ref, comb_ref, x1_ref, wg_ref, wu_ref, wd_ref, y_ref, acc_scr):
    e = pl.program_id(1)

    @pl.when(e == 0)
    def _():
        acc_scr[...] = jnp.zeros(acc_scr.shape, F32)

    comb = comb_ref[...]
    lane = lax.broadcasted_iota(jnp.int32, comb.shape, 1)
    c = jnp.sum(jnp.where(lane == e, comb, 0.0), axis=-1, keepdims=True)
    xn = xn_ref[...]
    hg = _dot(xn, wg_ref[0])
    hu = _dot(xn, wu_ref[0])
    hid = hg * jax.nn.sigmoid(hg) * hu
    acc_scr[...] += _dot((hid * c).astype(BF16), wd_ref[0])

    @pl.when(e == pl.num_programs(1) - 1)
    def _():
        y_ref[...] = x1_ref[...] + acc_scr[...]


def _moe(xn, comb, x1, wg, wu, wd, tm):
    n = xn.shape[0]
    return pl.pallas_call(
        _moe_kernel,
        grid=(n // tm, N_EXPERTS),
        in_specs=[pl.BlockSpec((tm, 1024), lambda i, e: (i, 0)),
                  pl.BlockSpec((tm, LANES), lambda i, e: (i, 0)),
                  pl.BlockSpec((tm, 1024), lambda i, e: (i, 0)),
                  pl.BlockSpec((1, 1024, D_FF), lambda i, e: (e, 0, 0)),
                  pl.BlockSpec((1, 1024, D_FF), lambda i, e: (e, 0, 0)),
                  pl.BlockSpec((1, D_FF, 1024), lambda i, e: (e, 0, 0))],
        out_specs=pl.BlockSpec((tm, 1024), lambda i, e: (i, 0)),
        out_shape=jax.ShapeDtypeStruct((n, 1024), F32),
        scratch_shapes=[pltpu.VMEM((tm, 1024), F32)],
        compiler_params=_params("parallel", "arbitrary"),
        name="moe",
    )(xn, comb, x1, wg, wu, wd)


def _row_copy(src_hbm, src_row, dst, dst_row, sem):
    return pltpu.make_async_copy(src_hbm.at[pl.ds(src_row * ROW_TILES, ROW_TILES)],
                                 dst.at[pl.ds(dst_row * ROW_TILES, ROW_TILES)], sem)


def _wait_rows(src_hbm, dst, sem):
    pltpu.make_async_copy(src_hbm.at[pl.ds(0, dst.shape[0])], dst, sem).wait()


def _rows_to_mat(buf, n_rows):
    return jnp.concatenate([buf[pl.ds(s, n_rows, stride=ROW_TILES), :] for s in range(ROW_TILES)], axis=1)


def _moe_scatter_kernel(p1_ref, p2_ref, xn_ref, xs_in, xs_hbm, sem):
    del xs_in
    i = pl.program_id(0)
    tm = p1_ref.shape[0]

    def wait_tile():
        for _ in range(2):
            pltpu.make_async_copy(xn_ref, xs_hbm.at[pl.ds(0, tm * ROW_TILES)], sem.at[0]).wait()

    def body(r, carry):
        src = xn_ref.at[pl.ds(r * ROW_TILES, ROW_TILES)]
        pltpu.make_async_copy(src, xs_hbm.at[pl.ds(p1_ref[r] * ROW_TILES, ROW_TILES)], sem.at[0]).start(priority=0)
        pltpu.make_async_copy(src, xs_hbm.at[pl.ds(p2_ref[r] * ROW_TILES, ROW_TILES)], sem.at[0]).start(priority=1)
        return carry

    lax.fori_loop(0, tm, body, 0)
    wait_tile()


def _moe_scatter(xg, pos1, pos2, rows, tm):
    n = pos1.shape[0]
    idx = pl.BlockSpec((tm,), lambda i: (i,), memory_space=pltpu.SMEM)
    xs0 = jnp.zeros((rows * ROW_TILES, LANES), F32)
    return pl.pallas_call(
        _moe_scatter_kernel,
        grid=(n // tm,),
        in_specs=[idx, idx, pl.BlockSpec((tm * ROW_TILES, LANES), lambda i: (i, 0)),
                  pl.BlockSpec(memory_space=pl.ANY)],
        out_specs=pl.BlockSpec(memory_space=pl.ANY),
        out_shape=jax.ShapeDtypeStruct(xs0.shape, F32),
        scratch_shapes=[pltpu.SemaphoreType.DMA((1,))],
        input_output_aliases={3: 0},
        compiler_params=_params("arbitrary"),
        name="moe_scatter",
    )(pos1, pos2, xg, xs0)


def _moe_ffn_kernel(te_ref, xs_ref, wg_ref, wu_ref, wd_ref, ys_ref, *, tp):
    del te_ref
    x = _rows_to_mat(xs_ref, tp).astype(BF16)
    hg = _dot(x, wg_ref[0])
    hu = _dot(x, wu_ref[0])
    hid = (hg * jax.nn.sigmoid(hg) * hu).astype(BF16)
    out = _dot(hid, wd_ref[0])
    for s in range(ROW_TILES):
        ys_ref[pl.ds(s, tp, stride=ROW_TILES), :] = out[:, s * LANES:(s + 1) * LANES]


def _moe_ffn(xs, tile_expert, wg, wu, wd, tp):
    n_tiles = tile_expert.shape[0]
    rows = pl.BlockSpec((tp * ROW_TILES, LANES), lambda i, te: (i, 0))
    wsp = lambda a, b: pl.BlockSpec((1, a, b), lambda i, te: (te[i], 0, 0))
    grid_spec = pltpu.PrefetchScalarGridSpec(
        num_scalar_prefetch=1,
        grid=(n_tiles,),
        in_specs=[rows, wsp(1024, D_FF), wsp(1024, D_FF), wsp(D_FF, 1024)],
        out_specs=rows,
    )
    return pl.pallas_call(
        functools.partial(_moe_ffn_kernel, tp=tp),
        grid_spec=grid_spec,
        out_shape=jax.ShapeDtypeStruct(xs.shape, F32),
        compiler_params=_params("parallel"),
        name="moe_ffn",
    )(tile_expert, xs, wg, wu, wd)


def _moe_combine_kernel(p1_ref, p1n_ref, p2_ref, p2n_ref, ys_hbm, x1_ref, w1_ref, w2_ref, y_ref, buf, sem, *, tm):
    i = pl.program_id(0)
    slot = lax.rem(i, 2)

    def gather(a_ref, b_ref, s):
        def body(r, carry):
            _row_copy(ys_hbm, a_ref[r], buf.at[s, 0], r, sem.at[s, 0]).start(priority=0)
            _row_copy(ys_hbm, b_ref[r], buf.at[s, 1], r, sem.at[s, 1]).start(priority=1)
            return carry
        lax.fori_loop(0, tm, body, 0)

    @pl.when(i == 0)
    def _():
        gather(p1_ref, p2_ref, 0)

    @pl.when(i + 1 < pl.num_programs(0))
    def _():
        gather(p1n_ref, p2n_ref, 1 - slot)

    _wait_rows(ys_hbm, buf.at[slot, 0], sem.at[slot, 0])
    _wait_rows(ys_hbm, buf.at[slot, 1], sem.at[slot, 1])
    y_ref[...] = x1_ref[...] + (w1_ref[...] * _rows_to_mat(buf.at[slot, 0], tm)
                                + w2_ref[...] * _rows_to_mat(buf.at[slot, 1], tm))


def _moe_combine(ys, pos1, pos2, w1, w2, x1, tm):
    n = x1.shape[0]
    n_tiles = n // tm
    idx = lambda off: pl.BlockSpec((tm,), lambda i: (jnp.minimum(i + off, n_tiles - 1),),
                                   memory_space=pltpu.SMEM)
    col = pl.BlockSpec((tm, 1), lambda i: (i, 0))
    return pl.pallas_call(
        functools.partial(_moe_combine_kernel, tm=tm),
        grid=(n_tiles,),
        in_specs=[idx(0), idx(1), idx(0), idx(1), pl.BlockSpec(memory_space=pl.ANY),
                  pl.BlockSpec((tm, 1024), lambda i: (i, 0)), col, col],
        out_specs=pl.BlockSpec((tm, 1024), lambda i: (i, 0)),
        out_shape=jax.ShapeDtypeStruct((n, 1024), F32),
        scratch_shapes=[pltpu.VMEM((2, 2, tm * ROW_TILES, LANES), F32), pltpu.SemaphoreType.DMA((2, 2))],
        compiler_params=_params("arbitrary"),
        name="moe_combine",
    )(pos1, pos1, pos2, pos2, ys, x1, w1, w2)


def _rank_kernel(picks_ref, tri_ref, rank_ref, cnt_ref, base_scr):
    @pl.when(pl.program_id(0) == 0)
    def _():
        base_scr[...] = jnp.zeros(base_scr.shape, F32)

    picks = picks_ref[...]
    lane = lax.broadcasted_iota(jnp.int32, picks.shape, 1)
    chosen = (lane == picks[:, 0:1].astype(jnp.int32)) | (lane == picks[:, 1:2].astype(jnp.int32))
    both = jnp.where(chosen, 1.0, 0.0)
    before = _dot(tri_ref[...], both.astype(BF16)) + base_scr[...]
    rank_ref[...] = jnp.where(chosen, before, 0.0)
    base_scr[...] += jnp.sum(both, axis=0, keepdims=True)
    cnt_ref[...] = base_scr[...]


def _pos_kernel(picks_ref, rank_ref, start_ref, pos_ref):
    picks = picks_ref[...]
    lane = lax.broadcasted_iota(jnp.int32, picks.shape, 1)
    where_to = rank_ref[...] + start_ref[...]
    pos = [jnp.sum(jnp.where(lane == picks[:, p:p + 1].astype(jnp.int32), where_to, 0.0), axis=-1, keepdims=True)
           for p in range(2)]
    pos_ref[...] = jnp.where(lane == 0, pos[0], jnp.where(lane == 1, pos[1], 0.0))


def _dispatch(picks, tp, tm):
    n = picks.shape[0]
    tri = jnp.asarray(np.tril(np.ones((tm, tm), np.float32), -1), BF16)
    tok = pl.BlockSpec((tm, LANES), lambda i: (i, 0))
    one = pl.BlockSpec((1, LANES), lambda i: (0, 0))
    rank, cnt = pl.pallas_call(
        _rank_kernel,
        grid=(n // tm,),
        in_specs=[tok, pl.BlockSpec((tm, tm), lambda i: (0, 0))],
        out_specs=(tok, one),
        out_shape=(jax.ShapeDtypeStruct((n, LANES), F32), jax.ShapeDtypeStruct((1, LANES), F32)),
        scratch_shapes=[pltpu.VMEM((1, LANES), F32)],
        compiler_params=_params("arbitrary"),
        name="moe_rank",
    )(picks, tri)
    counts = cnt[0, :N_EXPERTS].astype(jnp.int32)
    tiles = (counts + tp - 1) // tp
    tile_end = jnp.cumsum(tiles)
    row_start = jnp.zeros((1, LANES), F32).at[0, :N_EXPERTS].set(((tile_end - tiles) * tp).astype(F32))
    tp_pos = _pick(n, 4 * tm)
    tok_pos = pl.BlockSpec((tp_pos, LANES), lambda i: (i, 0))
    pos = pl.pallas_call(
        _pos_kernel,
        grid=(n // tp_pos,),
        in_specs=[tok_pos, tok_pos, one],
        out_specs=tok_pos,
        out_shape=jax.ShapeDtypeStruct((n, LANES), F32),
        compiler_params=_params("parallel"),
        name="moe_pos",
    )(picks, rank, row_start)
    n_tiles = (2 * n) // tp + N_EXPERTS
    tile_ids = jnp.arange(n_tiles, dtype=jnp.int32)
    tile_expert = jnp.minimum(jnp.sum((tile_end[None, :] <= tile_ids[:, None]).astype(jnp.int32), axis=1),
                              N_EXPERTS - 1)
    return pos[:, 0].astype(jnp.int32), pos[:, 1].astype(jnp.int32), tile_expert, n_tiles * tp


def _pick(n, pref):
    t = min(n, pref)
    while n % t:
        t //= 2
    return t


def kernel(x_prompt, x_sample, cache_k, cache_v, state_hgrn, page_table, rms_in, w_in, att_q_norm, att_k_norm, att_lambda, att_sub_norm, hg_lower_bound, hg_out_norm, w_branch_att, w_branch_hg, w_out, rms_ffn, w_router_group, b_router_group, w_router_expert, b_router_expert, w_exp_gate, w_exp_up, w_exp_down):
    depth = rms_in.shape[0]
    assert depth == 1 and hg_lower_bound.shape[0] == 2
    b, t, _ = x_prompt.shape
    db, ds, _ = x_sample.shape
    assert ds == 1
    lam_init = 0.8 - 0.6 * math.exp(-0.3 * 0)

    w_in_bf = w_in[0].astype(BF16)
    qn_row = jnp.tile(att_q_norm[0], 2).reshape(1, LANES)
    kn_row = jnp.tile(att_k_norm[0], 2).reshape(1, LANES)
    seg_np = (np.arange(LANES)[:, None] // ATT_DHALF) == (np.arange(LANES)[None, :] // ATT_DHALF)
    seg = jnp.asarray(seg_np.astype(np.float32), BF16)
    sub_row = att_sub_norm[0].reshape(1, LANES)
    hgn_row = hg_out_norm[0].reshape(1, LANES)
    wpa = w_branch_att[0].astype(BF16)
    wph = w_branch_hg[0].astype(BF16)
    wo = w_out[0].astype(BF16)
    wr = jnp.zeros((D_MODEL, LANES), F32)
    wr = wr.at[:, :N_EXPERTS].set(w_router_expert[0]).at[:, N_EXPERTS:N_EXPERTS + N_GROUPS].set(w_router_group[0])
    wr = wr.astype(BF16)
    br = jnp.zeros((1, LANES), F32)
    br = br.at[0, :N_EXPERTS].set(b_router_expert[0]).at[0, N_EXPERTS:N_EXPERTS + N_GROUPS].set(b_router_group[0])
    weg = w_exp_gate[0].astype(BF16)
    weu = w_exp_up[0].astype(BF16)
    wed = w_exp_down[0].astype(BF16)
    gin = rms_in[0].reshape(1, D_MODEL)
    gffn = rms_ffn[0].reshape(1, D_MODEL)
    lam_p = att_lambda[0]

    def tail(x2, oa, oh, sga, sgh, tm_merge, tm_moe):
        sparse = x2.shape[0] >= 16 * MOE_TILE
        x1, xn, route = _merge(oa, oh, sga, sgh, x2, wpa, wph, wo, gffn, wr, br, tm_merge, sparse)
        if not sparse:
            return _moe(xn, route, x1, weg, weu, wed, tm_moe)
        pos1, pos2, tile_expert, rows = _dispatch(route, MOE_TILE, _pick(x2.shape[0], 1024))
        xs = _moe_scatter(xn, pos1, pos2, rows, _pick(x2.shape[0], 512))
        ys = _moe_ffn(xs, tile_expert, weg, weu, wed, MOE_TILE)
        return _moe_combine(ys, pos1, pos2, route[:, 2:3], route[:, 3:4], x1, _pick(x2.shape[0], 512))

    n = b * t
    xp = x_prompt.reshape(n, D_MODEL)
    q, kt, v, hq, logf, kk, hv, og, sga, sgh = _inproj(
        xp, gin, w_in_bf, qn_row, kn_row, hg_lower_bound, seg, _pick(t, 1024), (b, t))
    oa = _attn_prompt(q.reshape(b, t, 1024), kt, v.reshape(b, t * ATT_KV_HEADS, LANES), lam_p, sub_row, lam_init,
                      _pick(t, 512))
    r3 = lambda a: a.reshape(b, t, 1024)
    oh, state_p = _hgrn_prompt(r3(hq), r3(logf), r3(kk), r3(hv), r3(og), hgn_row, _pick(t, 1024))
    yp = tail(xp, oa.reshape(n, 1024), oh.reshape(n, 1024), sga, sgh, _pick(n, 512), _pick(n, 1024))
    k_prompt = jnp.transpose(kt.reshape(1, b, ATT_KV_HEADS, 2, ATT_DHALF, t), (0, 1, 5, 2, 3, 4))

    xs = x_sample.reshape(db, D_MODEL)
    q, ks, vs, hq, logf, kk, hv, og, sga, sgh = _inproj(
        xs, gin, w_in_bf, qn_row, kn_row, hg_lower_bound, seg, _pick(db, 1024), None)
    q5 = q.reshape(db, ATT_KV_HEADS, 2, 2, ATT_DHALF)
    eye_n = jnp.eye(ATT_KV_HEADS, dtype=BF16)
    eye_c = jnp.eye(2, dtype=BF16)
    wq = jnp.einsum('bngcd,nm,ce->bcngmed', q5, eye_n, eye_c).reshape(db, 16, 512)
    n_pool = cache_k.shape[1]
    page = cache_k.shape[2]
    kt_pages = jnp.transpose(cache_k[0], (0, 2, 3, 4, 1)).reshape(n_pool, 512, page)
    v_pages = cache_v[0].reshape(n_pool, page * ATT_KV_HEADS, ATT_DV)
    oa_s = _attn_sample(wq, ks.reshape(db, 1, 512), vs.reshape(db, 1, 512), kt_pages, v_pages,
                        page_table, lam_p, sub_row, lam_init, _pick(page_table.shape[1], 32))
    row = lambda a: a.astype(F32).reshape(db, HG_HEADS, HG_DK)
    oh_s, state_s = _hgrn_sample(state_hgrn[0], row(hq), row(logf), row(kk), row(hv), row(og), hgn_row)
    ys = tail(xs, oa_s.reshape(db, 1024).astype(BF16), oh_s.reshape(db, 1024).astype(BF16), sga, sgh,
              _pick(db, 512), _pick(db, 1024))

    return (yp.reshape(b, t, D_MODEL), ys.reshape(db, 1, D_MODEL),
            k_prompt, v.reshape(1, b, t, ATT_KV_HEADS, ATT_DV),
            state_p.reshape(1, b, HG_HEADS, HG_DK, HG_DV),
            ks.reshape(1, db, 1, ATT_KV_HEADS, 2, ATT_DHALF), vs.reshape(1, db, 1, ATT_KV_HEADS, ATT_DV),
            state_s.reshape(1, db, HG_HEADS, HG_DK, HG_DV))
```

```python
import functools
import math

import numpy as np
import jax
import jax.numpy as jnp
from jax import lax
from jax.experimental import pallas as pl
from jax.experimental.pallas import tpu as pltpu

F32 = jnp.float32
BF16 = jnp.bfloat16

D_MODEL = 1024
ATT_HEADS = 8
ATT_KV_HEADS = 4
ATT_DHALF = 64
ATT_DV = 128
HG_HEADS = 8
HG_DK = 128
HG_DV = 128
N_GROUPS = 4
EXP_PER_GROUP = 8
N_EXPERTS = 32
D_FF = 512
EPS = 1e-6
LANES = 128
VMEM_LIMIT = 56 * 1024 * 1024

LOG2E = 1.4426950408889634
Q_SCALE = (ATT_DHALF ** -0.5) * LOG2E

MOE_TILE = 512
HG_CHUNK = 64
HG_LEVELS = 6


def _dot(a, b):
    return jnp.dot(a, b, preferred_element_type=F32)


def _dot_nt(a, b):
    return lax.dot_general(a, b, (((1,), (1,)), ((), ())), preferred_element_type=F32)


def _dot_tn(a, b):
    return lax.dot_general(a, b, (((0,), (0,)), ((), ())), preferred_element_type=F32)


def _split3(x):
    hi = x.astype(BF16)
    r1 = x - hi.astype(F32)
    mid = r1.astype(BF16)
    lo = (r1 - mid.astype(F32)).astype(BF16)
    return hi, mid, lo


def _seg_rms(z, seg, gain_row, scale):
    outs = []
    for i in range(z.shape[1] // LANES):
        zi = z[:, i * LANES:(i + 1) * LANES]
        z2 = zi * zi
        hi = z2.astype(BF16)
        lo = (z2 - hi.astype(F32)).astype(BF16)
        ss = _dot(hi, seg) + _dot(lo, seg)
        y = zi * lax.rsqrt(ss * (1.0 / ATT_DHALF) + EPS) * gain_row
        if scale != 1.0:
            y = y * scale
        outs.append(y)
    return jnp.concatenate(outs, axis=1)


def _params(*sem):
    return pltpu.CompilerParams(dimension_semantics=sem, vmem_limit_bytes=VMEM_LIMIT)


def _rms_kernel(x_ref, g_ref, h_ref):
    x = x_ref[...]
    ms = jnp.mean(x * x, axis=-1, keepdims=True)
    h_ref[...] = (x * lax.rsqrt(ms + EPS) * g_ref[...]).astype(BF16)


def _rms_cast(x, g, tm):
    n = x.shape[0]
    return pl.pallas_call(
        _rms_kernel,
        grid=(n // tm,),
        in_specs=[pl.BlockSpec((tm, D_MODEL), lambda i: (i, 0)), pl.BlockSpec((1, D_MODEL), lambda i: (0, 0))],
        out_specs=pl.BlockSpec((tm, D_MODEL), lambda i: (i, 0)),
        out_shape=jax.ShapeDtypeStruct((n, D_MODEL), BF16),
        compiler_params=_params("parallel"),
        name="rms_in",
    )(x, g)


def _proj_kernel(h_ref, w_ref, *refs, mode, n_aux):
    aux, outs = refs[:n_aux], refs[n_aux:]
    z = _dot(h_ref[...], w_ref[...])
    if mode == "q":
        outs[0][...] = _seg_rms(z, aux[0][...], aux[1][...], Q_SCALE).astype(BF16)
    elif mode in ("kv", "kv_t"):
        k = _seg_rms(z[:, :512], aux[0][...], aux[1][...], 1.0)
        if mode == "kv_t":
            outs[0][0] = k.T
            tm = z.shape[0]
            for n in range(ATT_KV_HEADS):
                outs[1][pl.ds(n, tm, stride=ATT_KV_HEADS), :] = z[:, 512 + n * LANES:512 + (n + 1) * LANES]
        else:
            outs[0][...] = k
            outs[1][...] = z[:, 512:]
    elif mode == "hq":
        outs[0][...] = (z * (HG_DK ** -0.5)).astype(BF16)
    elif mode == "hf":
        lbp = aux[0][...]
        m = jnp.max(lbp, axis=0, keepdims=True)
        e = jnp.exp(lbp - m)
        lb = e[0:1] / jnp.sum(e, axis=0, keepdims=True)
        sig = jax.nn.sigmoid(z)
        f = lb + (1.0 - lb) * sig
        outs[0][...] = jnp.log(f)
        outs[1][...] = (1.0 - lb) * (1.0 - sig)
    elif mode == "hv":
        outs[0][...] = z.astype(BF16)
    elif mode == "silu":
        outs[0][...] = z * jax.nn.sigmoid(z)
    elif mode == "sigmoid":
        outs[0][...] = jax.nn.sigmoid(z)


def _proj(h, w, col, mode, aux, outs, tm, name):
    n = h.shape[0]
    aux_specs = [pl.BlockSpec(a.shape, lambda i: (0, 0)) for a in aux]
    res = pl.pallas_call(
        functools.partial(_proj_kernel, mode=mode, n_aux=len(aux)),
        grid=(n // tm,),
        in_specs=[pl.BlockSpec((tm, D_MODEL), lambda i: (i, 0)),
                  pl.BlockSpec((D_MODEL, 1024), lambda i, col=col: (0, col))] + aux_specs,
        out_specs=tuple(pl.BlockSpec(o[2], o[3]) for o in outs),
        out_shape=tuple(jax.ShapeDtypeStruct(o[0], o[1]) for o in outs),
        compiler_params=_params("parallel"),
        name=name,
    )(h, w, *aux)
    return res


def _inproj(x, gin, w, qn_row, kn_row, lbp, seg, tm, seq):
    n = x.shape[0]
    h = _rms_cast(x, gin, tm)
    tok = lambda width, dt: ((n, width), dt, (tm, width), lambda i: (i, 0))
    q, = _proj(h, w, 0, "q", [seg, qn_row], [tok(1024, BF16)], tm, "proj_q")
    if seq is None:
        k, v = _proj(h, w, 1, "kv", [seg, kn_row], [tok(512, F32), tok(512, F32)], tm, "proj_kv")
    else:
        b, t = seq
        tpb = t // tm
        k_out = ((b, 512, t), F32, (1, 512, tm), lambda i: (i // tpb, 0, i % tpb))
        v_out = ((n * ATT_KV_HEADS, LANES), F32, (tm * ATT_KV_HEADS, LANES), lambda i: (i, 0))
        k, v = _proj(h, w, 1, "kv_t", [seg, kn_row], [k_out, v_out], tm, "proj_kv")
    hq, = _proj(h, w, 2, "hq", [], [tok(1024, BF16)], tm, "proj_hq")
    logf, kk = _proj(h, w, 3, "hf", [lbp], [tok(1024, F32), tok(1024, F32)], tm, "proj_hf")
    hv, = _proj(h, w, 4, "hv", [], [tok(1024, BF16)], tm, "proj_hv")
    og, = _proj(h, w, 5, "silu", [], [tok(1024, F32)], tm, "proj_og")
    sga, = _proj(h, w, 6, "sigmoid", [], [tok(1024, F32)], tm, "proj_ga")
    sgh, = _proj(h, w, 7, "sigmoid", [], [tok(1024, F32)], tm, "proj_gh")
    return q, k, v, hq, logf, kk, hv, og, sga, sgh


def _diff_lambda(lp, lam_init):
    a = jnp.sum(lp[0:1] * lp[1:2], axis=-1, keepdims=True)
    b = jnp.sum(lp[2:3] * lp[3:4], axis=-1, keepdims=True)
    return jnp.exp(a) - jnp.exp(b) + lam_init


def _sub_norm(o, sub_row, lam_init):
    ms = jnp.mean(o * o, axis=-1, keepdims=True)
    return o * lax.rsqrt(ms + EPS) * sub_row * (1.0 - lam_init)


def _attn_kernel(q_ref, kt_ref, v_ref, lamp_ref, sub_ref, o_ref,
                 kb_scr, vb_scr, qs_scr, m_scr, acc_scr, *, tq, lam_init):
    qi = pl.program_id(2)

    @pl.when(qi == 0)
    def _():
        kb_scr[...] = kt_ref[0].astype(BF16)
        t_len = vb_scr.shape[0]
        vb_scr[:, 0:LANES] = v_ref[0, pl.ds(pl.program_id(1), t_len, stride=ATT_KV_HEADS), :].astype(BF16)
        vb_scr[:, LANES:2 * LANES] = jnp.ones((vb_scr.shape[0], LANES), BF16)

    q = q_ref[0].astype(F32)
    lane = lax.broadcasted_iota(jnp.int32, (tq, LANES), 1)
    for g in range(2):
        qg = q[:, g * LANES:(g + 1) * LANES]
        for c in range(2):
            keep = (lane < ATT_DHALF) if c == 0 else (lane >= ATT_DHALF)
            r = g * 2 + c
            qs_scr[r * tq:(r + 1) * tq, :] = jnp.where(keep, qg, 0.0).astype(BF16)
    def chunk(j, masked, first):
        off = pl.multiple_of(j * tq, tq)
        s = _dot(qs_scr[...], kb_scr[:, pl.ds(off, tq)])
        if masked:
            row = lax.rem(lax.broadcasted_iota(jnp.int32, s.shape, 0), tq)
            s = jnp.where(lax.broadcasted_iota(jnp.int32, s.shape, 1) <= row, s, -jnp.inf)
        s_max = jnp.max(s, axis=-1, keepdims=True)
        if first:
            m_new = jnp.broadcast_to(s_max, m_scr.shape)
        else:
            m_prev = m_scr[...]
            m_new = jnp.maximum(m_prev, s_max)
            alpha = jnp.exp2(m_prev - m_new)
        pr = jnp.exp2(s - jnp.concatenate([m_new] * (tq // LANES), axis=1))
        pv = _dot(pr.astype(BF16), vb_scr[pl.ds(off, tq), :])
        acc_scr[...] = pv if first else jnp.concatenate([alpha, alpha], axis=1) * acc_scr[...] + pv
        m_scr[...] = m_new

    def body(j, carry):
        chunk(j, False, False)
        return carry

    @pl.when(qi == 0)
    def _():
        chunk(0, True, True)

    @pl.when(qi > 0)
    def _():
        chunk(0, False, True)
        lax.fori_loop(1, qi, body, 0)
        chunk(qi, True, False)

    lam = _diff_lambda(lamp_ref[...], lam_init)
    acc = acc_scr[...]
    out = acc[:, 0:LANES] / acc[:, LANES:2 * LANES]
    for g in range(2):
        o0 = out[(2 * g) * tq:(2 * g + 1) * tq]
        o1 = out[(2 * g + 1) * tq:(2 * g + 2) * tq]
        o_ref[0, :, g * LANES:(g + 1) * LANES] = _sub_norm(o0 - lam * o1, sub_ref[...], lam_init).astype(BF16)


def _attn_prompt(q, kt, v, lam_p, sub_row, lam_init, tq):
    b, t, _ = q.shape
    return pl.pallas_call(
        functools.partial(_attn_kernel, tq=tq, lam_init=lam_init),
        grid=(b, ATT_KV_HEADS, t // tq),
        in_specs=[
            pl.BlockSpec((1, tq, 256), lambda bi, n, qi: (bi, qi, n)),
            pl.BlockSpec((1, LANES, t), lambda bi, n, qi: (bi, n, 0)),
            pl.BlockSpec((1, t * ATT_KV_HEADS, LANES), lambda bi, n, qi: (bi, 0, 0)),
            pl.BlockSpec((4, ATT_DHALF), lambda bi, n, qi: (0, 0)),
            pl.BlockSpec((1, LANES), lambda bi, n, qi: (0, 0)),
        ],
        out_specs=pl.BlockSpec((1, tq, 256), lambda bi, n, qi: (bi, qi, n)),
        out_shape=jax.ShapeDtypeStruct((b, t, 1024), BF16),
        scratch_shapes=[pltpu.VMEM((LANES, t), BF16), pltpu.VMEM((t, 2 * LANES), BF16),
                        pltpu.VMEM((4 * tq, LANES), BF16), pltpu.VMEM((4 * tq, LANES), F32),
                        pltpu.VMEM((4 * tq, 2 * LANES), F32)],
        compiler_params=_params("parallel", "parallel", "arbitrary"),
        name="attn_prompt",
    )(q, kt, v, lam_p, sub_row)


def _hgrn_tables(c, levels):
    t = np.arange(c)[:, None]
    u = np.arange(c)[None, :]
    w = [(u <= t), (u > t)]
    masks = [(t == u)]
    for l in range(1, levels + 1):
        n = 1 << l
        half = n // 2
        mid = (t // n) * n + half
        hi = t >= mid
        w.append(np.where(hi, (u > mid) & (u <= t), (u > t) & (u <= mid)))
        masks.append((t // n == u // n) & (t % n >= half) & (u % n < half))
    w = np.concatenate(w, axis=0).astype(np.float32)
    w3 = np.concatenate([w, w, w], axis=1)
    if len(masks) % 2:
        masks.append(np.zeros_like(masks[0]))
    pairs = [np.concatenate([masks[i], masks[i + 1]], axis=1) for i in range(0, len(masks), 2)]
    return w3, np.stack(pairs).astype(np.float32)


def _hgrn_kernel(q_ref, g_ref, k_ref, v_ref, og_ref, norm_ref, w_ref, mask_ref, o_ref, s_ref,
                 st_scr, *, c, levels, n_chunks):
    ti = pl.program_id(2)

    @pl.when(ti == 0)
    def _():
        st_scr[...] = jnp.zeros(st_scr.shape, F32)

    w = w_ref[...]
    zeros = jnp.zeros((c, LANES), BF16)
    chunk_rows = [slice(ci * c, (ci + 1) * c) for ci in range(n_chunks)]
    g3 = jnp.concatenate([jnp.concatenate(_split3(g_ref[0, rows, :]), axis=0) for rows in chunk_rows], axis=1)
    e_all = jnp.exp(_dot(w, g3))

    parts = []
    for ci, rows in enumerate(chunk_rows):
        q = q_ref[0, rows, :].astype(F32)
        k = k_ref[0, rows, :]
        v = v_ref[0, rows, :]
        e = e_all[:, ci * LANES:(ci + 1) * LANES]
        e_b = e[0:c]
        e_k = e[c:2 * c]
        qs = [q.astype(BF16)]
        ks = [k.astype(BF16)]
        for l in range(1, levels + 1):
            e_l = e[(l + 1) * c:(l + 2) * c]
            qs.append((q * e_l).astype(BF16))
            ks.append((k * e_l).astype(BF16))
        if len(qs) % 2:
            qs.append(zeros)
            ks.append(zeros)
        a2 = None
        for p in range(len(qs) // 2):
            ql = jnp.concatenate([qs[2 * p], qs[2 * p + 1]], axis=1)
            kbd = jnp.concatenate([jnp.concatenate([ks[2 * p], zeros], axis=1),
                                   jnp.concatenate([zeros, ks[2 * p + 1]], axis=1)], axis=0)
            term = mask_ref[p] * _dot_nt(ql, kbd)
            a2 = term if a2 is None else a2 + term
        o_intra = _dot(a2.astype(BF16), jnp.concatenate([v, v], axis=0))
        st_add = _dot_tn(v, (k * e_k).astype(BF16))
        parts.append(((q * e_b).astype(BF16), e_b[c - 1:c, :], o_intra, st_add))

    st = st_scr[...]
    for rows, (q_b, decay, o_intra, st_add) in zip(chunk_rows, parts):
        o = _dot_nt(q_b, st.astype(BF16)) + o_intra
        st = st * decay + st_add
        ms = jnp.mean(o * o, axis=-1, keepdims=True)
        y = o * lax.rsqrt(ms + EPS) * norm_ref[...] * og_ref[0, rows, :]
        o_ref[0, rows, :] = y.astype(BF16)
    st_scr[...] = st

    @pl.when(ti == pl.num_programs(2) - 1)
    def _():
        s_ref[0, 0] = st.T


def _hgrn_prompt(hq, logf, kk, hv, og, norm_row, ct):
    b, t, _ = hq.shape
    c, levels = HG_CHUNK, HG_LEVELS
    w_np, m_np = _hgrn_tables(c, levels)
    w = jnp.asarray(w_np, BF16)
    masks = jnp.asarray(m_np, F32)
    tok = pl.BlockSpec((1, ct, LANES), lambda bi, h, ti: (bi, ti, h))
    return pl.pallas_call(
        functools.partial(_hgrn_kernel, c=c, levels=levels, n_chunks=ct // c),
        grid=(b, HG_HEADS, t // ct),
        in_specs=[tok, tok, tok, tok, tok,
                  pl.BlockSpec((1, LANES), lambda bi, h, ti: (0, 0)),
                  pl.BlockSpec(w.shape, lambda bi, h, ti: (0, 0)),
                  pl.BlockSpec(masks.shape, lambda bi, h, ti: (0, 0, 0))],
        out_specs=(tok, pl.BlockSpec((1, 1, HG_DK, HG_DV), lambda bi, h, ti: (bi, h, 0, 0))),
        out_shape=(jax.ShapeDtypeStruct((b, t, 1024), BF16),
                   jax.ShapeDtypeStruct((b, HG_HEADS, HG_DK, HG_DV), F32)),
        scratch_shapes=[pltpu.VMEM((HG_DV, HG_DK), F32)],
        compiler_params=_params("parallel", "parallel", "arbitrary"),
        name="hgrn_prompt",
    )(hq, logf, kk, hv, og, norm_row, w, masks)


def _dec_attn_kernel(pt_ref, wq_ref, kn_ref, vn_ref, lamp_ref, sub_ref, *refs, pages, lam_init):
    k_refs = refs[:pages]
    v_refs = refs[pages:2 * pages]
    o_ref = refs[2 * pages]
    m_scr, l_scr, acc_scr = refs[2 * pages + 1:]
    j = pl.program_id(1)
    page = k_refs[0].shape[2]

    @pl.when(j == 0)
    def _():
        m_scr[...] = jnp.full(m_scr.shape, -jnp.inf, F32)
        l_scr[...] = jnp.zeros(l_scr.shape, F32)
        acc_scr[...] = jnp.zeros(acc_scr.shape, F32)

    rown = lax.rem(lax.broadcasted_iota(jnp.int32, (16, LANES), 0), 8) // 2
    wq = wq_ref[0]
    s = jnp.concatenate([_dot(wq, k_refs[i][0].astype(BF16)) for i in range(pages)], axis=1)
    m_prev = m_scr[...]
    m_new = jnp.maximum(m_prev, jnp.max(s, axis=-1, keepdims=True))
    alpha = jnp.exp2(m_prev - m_new)
    pr = jnp.exp2(s - m_new)
    l_scr[...] = alpha * l_scr[...] + jnp.sum(pr, axis=-1, keepdims=True)
    prb = pr.astype(BF16)
    pv = jnp.zeros((16, LANES), F32)
    for n in range(ATT_KV_HEADS):
        v_n = jnp.concatenate([v_refs[i][0, pl.ds(n, page, stride=ATT_KV_HEADS), :] for i in range(pages)],
                              axis=0).astype(BF16)
        pv = pv + jnp.where(rown == n, _dot(prb, v_n), 0.0)
    acc_scr[...] = alpha * acc_scr[...] + pv
    m_scr[...] = m_new

    @pl.when(j == pl.num_programs(1) - 1)
    def _():
        s_cur = jnp.sum(wq.astype(F32) * kn_ref[0], axis=-1, keepdims=True)
        m_old = m_scr[...]
        m_fin = jnp.maximum(m_old, s_cur)
        al = jnp.exp2(m_old - m_fin)
        pc = jnp.exp2(s_cur - m_fin)
        l_fin = al * l_scr[...] + pc
        vn = vn_ref[0]
        v_cur = jnp.zeros((16, LANES), F32)
        for n in range(ATT_KV_HEADS):
            v_cur = v_cur + jnp.where(rown == n, vn[:, n * LANES:(n + 1) * LANES], 0.0)
        out = (al * acc_scr[...] + pc * v_cur) / l_fin
        lam = _diff_lambda(lamp_ref[...], lam_init)
        o = out[0:8] - lam * out[8:16]
        o_ref[0] = _sub_norm(o, sub_ref[...], lam_init)


def _attn_sample(wq, k_new, v_new, kt_pages, v_pages, page_table, lam_p, sub_row, lam_init, pages):
    db, n_pages = page_table.shape
    pt_flat = page_table.reshape(-1)
    blk = kt_pages.shape[1:]

    def page_spec(i):
        return pl.BlockSpec((1,) + blk, lambda bi, j, pt, i=i: (pt[bi * n_pages + j * pages + i], 0, 0))

    row3 = lambda w: pl.BlockSpec((1, 1, w), lambda bi, j, pt: (bi, 0, 0))
    grid_spec = pltpu.PrefetchScalarGridSpec(
        num_scalar_prefetch=1,
        grid=(db, n_pages // pages),
        in_specs=[pl.BlockSpec((1, 16, 512), lambda bi, j, pt: (bi, 0, 0)), row3(512), row3(512),
                  pl.BlockSpec((4, ATT_DHALF), lambda bi, j, pt: (0, 0)),
                  pl.BlockSpec((1, LANES), lambda bi, j, pt: (0, 0))]
                 + [page_spec(i) for i in range(pages)] + [page_spec(i) for i in range(pages)],
        out_specs=pl.BlockSpec((1, 8, LANES), lambda bi, j, pt: (bi, 0, 0)),
        scratch_shapes=[pltpu.VMEM((16, 1), F32), pltpu.VMEM((16, 1), F32), pltpu.VMEM((16, LANES), F32)],
    )
    return pl.pallas_call(
        functools.partial(_dec_attn_kernel, pages=pages, lam_init=lam_init),
        grid_spec=grid_spec,
        out_shape=jax.ShapeDtypeStruct((db, 8, LANES), F32),
        compiler_params=_params("parallel", "arbitrary"),
        name="attn_sample",
    )(pt_flat, wq, k_new, v_new, lam_p, sub_row, *([kt_pages] * pages), *([v_pages] * pages))


def _hgrn_step_kernel(s_ref, q_ref, g_ref, k_ref, v_ref, og_ref, norm_ref, o_ref, sn_ref):
    pad = jnp.zeros((LANES - 3 * HG_HEADS, LANES), F32)
    for bi in range(s_ref.shape[0]):
        cols = jnp.concatenate([q_ref[bi], g_ref[bi], k_ref[bi], pad], axis=0).T
        v_all = v_ref[bi]
        og_all = og_ref[bi]
        outs = []
        for h in range(HG_HEADS):
            s = s_ref[bi, h]
            qc = cols[:, h:h + 1]
            f = jnp.exp(cols[:, HG_HEADS + h:HG_HEADS + h + 1])
            kc = cols[:, 2 * HG_HEADS + h:2 * HG_HEADS + h + 1]
            v = v_all[h:h + 1]
            o = jnp.sum(s * (qc * f), axis=0, keepdims=True) + jnp.sum(qc * kc, axis=0, keepdims=True) * v
            sn_ref[bi, h] = s * f + kc * v
            ms = jnp.mean(o * o, axis=-1, keepdims=True)
            outs.append(o * lax.rsqrt(ms + EPS) * norm_ref[...] * og_all[h:h + 1])
        o_ref[bi] = jnp.concatenate(outs, axis=0)


def _hgrn_sample(state, q, g, k, v, og, norm_row):
    db = state.shape[0]
    nb = _pick(db, 4)
    row = pl.BlockSpec((nb, HG_HEADS, HG_DK), lambda bi: (bi, 0, 0))
    st = pl.BlockSpec((nb, HG_HEADS, HG_DK, HG_DV), lambda bi: (bi, 0, 0, 0))
    return pl.pallas_call(
        _hgrn_step_kernel,
        grid=(db // nb,),
        in_specs=[st, row, row, row, row, row, pl.BlockSpec((1, LANES), lambda bi: (0, 0))],
        out_specs=(row, st),
        out_shape=(jax.ShapeDtypeStruct((db, HG_HEADS, HG_DV), F32),
                   jax.ShapeDtypeStruct(state.shape, F32)),
        compiler_params=_params("parallel"),
        name="hgrn_sample",
    )(state, q, g, k, v, og, norm_row)


def _lane_first(cond, lane):
    return jnp.min(jnp.where(cond, lane, LANES), axis=-1, keepdims=True)


def _route(logits):
    lane = lax.broadcasted_iota(jnp.int32, logits.shape, 1)
    gmask = (lane >= N_EXPERTS) & (lane < N_EXPERTS + N_GROUPS)
    lg = jnp.where(gmask, logits, -jnp.inf)
    mg = jnp.max(lg, axis=-1, keepdims=True)
    eg = jnp.exp(lg - mg)
    pg = eg / jnp.sum(eg, axis=-1, keepdims=True)
    pg_top = jnp.max(pg, axis=-1, keepdims=True)
    gsel = _lane_first(gmask & (pg == pg_top), lane) - N_EXPERTS
    emask = (lane >= gsel * EXP_PER_GROUP) & (lane < (gsel + 1) * EXP_PER_GROUP)
    le = jnp.where(emask, logits, -jnp.inf)
    me = jnp.max(le, axis=-1, keepdims=True)
    ee = jnp.exp(le - me)
    pe = ee / jnp.sum(ee, axis=-1, keepdims=True)
    pe = jnp.where(emask, pe, -1.0)
    p1 = jnp.max(pe, axis=-1, keepdims=True)
    e1 = _lane_first(pe == p1, lane)
    pe2 = jnp.where(lane == e1, -1.0, pe)
    p2 = jnp.max(pe2, axis=-1, keepdims=True)
    e2 = _lane_first(pe2 == p2, lane)
    den = p1 + p2
    w1 = p1 / den * pg_top
    w2 = p2 / den * pg_top
    comb = jnp.where(lane == e1, w1, 0.0) + jnp.where(lane == e2, w2, 0.0)
    picks = jnp.where(lane == 0, e1.astype(F32), jnp.where(lane == 1, e2.astype(F32),
                      jnp.where(lane == 2, w1, jnp.where(lane == 3, w2, 0.0))))
    return comb, picks


ROW_TILES = D_MODEL // LANES


def _merge_kernel(oa_ref, oh_ref, sga_ref, sgh_ref, x_ref, wpa_ref, wph_ref, wo_ref, gf_ref,
                  wr_ref, br_ref, x1_ref, xn_ref, route_ref, *, sparse):
    ya = _dot(oa_ref[...], wpa_ref[...])
    yh = _dot(oh_ref[...], wph_ref[...])
    mixed = (sga_ref[...] * ya + sgh_ref[...] * yh).astype(BF16)
    x1 = x_ref[...] + _dot(mixed, wo_ref[...])
    x1_ref[...] = x1
    ms = jnp.mean(x1 * x1, axis=-1, keepdims=True)
    xn_f = x1 * lax.rsqrt(ms + EPS) * gf_ref[...]
    xn = xn_f.astype(BF16)
    comb, picks = _route(_dot(xn, wr_ref[...]) + br_ref[...])
    if sparse:
        tm = x1.shape[0]
        for s in range(ROW_TILES):
            xn_ref[pl.ds(s, tm, stride=ROW_TILES), :] = xn_f[:, s * LANES:(s + 1) * LANES]
        route_ref[...] = picks
    else:
        xn_ref[...] = xn
        route_ref[...] = comb


def _merge(oa, oh, sga, sgh, x, wpa, wph, wo, gf, wr, br, tm, sparse):
    n = x.shape[0]
    tok = pl.BlockSpec((tm, 1024), lambda i: (i, 0))
    wsp = pl.BlockSpec((1024, 1024), lambda i: (0, 0))
    if sparse:
        xn_spec = pl.BlockSpec((tm * ROW_TILES, LANES), lambda i: (i, 0))
        xn_shape = jax.ShapeDtypeStruct((n * ROW_TILES, LANES), F32)
    else:
        xn_spec, xn_shape = tok, jax.ShapeDtypeStruct((n, 1024), BF16)
    return pl.pallas_call(
        functools.partial(_merge_kernel, sparse=sparse),
        grid=(n // tm,),
        in_specs=[tok, tok, tok, tok, tok, wsp, wsp, wsp,
                  pl.BlockSpec((1, 1024), lambda i: (0, 0)),
                  pl.BlockSpec((1024, LANES), lambda i: (0, 0)),
                  pl.BlockSpec((1, LANES), lambda i: (0, 0))],
        out_specs=(tok, xn_spec, pl.BlockSpec((tm, LANES), lambda i: (i, 0))),
        out_shape=(jax.ShapeDtypeStruct((n, 1024), F32), xn_shape,
                   jax.ShapeDtypeStruct((n, LANES), F32)),
        compiler_params=_params("parallel"),
        name="merge",
    )(oa, oh, sga, sgh, x, wpa, wph, wo, gf, wr, br)


def _moe_kernel(xn_ref, comb_ref, x1_ref, wg_ref, wu_ref, wd_ref, y_ref, acc_scr):
    e = pl.program_id(1)

    @pl.when(e == 0)
    def _():
        acc_scr[...] = jnp.zeros(acc_scr.shape, F32)

    comb = comb_ref[...]
    lane = lax.broadcasted_iota(jnp.int32, comb.shape, 1)
    c = jnp.sum(jnp.where(lane == e, comb, 0.0), axis=-1, keepdims=True)
    xn = xn_ref[...]
    hg = _dot(xn, wg_ref[0])
    hu = _dot(xn, wu_ref[0])
    hid = hg * jax.nn.sigmoid(hg) * hu
    acc_scr[...] += _dot((hid * c).astype(BF16), wd_ref[0])

    @pl.when(e == pl.num_programs(1) - 1)
    def _():
        y_ref[...] = x1_ref[...] + acc_scr[...]


def _moe(xn, comb, x1, wg, wu, wd, tm):
    n = xn.shape[0]
    return pl.pallas_call(
        _moe_kernel,
        grid=(n // tm, N_EXPERTS),
        in_specs=[pl.BlockSpec((tm, 1024), lambda i, e: (i, 0)),
                  pl.BlockSpec((tm, LANES), lambda i, e: (i, 0)),
                  pl.BlockSpec((tm, 1024), lambda i, e: (i, 0)),
                  pl.BlockSpec((1, 1024, D_FF), lambda i, e: (e, 0, 0)),
                  pl.BlockSpec((1, 1024, D_FF), lambda i, e: (e, 0, 0)),
                  pl.BlockSpec((1, D_FF, 1024), lambda i, e: (e, 0, 0))],
        out_specs=pl.BlockSpec((tm, 1024), lambda i, e: (i, 0)),
        out_shape=jax.ShapeDtypeStruct((n, 1024), F32),
        scratch_shapes=[pltpu.VMEM((tm, 1024), F32)],
        compiler_params=_params("parallel", "arbitrary"),
        name="moe",
    )(xn, comb, x1, wg, wu, wd)


def _row_copy(src_hbm, src_row, dst, dst_row, sem):
    return pltpu.make_async_copy(src_hbm.at[pl.ds(src_row * ROW_TILES, ROW_TILES)],
                                 dst.at[pl.ds(dst_row * ROW_TILES, ROW_TILES)], sem)


def _wait_rows(src_hbm, dst, sem):
    pltpu.make_async_copy(src_hbm.at[pl.ds(0, dst.shape[0])], dst, sem).wait()


def _rows_to_mat(buf, n_rows):
    return jnp.concatenate([buf[pl.ds(s, n_rows, stride=ROW_TILES), :] for s in range(ROW_TILES)], axis=1)


def _moe_scatter_kernel(p1_ref, p2_ref, xn_ref, xs_in, xs_hbm, sem):
    del xs_in
    i = pl.program_id(0)
    tm = p1_ref.shape[0]

    def wait_tile():
        for _ in range(2):
            pltpu.make_async_copy(xn_ref, xs_hbm.at[pl.ds(0, tm * ROW_TILES)], sem.at[0]).wait()

    def body(r, carry):
        src = xn_ref.at[pl.ds(r * ROW_TILES, ROW_TILES)]
        pltpu.make_async_copy(src, xs_hbm.at[pl.ds(p1_ref[r] * ROW_TILES, ROW_TILES)], sem.at[0]).start(priority=0)
        pltpu.make_async_copy(src, xs_hbm.at[pl.ds(p2_ref[r] * ROW_TILES, ROW_TILES)], sem.at[0]).start(priority=1)
        return carry

    lax.fori_loop(0, tm, body, 0)
    wait_tile()


def _moe_scatter(xg, pos1, pos2, rows, tm):
    n = pos1.shape[0]
    idx = pl.BlockSpec((tm,), lambda i: (i,), memory_space=pltpu.SMEM)
    xs0 = jnp.zeros((rows * ROW_TILES, LANES), F32)
    return pl.pallas_call(
        _moe_scatter_kernel,
        grid=(n // tm,),
        in_specs=[idx, idx, pl.BlockSpec((tm * ROW_TILES, LANES), lambda i: (i, 0)),
                  pl.BlockSpec(memory_space=pl.ANY)],
        out_specs=pl.BlockSpec(memory_space=pl.ANY),
        out_shape=jax.ShapeDtypeStruct(xs0.shape, F32),
        scratch_shapes=[pltpu.SemaphoreType.DMA((1,))],
        input_output_aliases={3: 0},
        compiler_params=_params("arbitrary"),
        name="moe_scatter",
    )(pos1, pos2, xg, xs0)


def _moe_ffn_kernel(te_ref, xs_ref, wg_ref, wu_ref, wd_ref, ys_ref, *, tp):
    del te_ref
    x = _rows_to_mat(xs_ref, tp).astype(BF16)
    hg = _dot(x, wg_ref[0])
    hu = _dot(x, wu_ref[0])
    hid = (hg * jax.nn.sigmoid(hg) * hu).astype(BF16)
    out = _dot(hid, wd_ref[0])
    for s in range(ROW_TILES):
        ys_ref[pl.ds(s, tp, stride=ROW_TILES), :] = out[:, s * LANES:(s + 1) * LANES]


def _moe_ffn(xs, tile_expert, wg, wu, wd, tp):
    n_tiles = tile_expert.shape[0]
    rows = pl.BlockSpec((tp * ROW_TILES, LANES), lambda i, te: (i, 0))
    wsp = lambda a, b: pl.BlockSpec((1, a, b), lambda i, te: (te[i], 0, 0))
    grid_spec = pltpu.PrefetchScalarGridSpec(
        num_scalar_prefetch=1,
        grid=(n_tiles,),
        in_specs=[rows, wsp(1024, D_FF), wsp(1024, D_FF), wsp(D_FF, 1024)],
        out_specs=rows,
    )
    return pl.pallas_call(
        functools.partial(_moe_ffn_kernel, tp=tp),
        grid_spec=grid_spec,
        out_shape=jax.ShapeDtypeStruct(xs.shape, F32),
        compiler_params=_params("parallel"),
        name="moe_ffn",
    )(tile_expert, xs, wg, wu, wd)


def _moe_combine_kernel(p1_ref, p1n_ref, p2_ref, p2n_ref, ys_hbm, x1_ref, w1_ref, w2_ref, y_ref, buf, sem, *, tm):
    i = pl.program_id(0)
    slot = lax.rem(i, 2)

    def gather(a_ref, b_ref, s):
        def body(r, carry):
            _row_copy(ys_hbm, a_ref[r], buf.at[s, 0], r, sem.at[s, 0]).start(priority=0)
            _row_copy(ys_hbm, b_ref[r], buf.at[s, 1], r, sem.at[s, 1]).start(priority=1)
            return carry
        lax.fori_loop(0, tm, body, 0)

    @pl.when(i == 0)
    def _():
        gather(p1_ref, p2_ref, 0)

    @pl.when(i + 1 < pl.num_programs(0))
    def _():
        gather(p1n_ref, p2n_ref, 1 - slot)

    _wait_rows(ys_hbm, buf.at[slot, 0], sem.at[slot, 0])
    _wait_rows(ys_hbm, buf.at[slot, 1], sem.at[slot, 1])
    y_ref[...] = x1_ref[...] + (w1_ref[...] * _rows_to_mat(buf.at[slot, 0], tm)
                                + w2_ref[...] * _rows_to_mat(buf.at[slot, 1], tm))


def _moe_combine(ys, pos1, pos2, w1, w2, x1, tm):
    n = x1.shape[0]
    n_tiles = n // tm
    idx = lambda off: pl.BlockSpec((tm,), lambda i: (jnp.minimum(i + off, n_tiles - 1),),
                                   memory_space=pltpu.SMEM)
    col = pl.BlockSpec((tm, 1), lambda i: (i, 0))
    return pl.pallas_call(
        functools.partial(_moe_combine_kernel, tm=tm),
        grid=(n_tiles,),
        in_specs=[idx(0), idx(1), idx(0), idx(1), pl.BlockSpec(memory_space=pl.ANY),
                  pl.BlockSpec((tm, 1024), lambda i: (i, 0)), col, col],
        out_specs=pl.BlockSpec((tm, 1024), lambda i: (i, 0)),
        out_shape=jax.ShapeDtypeStruct((n, 1024), F32),
        scratch_shapes=[pltpu.VMEM((2, 2, tm * ROW_TILES, LANES), F32), pltpu.SemaphoreType.DMA((2, 2))],
        compiler_params=_params("arbitrary"),
        name="moe_combine",
    )(pos1, pos1, pos2, pos2, ys, x1, w1, w2)


def _rank_kernel(picks_ref, tri_ref, rank_ref, cnt_ref, base_scr):
    @pl.when(pl.program_id(0) == 0)
    def _():
        base_scr[...] = jnp.zeros(base_scr.shape, F32)

    picks = picks_ref[...]
    lane = lax.broadcasted_iota(jnp.int32, picks.shape, 1)
    chosen = (lane == picks[:, 0:1].astype(jnp.int32)) | (lane == picks[:, 1:2].astype(jnp.int32))
    both = jnp.where(chosen, 1.0, 0.0)
    before = _dot(tri_ref[...], both.astype(BF16)) + base_scr[...]
    rank_ref[...] = jnp.where(chosen, before, 0.0)
    base_scr[...] += jnp.sum(both, axis=0, keepdims=True)
    cnt_ref[...] = base_scr[...]


def _pos_kernel(picks_ref, rank_ref, start_ref, pos_ref):
    picks = picks_ref[...]
    lane = lax.broadcasted_iota(jnp.int32, picks.shape, 1)
    where_to = rank_ref[...] + start_ref[...]
    pos = [jnp.sum(jnp.where(lane == picks[:, p:p + 1].astype(jnp.int32), where_to, 0.0), axis=-1, keepdims=True)
           for p in range(2)]
    pos_ref[...] = jnp.where(lane == 0, pos[0], jnp.where(lane == 1, pos[1], 0.0))


def _dispatch(picks, tp, tm):
    n = picks.shape[0]
    tri = jnp.asarray(np.tril(np.ones((tm, tm), np.float32), -1), BF16)
    tok = pl.BlockSpec((tm, LANES), lambda i: (i, 0))
    one = pl.BlockSpec((1, LANES), lambda i: (0, 0))
    rank, cnt = pl.pallas_call(
        _rank_kernel,
        grid=(n // tm,),
        in_specs=[tok, pl.BlockSpec((tm, tm), lambda i: (0, 0))],
        out_specs=(tok, one),
        out_shape=(jax.ShapeDtypeStruct((n, LANES), F32), jax.ShapeDtypeStruct((1, LANES), F32)),
        scratch_shapes=[pltpu.VMEM((1, LANES), F32)],
        compiler_params=_params("arbitrary"),
        name="moe_rank",
    )(picks, tri)
    counts = cnt[0, :N_EXPERTS].astype(jnp.int32)
    tiles = (counts + tp - 1) // tp
    tile_end = jnp.cumsum(tiles)
    row_start = jnp.zeros((1, LANES), F32).at[0, :N_EXPERTS].set(((tile_end - tiles) * tp).astype(F32))
    tp_pos = _pick(n, 4 * tm)
    tok_pos = pl.BlockSpec((tp_pos, LANES), lambda i: (i, 0))
    pos = pl.pallas_call(
        _pos_kernel,
        grid=(n // tp_pos,),
        in_specs=[tok_pos, tok_pos, one],
        out_specs=tok_pos,
        out_shape=jax.ShapeDtypeStruct((n, LANES), F32),
        compiler_params=_params("parallel"),
        name="moe_pos",
    )(picks, rank, row_start)
    n_tiles = (2 * n) // tp + N_EXPERTS
    tile_ids = jnp.arange(n_tiles, dtype=jnp.int32)
    tile_expert = jnp.minimum(jnp.sum((tile_end[None, :] <= tile_ids[:, None]).astype(jnp.int32), axis=1),
                              N_EXPERTS - 1)
    return pos[:, 0].astype(jnp.int32), pos[:, 1].astype(jnp.int32), tile_expert, n_tiles * tp


def _pick(n, pref):
    t = min(n, pref)
    while n % t:
        t //= 2
    return t


def kernel(x_prompt, x_sample, cache_k, cache_v, state_hgrn, page_table, rms_in, w_in, att_q_norm, att_k_norm, att_lambda, att_sub_norm, hg_lower_bound, hg_out_norm, w_branch_att, w_branch_hg, w_out, rms_ffn, w_router_group, b_router_group, w_router_expert, b_router_expert, w_exp_gate, w_exp_up, w_exp_down):
    depth = rms_in.shape[0]
    assert depth == 1 and hg_lower_bound.shape[0] == 2
    b, t, _ = x_prompt.shape
    db, ds, _ = x_sample.shape
    assert ds == 1
    lam_init = 0.8 - 0.6 * math.exp(-0.3 * 0)

    w_in_bf = w_in[0].astype(BF16)
    qn_row = jnp.tile(att_q_norm[0], 2).reshape(1, LANES)
    kn_row = jnp.tile(att_k_norm[0], 2).reshape(1, LANES)
    seg_np = (np.arange(LANES)[:, None] // ATT_DHALF) == (np.arange(LANES)[None, :] // ATT_DHALF)
    seg = jnp.asarray(seg_np.astype(np.float32), BF16)
    sub_row = att_sub_norm[0].reshape(1, LANES)
    hgn_row = hg_out_norm[0].reshape(1, LANES)
    wpa = w_branch_att[0].astype(BF16)
    wph = w_branch_hg[0].astype(BF16)
    wo = w_out[0].astype(BF16)
    wr = jnp.zeros((D_MODEL, LANES), F32)
    wr = wr.at[:, :N_EXPERTS].set(w_router_expert[0]).at[:, N_EXPERTS:N_EXPERTS + N_GROUPS].set(w_router_group[0])
    wr = wr.astype(BF16)
    br = jnp.zeros((1, LANES), F32)
    br = br.at[0, :N_EXPERTS].set(b_router_expert[0]).at[0, N_EXPERTS:N_EXPERTS + N_GROUPS].set(b_router_group[0])
    weg = w_exp_gate[0].astype(BF16)
    weu = w_exp_up[0].astype(BF16)
    wed = w_exp_down[0].astype(BF16)
    gin = rms_in[0].reshape(1, D_MODEL)
    gffn = rms_ffn[0].reshape(1, D_MODEL)
    lam_p = att_lambda[0]

    def tail(x2, oa, oh, sga, sgh, tm_merge, tm_moe):
        sparse = x2.shape[0] >= 16 * MOE_TILE
        x1, xn, route = _merge(oa, oh, sga, sgh, x2, wpa, wph, wo, gffn, wr, br, tm_merge, sparse)
        if not sparse:
            return _moe(xn, route, x1, weg, weu, wed, tm_moe)
        pos1, pos2, tile_expert, rows = _dispatch(route, MOE_TILE, _pick(x2.shape[0], 1024))
        xs = _moe_scatter(xn, pos1, pos2, rows, _pick(x2.shape[0], 512))
        ys = _moe_ffn(xs, tile_expert, weg, weu, wed, MOE_TILE)
        return _moe_combine(ys, pos1, pos2, route[:, 2:3], route[:, 3:4], x1, _pick(x2.shape[0], 512))

    n = b * t
    xp = x_prompt.reshape(n, D_MODEL)
    q, kt, v, hq, logf, kk, hv, og, sga, sgh = _inproj(
        xp, gin, w_in_bf, qn_row, kn_row, hg_lower_bound, seg, _pick(t, 2048), (b, t))
    oa = _attn_prompt(q.reshape(b, t, 1024), kt, v.reshape(b, t * ATT_KV_HEADS, LANES), lam_p, sub_row, lam_init,
                      _pick(t, 512))
    r3 = lambda a: a.reshape(b, t, 1024)
    oh, state_p = _hgrn_prompt(r3(hq), r3(logf), r3(kk), r3(hv), r3(og), hgn_row, _pick(t, 1024))
    yp = tail(xp, oa.reshape(n, 1024), oh.reshape(n, 1024), sga, sgh, _pick(n, 512), _pick(n, 1024))
    k_prompt = jnp.transpose(kt.reshape(1, b, ATT_KV_HEADS, 2, ATT_DHALF, t), (0, 1, 5, 2, 3, 4))

    xs = x_sample.reshape(db, D_MODEL)
    q, ks, vs, hq, logf, kk, hv, og, sga, sgh = _inproj(
        xs, gin, w_in_bf, qn_row, kn_row, hg_lower_bound, seg, _pick(db, 1024), None)
    q5 = q.reshape(db, ATT_KV_HEADS, 2, 2, ATT_DHALF)
    eye_n = jnp.eye(ATT_KV_HEADS, dtype=BF16)
    eye_c = jnp.eye(2, dtype=BF16)
    wq = jnp.einsum('bngcd,nm,ce->bcngmed', q5, eye_n, eye_c).reshape(db, 16, 512)
    n_pool = cache_k.shape[1]
    page = cache_k.shape[2]
    kt_pages = jnp.transpose(cache_k[0], (0, 2, 3, 4, 1)).reshape(n_pool, 512, page)
    v_pages = cache_v[0].reshape(n_pool, page * ATT_KV_HEADS, ATT_DV)
    oa_s = _attn_sample(wq, ks.reshape(db, 1, 512), vs.reshape(db, 1, 512), kt_pages, v_pages,
                        page_table, lam_p, sub_row, lam_init, _pick(page_table.shape[1], 32))
    row = lambda a: a.astype(F32).reshape(db, HG_HEADS, HG_DK)
    oh_s, state_s = _hgrn_sample(state_hgrn[0], row(hq), row(logf), row(kk), row(hv), row(og), hgn_row)
    ys = tail(xs, oa_s.reshape(db, 1024).astype(BF16), oh_s.reshape(db, 1024).astype(BF16), sga, sgh,
              _pick(db, 512), _pick(db, 1024))

    return (yp.reshape(b, t, D_MODEL), ys.reshape(db, 1, D_MODEL),
            k_prompt, v.reshape(1, b, t, ATT_KV_HEADS, ATT_DV),
            state_p.reshape(1, b, HG_HEADS, HG_DK, HG_DV),
            ks.reshape(1, db, 1, ATT_KV_HEADS, 2, ATT_DHALF), vs.reshape(1, db, 1, ATT_KV_HEADS, ATT_DV),
            state_s.reshape(1, db, HG_HEADS, HG_DK, HG_DV))
```

```python
import functools
import math

import numpy as np
import jax
import jax.numpy as jnp
from jax import lax
from jax.experimental import pallas as pl
from jax.experimental.pallas import tpu as pltpu

F32 = jnp.float32
BF16 = jnp.bfloat16

D_MODEL = 1024
ATT_HEADS = 8
ATT_KV_HEADS = 4
ATT_DHALF = 64
ATT_DV = 128
HG_HEADS = 8
HG_DK = 128
HG_DV = 128
N_GROUPS = 4
EXP_PER_GROUP = 8
N_EXPERTS = 32
D_FF = 512
EPS = 1e-6
LANES = 128
VMEM_LIMIT = 56 * 1024 * 1024

LOG2E = 1.4426950408889634
Q_SCALE = (ATT_DHALF ** -0.5) * LOG2E

MOE_TILE = 512
HG_CHUNK = 64
HG_LEVELS = 6


def _dot(a, b):
    return jnp.dot(a, b, preferred_element_type=F32)


def _dot_nt(a, b):
    return lax.dot_general(a, b, (((1,), (1,)), ((), ())), preferred_element_type=F32)


def _dot_tn(a, b):
    return lax.dot_general(a, b, (((0,), (0,)), ((), ())), preferred_element_type=F32)


def _split3(x):
    hi = x.astype(BF16)
    r1 = x - hi.astype(F32)
    mid = r1.astype(BF16)
    lo = (r1 - mid.astype(F32)).astype(BF16)
    return hi, mid, lo


def _seg_rms(z, seg, gain_row, scale):
    outs = []
    for i in range(z.shape[1] // LANES):
        zi = z[:, i * LANES:(i + 1) * LANES]
        z2 = zi * zi
        hi = z2.astype(BF16)
        lo = (z2 - hi.astype(F32)).astype(BF16)
        ss = _dot(hi, seg) + _dot(lo, seg)
        y = zi * lax.rsqrt(ss * (1.0 / ATT_DHALF) + EPS) * gain_row
        if scale != 1.0:
            y = y * scale
        outs.append(y)
    return jnp.concatenate(outs, axis=1)


def _params(*sem):
    return pltpu.CompilerParams(dimension_semantics=sem, vmem_limit_bytes=VMEM_LIMIT)


def _rms_kernel(x_ref, g_ref, h_ref):
    x = x_ref[...]
    ms = jnp.mean(x * x, axis=-1, keepdims=True)
    h_ref[...] = (x * lax.rsqrt(ms + EPS) * g_ref[...]).astype(BF16)


def _rms_cast(x, g, tm):
    n = x.shape[0]
    return pl.pallas_call(
        _rms_kernel,
        grid=(n // tm,),
        in_specs=[pl.BlockSpec((tm, D_MODEL), lambda i: (i, 0)), pl.BlockSpec((1, D_MODEL), lambda i: (0, 0))],
        out_specs=pl.BlockSpec((tm, D_MODEL), lambda i: (i, 0)),
        out_shape=jax.ShapeDtypeStruct((n, D_MODEL), BF16),
        compiler_params=_params("parallel"),
        name="rms_in",
    )(x, g)


def _proj_kernel(h_ref, w_ref, *refs, mode, n_aux):
    aux, outs = refs[:n_aux], refs[n_aux:]
    z = _dot(h_ref[...], w_ref[...])
    if mode == "q":
        outs[0][...] = _seg_rms(z, aux[0][...], aux[1][...], Q_SCALE).astype(BF16)
    elif mode in ("kv", "kv_t"):
        k = _seg_rms(z[:, :512], aux[0][...], aux[1][...], 1.0)
        if mode == "kv_t":
            outs[0][0] = k.T
            tm = z.shape[0]
            for n in range(ATT_KV_HEADS):
                outs[1][pl.ds(n, tm, stride=ATT_KV_HEADS), :] = z[:, 512 + n * LANES:512 + (n + 1) * LANES]
        else:
            outs[0][...] = k
            outs[1][...] = z[:, 512:]
    elif mode == "hq":
        outs[0][...] = (z * (HG_DK ** -0.5)).astype(BF16)
    elif mode == "hf":
        lbp = aux[0][...]
        m = jnp.max(lbp, axis=0, keepdims=True)
        e = jnp.exp(lbp - m)
        lb = e[0:1] / jnp.sum(e, axis=0, keepdims=True)
        sig = jax.nn.sigmoid(z)
        f = lb + (1.0 - lb) * sig
        outs[0][...] = jnp.log(f)
        outs[1][...] = (1.0 - lb) * (1.0 - sig)
    elif mode == "hv":
        outs[0][...] = z.astype(BF16)
    elif mode == "silu":
        outs[0][...] = z * jax.nn.sigmoid(z)
    elif mode == "sigmoid":
        outs[0][...] = jax.nn.sigmoid(z)


def _proj(h, w, col, mode, aux, outs, tm, name):
    n = h.shape[0]
    aux_specs = [pl.BlockSpec(a.shape, lambda i: (0, 0)) for a in aux]
    res = pl.pallas_call(
        functools.partial(_proj_kernel, mode=mode, n_aux=len(aux)),
        grid=(n // tm,),
        in_specs=[pl.BlockSpec((tm, D_MODEL), lambda i: (i, 0)),
                  pl.BlockSpec((D_MODEL, 1024), lambda i, col=col: (0, col))] + aux_specs,
        out_specs=tuple(pl.BlockSpec(o[2], o[3]) for o in outs),
        out_shape=tuple(jax.ShapeDtypeStruct(o[0], o[1]) for o in outs),
        compiler_params=_params("parallel"),
        name=name,
    )(h, w, *aux)
    return res


def _inproj(x, gin, w, qn_row, kn_row, lbp, seg, tm, seq):
    n = x.shape[0]
    h = _rms_cast(x, gin, tm)
    tok = lambda width, dt: ((n, width), dt, (tm, width), lambda i: (i, 0))
    q, = _proj(h, w, 0, "q", [seg, qn_row], [tok(1024, BF16)], tm, "proj_q")
    if seq is None:
        k, v = _proj(h, w, 1, "kv", [seg, kn_row], [tok(512, F32), tok(512, F32)], tm, "proj_kv")
    else:
        b, t = seq
        tpb = t // tm
        k_out = ((b, 512, t), F32, (1, 512, tm), lambda i: (i // tpb, 0, i % tpb))
        v_out = ((n * ATT_KV_HEADS, LANES), F32, (tm * ATT_KV_HEADS, LANES), lambda i: (i, 0))
        k, v = _proj(h, w, 1, "kv_t", [seg, kn_row], [k_out, v_out], tm, "proj_kv")
    hq, = _proj(h, w, 2, "hq", [], [tok(1024, BF16)], tm, "proj_hq")
    logf, kk = _proj(h, w, 3, "hf", [lbp], [tok(1024, F32), tok(1024, F32)], tm, "proj_hf")
    hv, = _proj(h, w, 4, "hv", [], [tok(1024, BF16)], tm, "proj_hv")
    og, = _proj(h, w, 5, "silu", [], [tok(1024, F32)], tm, "proj_og")
    sga, = _proj(h, w, 6, "sigmoid", [], [tok(1024, F32)], tm, "proj_ga")
    sgh, = _proj(h, w, 7, "sigmoid", [], [tok(1024, F32)], tm, "proj_gh")
    return q, k, v, hq, logf, kk, hv, og, sga, sgh


def _diff_lambda(lp, lam_init):
    a = jnp.sum(lp[0:1] * lp[1:2], axis=-1, keepdims=True)
    b = jnp.sum(lp[2:3] * lp[3:4], axis=-1, keepdims=True)
    return jnp.exp(a) - jnp.exp(b) + lam_init


def _sub_norm(o, sub_row, lam_init):
    ms = jnp.mean(o * o, axis=-1, keepdims=True)
    return o * lax.rsqrt(ms + EPS) * sub_row * (1.0 - lam_init)


def _attn_kernel(q_ref, kt_ref, v_ref, lamp_ref, sub_ref, o_ref,
                 kb_scr, vb_scr, qs_scr, m_scr, acc_scr, *, tq, lam_init):
    qi = pl.program_id(2)

    @pl.when(qi == 0)
    def _():
        kb_scr[...] = kt_ref[0].astype(BF16)
        t_len = vb_scr.shape[0]
        vb_scr[:, 0:LANES] = v_ref[0, pl.ds(pl.program_id(1), t_len, stride=ATT_KV_HEADS), :].astype(BF16)
        vb_scr[:, LANES:2 * LANES] = jnp.ones((vb_scr.shape[0], LANES), BF16)

    q = q_ref[0].astype(F32)
    lane = lax.broadcasted_iota(jnp.int32, (tq, LANES), 1)
    for g in range(2):
        qg = q[:, g * LANES:(g + 1) * LANES]
        for c in range(2):
            keep = (lane < ATT_DHALF) if c == 0 else (lane >= ATT_DHALF)
            r = g * 2 + c
            qs_scr[r * tq:(r + 1) * tq, :] = jnp.where(keep, qg, 0.0).astype(BF16)
    def chunk(j, masked, first):
        off = pl.multiple_of(j * tq, tq)
        s = _dot(qs_scr[...], kb_scr[:, pl.ds(off, tq)])
        if masked:
            row = lax.rem(lax.broadcasted_iota(jnp.int32, s.shape, 0), tq)
            s = jnp.where(lax.broadcasted_iota(jnp.int32, s.shape, 1) <= row, s, -jnp.inf)
        s_max = jnp.max(s, axis=-1, keepdims=True)
        if first:
            m_new = jnp.broadcast_to(s_max, m_scr.shape)
        else:
            m_prev = m_scr[...]
            m_new = jnp.maximum(m_prev, s_max)
            alpha = jnp.exp2(m_prev - m_new)
        pr = jnp.exp2(s - jnp.concatenate([m_new] * (tq // LANES), axis=1))
        pv = _dot(pr.astype(BF16), vb_scr[pl.ds(off, tq), :])
        acc_scr[...] = pv if first else jnp.concatenate([alpha, alpha], axis=1) * acc_scr[...] + pv
        m_scr[...] = m_new

    def body(j, carry):
        chunk(j, False, False)
        return carry

    @pl.when(qi == 0)
    def _():
        chunk(0, True, True)

    @pl.when(qi > 0)
    def _():
        chunk(0, False, True)
        lax.fori_loop(1, qi, body, 0)
        chunk(qi, True, False)

    lam = _diff_lambda(lamp_ref[...], lam_init)
    acc = acc_scr[...]
    out = acc[:, 0:LANES] / acc[:, LANES:2 * LANES]
    for g in range(2):
        o0 = out[(2 * g) * tq:(2 * g + 1) * tq]
        o1 = out[(2 * g + 1) * tq:(2 * g + 2) * tq]
        o_ref[0, :, g * LANES:(g + 1) * LANES] = _sub_norm(o0 - lam * o1, sub_ref[...], lam_init).astype(BF16)


def _attn_prompt(q, kt, v, lam_p, sub_row, lam_init, tq):
    b, t, _ = q.shape
    return pl.pallas_call(
        functools.partial(_attn_kernel, tq=tq, lam_init=lam_init),
        grid=(b, ATT_KV_HEADS, t // tq),
        in_specs=[
            pl.BlockSpec((1, tq, 256), lambda bi, n, qi: (bi, qi, n)),
            pl.BlockSpec((1, LANES, t), lambda bi, n, qi: (bi, n, 0)),
            pl.BlockSpec((1, t * ATT_KV_HEADS, LANES), lambda bi, n, qi: (bi, 0, 0)),
            pl.BlockSpec((4, ATT_DHALF), lambda bi, n, qi: (0, 0)),
            pl.BlockSpec((1, LANES), lambda bi, n, qi: (0, 0)),
        ],
        out_specs=pl.BlockSpec((1, tq, 256), lambda bi, n, qi: (bi, qi, n)),
        out_shape=jax.ShapeDtypeStruct((b, t, 1024), BF16),
        scratch_shapes=[pltpu.VMEM((LANES, t), BF16), pltpu.VMEM((t, 2 * LANES), BF16),
                        pltpu.VMEM((4 * tq, LANES), BF16), pltpu.VMEM((4 * tq, LANES), F32),
                        pltpu.VMEM((4 * tq, 2 * LANES), F32)],
        compiler_params=_params("parallel", "parallel", "arbitrary"),
        name="attn_prompt",
    )(q, kt, v, lam_p, sub_row)


def _hgrn_tables(c, levels):
    t = np.arange(c)[:, None]
    u = np.arange(c)[None, :]
    w = [(u <= t), (u > t)]
    masks = [(t == u)]
    for l in range(1, levels + 1):
        n = 1 << l
        half = n // 2
        mid = (t // n) * n + half
        hi = t >= mid
        w.append(np.where(hi, (u > mid) & (u <= t), (u > t) & (u <= mid)))
        masks.append((t // n == u // n) & (t % n >= half) & (u % n < half))
    w = np.concatenate(w, axis=0).astype(np.float32)
    w3 = np.concatenate([w, w, w], axis=1)
    if len(masks) % 2:
        masks.append(np.zeros_like(masks[0]))
    pairs = [np.concatenate([masks[i], masks[i + 1]], axis=1) for i in range(0, len(masks), 2)]
    return w3, np.stack(pairs).astype(np.float32)


def _hgrn_kernel(q_ref, g_ref, k_ref, v_ref, og_ref, norm_ref, w_ref, mask_ref, o_ref, s_ref,
                 st_scr, *, c, levels, n_chunks):
    ti = pl.program_id(2)

    @pl.when(ti == 0)
    def _():
        st_scr[...] = jnp.zeros(st_scr.shape, F32)

    w = w_ref[...]
    zeros = jnp.zeros((c, LANES), BF16)
    chunk_rows = [slice(ci * c, (ci + 1) * c) for ci in range(n_chunks)]
    g3 = jnp.concatenate([jnp.concatenate(_split3(g_ref[0, rows, :]), axis=0) for rows in chunk_rows], axis=1)
    e_all = jnp.exp(_dot(w, g3))

    parts = []
    for ci, rows in enumerate(chunk_rows):
        q = q_ref[0, rows, :].astype(F32)
        k = k_ref[0, rows, :]
        v = v_ref[0, rows, :]
        e = e_all[:, ci * LANES:(ci + 1) * LANES]
        e_b = e[0:c]
        e_k = e[c:2 * c]
        qs = [q.astype(BF16)]
        ks = [k.astype(BF16)]
        for l in range(1, levels + 1):
            e_l = e[(l + 1) * c:(l + 2) * c]
            qs.append((q * e_l).astype(BF16))
            ks.append((k * e_l).astype(BF16))
        if len(qs) % 2:
            qs.append(zeros)
            ks.append(zeros)
        a2 = None
        for p in range(len(qs) // 2):
            ql = jnp.concatenate([qs[2 * p], qs[2 * p + 1]], axis=1)
            kbd = jnp.concatenate([jnp.concatenate([ks[2 * p], zeros], axis=1),
                                   jnp.concatenate([zeros, ks[2 * p + 1]], axis=1)], axis=0)
            term = mask_ref[p] * _dot_nt(ql, kbd)
            a2 = term if a2 is None else a2 + term
        o_intra = _dot(a2.astype(BF16), jnp.concatenate([v, v], axis=0))
        st_add = _dot_tn(v, (k * e_k).astype(BF16))
        parts.append(((q * e_b).astype(BF16), e_b[c - 1:c, :], o_intra, st_add))

    st = st_scr[...]
    for rows, (q_b, decay, o_intra, st_add) in zip(chunk_rows, parts):
        o = _dot_nt(q_b, st.astype(BF16)) + o_intra
        st = st * decay + st_add
        ms = jnp.mean(o * o, axis=-1, keepdims=True)
        y = o * lax.rsqrt(ms + EPS) * norm_ref[...] * og_ref[0, rows, :]
        o_ref[0, rows, :] = y.astype(BF16)
    st_scr[...] = st

    @pl.when(ti == pl.num_programs(2) - 1)
    def _():
        s_ref[0, 0] = st.T


def _hgrn_prompt(hq, logf, kk, hv, og, norm_row, ct):
    b, t, _ = hq.shape
    c, levels = HG_CHUNK, HG_LEVELS
    w_np, m_np = _hgrn_tables(c, levels)
    w = jnp.asarray(w_np, BF16)
    masks = jnp.asarray(m_np, F32)
    tok = pl.BlockSpec((1, ct, LANES), lambda bi, h, ti: (bi, ti, h))
    return pl.pallas_call(
        functools.partial(_hgrn_kernel, c=c, levels=levels, n_chunks=ct // c),
        grid=(b, HG_HEADS, t // ct),
        in_specs=[tok, tok, tok, tok, tok,
                  pl.BlockSpec((1, LANES), lambda bi, h, ti: (0, 0)),
                  pl.BlockSpec(w.shape, lambda bi, h, ti: (0, 0)),
                  pl.BlockSpec(masks.shape, lambda bi, h, ti: (0, 0, 0))],
        out_specs=(tok, pl.BlockSpec((1, 1, HG_DK, HG_DV), lambda bi, h, ti: (bi, h, 0, 0))),
        out_shape=(jax.ShapeDtypeStruct((b, t, 1024), BF16),
                   jax.ShapeDtypeStruct((b, HG_HEADS, HG_DK, HG_DV), F32)),
        scratch_shapes=[pltpu.VMEM((HG_DV, HG_DK), F32)],
        compiler_params=_params("parallel", "parallel", "arbitrary"),
        name="hgrn_prompt",
    )(hq, logf, kk, hv, og, norm_row, w, masks)


def _dec_attn_kernel(pt_ref, wq_ref, kn_ref, vn_ref, lamp_ref, sub_ref, *refs, pages, lam_init):
    k_refs = refs[:pages]
    v_refs = refs[pages:2 * pages]
    o_ref = refs[2 * pages]
    m_scr, l_scr, acc_scr = refs[2 * pages + 1:]
    j = pl.program_id(1)
    page = k_refs[0].shape[2]

    @pl.when(j == 0)
    def _():
        m_scr[...] = jnp.full(m_scr.shape, -jnp.inf, F32)
        l_scr[...] = jnp.zeros(l_scr.shape, F32)
        acc_scr[...] = jnp.zeros(acc_scr.shape, F32)

    rown = lax.rem(lax.broadcasted_iota(jnp.int32, (16, LANES), 0), 8) // 2
    wq = wq_ref[0]
    s = jnp.concatenate([_dot(wq, k_refs[i][0].astype(BF16)) for i in range(pages)], axis=1)
    m_prev = m_scr[...]
    m_new = jnp.maximum(m_prev, jnp.max(s, axis=-1, keepdims=True))
    alpha = jnp.exp2(m_prev - m_new)
    pr = jnp.exp2(s - m_new)
    l_scr[...] = alpha * l_scr[...] + jnp.sum(pr, axis=-1, keepdims=True)
    prb = pr.astype(BF16)
    pv = jnp.zeros((16, LANES), F32)
    for n in range(ATT_KV_HEADS):
        v_n = jnp.concatenate([v_refs[i][0, pl.ds(n, page, stride=ATT_KV_HEADS), :] for i in range(pages)],
                              axis=0).astype(BF16)
        pv = pv + jnp.where(rown == n, _dot(prb, v_n), 0.0)
    acc_scr[...] = alpha * acc_scr[...] + pv
    m_scr[...] = m_new

    @pl.when(j == pl.num_programs(1) - 1)
    def _():
        s_cur = jnp.sum(wq.astype(F32) * kn_ref[0], axis=-1, keepdims=True)
        m_old = m_scr[...]
        m_fin = jnp.maximum(m_old, s_cur)
        al = jnp.exp2(m_old - m_fin)
        pc = jnp.exp2(s_cur - m_fin)
        l_fin = al * l_scr[...] + pc
        vn = vn_ref[0]
        v_cur = jnp.zeros((16, LANES), F32)
        for n in range(ATT_KV_HEADS):
            v_cur = v_cur + jnp.where(rown == n, vn[:, n * LANES:(n + 1) * LANES], 0.0)
        out = (al * acc_scr[...] + pc * v_cur) / l_fin
        lam = _diff_lambda(lamp_ref[...], lam_init)
        o = out[0:8] - lam * out[8:16]
        o_ref[0] = _sub_norm(o, sub_ref[...], lam_init)


def _attn_sample(wq, k_new, v_new, kt_pages, v_pages, page_table, lam_p, sub_row, lam_init, pages):
    db, n_pages = page_table.shape
    pt_flat = page_table.reshape(-1)
    blk = kt_pages.shape[1:]

    def page_spec(i):
        return pl.BlockSpec((1,) + blk, lambda bi, j, pt, i=i: (pt[bi * n_pages + j * pages + i], 0, 0))

    row3 = lambda w: pl.BlockSpec((1, 1, w), lambda bi, j, pt: (bi, 0, 0))
    grid_spec = pltpu.PrefetchScalarGridSpec(
        num_scalar_prefetch=1,
        grid=(db, n_pages // pages),
        in_specs=[pl.BlockSpec((1, 16, 512), lambda bi, j, pt: (bi, 0, 0)), row3(512), row3(512),
                  pl.BlockSpec((4, ATT_DHALF), lambda bi, j, pt: (0, 0)),
                  pl.BlockSpec((1, LANES), lambda bi, j, pt: (0, 0))]
                 + [page_spec(i) for i in range(pages)] + [page_spec(i) for i in range(pages)],
        out_specs=pl.BlockSpec((1, 8, LANES), lambda bi, j, pt: (bi, 0, 0)),
        scratch_shapes=[pltpu.VMEM((16, 1), F32), pltpu.VMEM((16, 1), F32), pltpu.VMEM((16, LANES), F32)],
    )
    return pl.pallas_call(
        functools.partial(_dec_attn_kernel, pages=pages, lam_init=lam_init),
        grid_spec=grid_spec,
        out_shape=jax.ShapeDtypeStruct((db, 8, LANES), F32),
        compiler_params=_params("parallel", "arbitrary"),
        name="attn_sample",
    )(pt_flat, wq, k_new, v_new, lam_p, sub_row, *([kt_pages] * pages), *([v_pages] * pages))


def _hgrn_step_kernel(s_ref, q_ref, g_ref, k_ref, v_ref, og_ref, norm_ref, o_ref, sn_ref):
    pad = jnp.zeros((LANES - 3 * HG_HEADS, LANES), F32)
    for bi in range(s_ref.shape[0]):
        cols = jnp.concatenate([q_ref[bi], g_ref[bi], k_ref[bi], pad], axis=0).T
        v_all = v_ref[bi]
        og_all = og_ref[bi]
        outs = []
        for h in range(HG_HEADS):
            s = s_ref[bi, h]
            qc = cols[:, h:h + 1]
            f = jnp.exp(cols[:, HG_HEADS + h:HG_HEADS + h + 1])
            kc = cols[:, 2 * HG_HEADS + h:2 * HG_HEADS + h + 1]
            v = v_all[h:h + 1]
            o = jnp.sum(s * (qc * f), axis=0, keepdims=True) + jnp.sum(qc * kc, axis=0, keepdims=True) * v
            sn_ref[bi, h] = s * f + kc * v
            ms = jnp.mean(o * o, axis=-1, keepdims=True)
            outs.append(o * lax.rsqrt(ms + EPS) * norm_ref[...] * og_all[h:h + 1])
        o_ref[bi] = jnp.concatenate(outs, axis=0)


def _hgrn_sample(state, q, g, k, v, og, norm_row):
    db = state.shape[0]
    nb = _pick(db, 4)
    row = pl.BlockSpec((nb, HG_HEADS, HG_DK), lambda bi: (bi, 0, 0))
    st = pl.BlockSpec((nb, HG_HEADS, HG_DK, HG_DV), lambda bi: (bi, 0, 0, 0))
    return pl.pallas_call(
        _hgrn_step_kernel,
        grid=(db // nb,),
        in_specs=[st, row, row, row, row, row, pl.BlockSpec((1, LANES), lambda bi: (0, 0))],
        out_specs=(row, st),
        out_shape=(jax.ShapeDtypeStruct((db, HG_HEADS, HG_DV), F32),
                   jax.ShapeDtypeStruct(state.shape, F32)),
        compiler_params=_params("parallel"),
        name="hgrn_sample",
    )(state, q, g, k, v, og, norm_row)


def _lane_first(cond, lane):
    return jnp.min(jnp.where(cond, lane, LANES), axis=-1, keepdims=True)


def _route(logits):
    lane = lax.broadcasted_iota(jnp.int32, logits.shape, 1)
    gmask = (lane >= N_EXPERTS) & (lane < N_EXPERTS + N_GROUPS)
    lg = jnp.where(gmask, logits, -jnp.inf)
    mg = jnp.max(lg, axis=-1, keepdims=True)
    eg = jnp.exp(lg - mg)
    pg = eg / jnp.sum(eg, axis=-1, keepdims=True)
    pg_top = jnp.max(pg, axis=-1, keepdims=True)
    gsel = _lane_first(gmask & (pg == pg_top), lane) - N_EXPERTS
    emask = (lane >= gsel * EXP_PER_GROUP) & (lane < (gsel + 1) * EXP_PER_GROUP)
    le = jnp.where(emask, logits, -jnp.inf)
    me = jnp.max(le, axis=-1, keepdims=True)
    ee = jnp.exp(le - me)
    pe = ee / jnp.sum(ee, axis=-1, keepdims=True)
    pe = jnp.where(emask, pe, -1.0)
    p1 = jnp.max(pe, axis=-1, keepdims=True)
    e1 = _lane_first(pe == p1, lane)
    pe2 = jnp.where(lane == e1, -1.0, pe)
    p2 = jnp.max(pe2, axis=-1, keepdims=True)
    e2 = _lane_first(pe2 == p2, lane)
    den = p1 + p2
    w1 = p1 / den * pg_top
    w2 = p2 / den * pg_top
    comb = jnp.where(lane == e1, w1, 0.0) + jnp.where(lane == e2, w2, 0.0)
    picks = jnp.where(lane == 0, e1.astype(F32), jnp.where(lane == 1, e2.astype(F32),
                      jnp.where(lane == 2, w1, jnp.where(lane == 3, w2, 0.0))))
    return comb, picks


ROW_TILES = D_MODEL // LANES


def _merge_kernel(oa_ref, oh_ref, sga_ref, sgh_ref, x_ref, wpa_ref, wph_ref, wo_ref, gf_ref,
                  wr_ref, br_ref, x1_ref, xn_ref, route_ref, *, sparse):
    ya = _dot(oa_ref[...], wpa_ref[...])
    yh = _dot(oh_ref[...], wph_ref[...])
    mixed = (sga_ref[...] * ya + sgh_ref[...] * yh).astype(BF16)
    x1 = x_ref[...] + _dot(mixed, wo_ref[...])
    x1_ref[...] = x1
    ms = jnp.mean(x1 * x1, axis=-1, keepdims=True)
    xn_f = x1 * lax.rsqrt(ms + EPS) * gf_ref[...]
    xn = xn_f.astype(BF16)
    comb, picks = _route(_dot(xn, wr_ref[...]) + br_ref[...])
    if sparse:
        tm = x1.shape[0]
        for s in range(ROW_TILES):
            xn_ref[pl.ds(s, tm, stride=ROW_TILES), :] = xn_f[:, s * LANES:(s + 1) * LANES]
        route_ref[...] = picks
    else:
        xn_ref[...] = xn
        route_ref[...] = comb


def _merge(oa, oh, sga, sgh, x, wpa, wph, wo, gf, wr, br, tm, sparse):
    n = x.shape[0]
    tok = pl.BlockSpec((tm, 1024), lambda i: (i, 0))
    wsp = pl.BlockSpec((1024, 1024), lambda i: (0, 0))
    if sparse:
        xn_spec = pl.BlockSpec((tm * ROW_TILES, LANES), lambda i: (i, 0))
        xn_shape = jax.ShapeDtypeStruct((n * ROW_TILES, LANES), F32)
    else:
        xn_spec, xn_shape = tok, jax.ShapeDtypeStruct((n, 1024), BF16)
    return pl.pallas_call(
        functools.partial(_merge_kernel, sparse=sparse),
        grid=(n // tm,),
        in_specs=[tok, tok, tok, tok, tok, wsp, wsp, wsp,
                  pl.BlockSpec((1, 1024), lambda i: (0, 0)),
                  pl.BlockSpec((1024, LANES), lambda i: (0, 0)),
                  pl.BlockSpec((1, LANES), lambda i: (0, 0))],
        out_specs=(tok, xn_spec, pl.BlockSpec((tm, LANES), lambda i: (i, 0))),
        out_shape=(jax.ShapeDtypeStruct((n, 1024), F32), xn_shape,
                   jax.ShapeDtypeStruct((n, LANES), F32)),
        compiler_params=_params("parallel"),
        name="merge",
    )(oa, oh, sga, sgh, x, wpa, wph, wo, gf, wr, br)


def _moe_kernel(xn_ref, comb_ref, x1_ref, wg_ref, wu_ref, wd_ref, y_ref, acc_scr):
    e = pl.program_id(1)

    @pl.when(e == 0)
    def _():
        acc_scr[...] = jnp.zeros(acc_scr.shape, F32)

    comb = comb_ref[...]
    lane = lax.broadcasted_iota(jnp.int32, comb.shape, 1)
    c = jnp.sum(jnp.where(lane == e, comb, 0.0), axis=-1, keepdims=True)
    xn = xn_ref[...]
    hg = _dot(xn, wg_ref[0])
    hu = _dot(xn, wu_ref[0])
    hid = hg * jax.nn.sigmoid(hg) * hu
    acc_scr[...] += _dot((hid * c).astype(BF16), wd_ref[0])

    @pl.when(e == pl.num_programs(1) - 1)
    def _():
        y_ref[...] = x1_ref[...] + acc_scr[...]


def _moe(xn, comb, x1, wg, wu, wd, tm):
    n = xn.shape[0]
    return pl.pallas_call(
        _moe_kernel,
        grid=(n // tm, N_EXPERTS),
        in_specs=[pl.BlockSpec((tm, 1024), lambda i, e: (i, 0)),
                  pl.BlockSpec((tm, LANES), lambda i, e: (i, 0)),
                  pl.BlockSpec((tm, 1024), lambda i, e: (i, 0)),
                  pl.BlockSpec((1, 1024, D_FF), lambda i, e: (e, 0, 0)),
                  pl.BlockSpec((1, 1024, D_FF), lambda i, e: (e, 0, 0)),
                  pl.BlockSpec((1, D_FF, 1024), lambda i, e: (e, 0, 0))],
        out_specs=pl.BlockSpec((tm, 1024), lambda i, e: (i, 0)),
        out_shape=jax.ShapeDtypeStruct((n, 1024), F32),
        scratch_shapes=[pltpu.VMEM((tm, 1024), F32)],
        compiler_params=_params("parallel", "arbitrary"),
        name="moe",
    )(xn, comb, x1, wg, wu, wd)


def _row_copy(src_hbm, src_row, dst, dst_row, sem):
    return pltpu.make_async_copy(src_hbm.at[pl.ds(src_row * ROW_TILES, ROW_TILES)],
                                 dst.at[pl.ds(dst_row * ROW_TILES, ROW_TILES)], sem)


def _wait_rows(src_hbm, dst, sem):
    pltpu.make_async_copy(src_hbm.at[pl.ds(0, dst.shape[0])], dst, sem).wait()


def _rows_to_mat(buf, n_rows):
    return jnp.concatenate([buf[pl.ds(s, n_rows, stride=ROW_TILES), :] for s in range(ROW_TILES)], axis=1)


def _moe_scatter_kernel(p1_ref, p2_ref, xn_ref, xs_in, xs_hbm, sem):
    del xs_in
    i = pl.program_id(0)
    tm = p1_ref.shape[0]

    def wait_tile():
        for _ in range(2):
            pltpu.make_async_copy(xn_ref, xs_hbm.at[pl.ds(0, tm * ROW_TILES)], sem.at[0]).wait()

    def body(r, carry):
        src = xn_ref.at[pl.ds(r * ROW_TILES, ROW_TILES)]
        pltpu.make_async_copy(src, xs_hbm.at[pl.ds(p1_ref[r] * ROW_TILES, ROW_TILES)], sem.at[0]).start(priority=0)
        pltpu.make_async_copy(src, xs_hbm.at[pl.ds(p2_ref[r] * ROW_TILES, ROW_TILES)], sem.at[0]).start(priority=1)
        return carry

    lax.fori_loop(0, tm, body, 0)
    wait_tile()


def _moe_scatter(xg, pos1, pos2, rows, tm):
    n = pos1.shape[0]
    idx = pl.BlockSpec((tm,), lambda i: (i,), memory_space=pltpu.SMEM)
    xs0 = jnp.zeros((rows * ROW_TILES, LANES), F32)
    return pl.pallas_call(
        _moe_scatter_kernel,
        grid=(n // tm,),
        in_specs=[idx, idx, pl.BlockSpec((tm * ROW_TILES, LANES), lambda i: (i, 0)),
                  pl.BlockSpec(memory_space=pl.ANY)],
        out_specs=pl.BlockSpec(memory_space=pl.ANY),
        out_shape=jax.ShapeDtypeStruct(xs0.shape, F32),
        scratch_shapes=[pltpu.SemaphoreType.DMA((1,))],
        input_output_aliases={3: 0},
        compiler_params=_params("arbitrary"),
        name="moe_scatter",
    )(pos1, pos2, xg, xs0)


def _moe_ffn_kernel(te_ref, xs_ref, wg_ref, wu_ref, wd_ref, ys_ref, *, tp):
    del te_ref
    x = _rows_to_mat(xs_ref, tp).astype(BF16)
    hg = _dot(x, wg_ref[0])
    hu = _dot(x, wu_ref[0])
    hid = (hg * jax.nn.sigmoid(hg) * hu).astype(BF16)
    out = _dot(hid, wd_ref[0])
    for s in range(ROW_TILES):
        ys_ref[pl.ds(s, tp, stride=ROW_TILES), :] = out[:, s * LANES:(s + 1) * LANES]


def _moe_ffn(xs, tile_expert, wg, wu, wd, tp):
    n_tiles = tile_expert.shape[0]
    rows = pl.BlockSpec((tp * ROW_TILES, LANES), lambda i, te: (i, 0))
    wsp = lambda a, b: pl.BlockSpec((1, a, b), lambda i, te: (te[i], 0, 0))
    grid_spec = pltpu.PrefetchScalarGridSpec(
        num_scalar_prefetch=1,
        grid=(n_tiles,),
        in_specs=[rows, wsp(1024, D_FF), wsp(1024, D_FF), wsp(D_FF, 1024)],
        out_specs=rows,
    )
    return pl.pallas_call(
        functools.partial(_moe_ffn_kernel, tp=tp),
        grid_spec=grid_spec,
        out_shape=jax.ShapeDtypeStruct(xs.shape, F32),
        compiler_params=_params("parallel"),
        name="moe_ffn",
    )(tile_expert, xs, wg, wu, wd)


def _moe_combine_kernel(p1_ref, p1n_ref, p2_ref, p2n_ref, ys_hbm, x1_ref, w1_ref, w2_ref, y_ref, buf, sem, *, tm):
    i = pl.program_id(0)
    slot = lax.rem(i, 2)

    def gather(a_ref, b_ref, s):
        def body(r, carry):
            _row_copy(ys_hbm, a_ref[r], buf.at[s, 0], r, sem.at[s, 0]).start(priority=0)
            _row_copy(ys_hbm, b_ref[r], buf.at[s, 1], r, sem.at[s, 1]).start(priority=1)
            return carry
        lax.fori_loop(0, tm, body, 0)

    @pl.when(i == 0)
    def _():
        gather(p1_ref, p2_ref, 0)

    @pl.when(i + 1 < pl.num_programs(0))
    def _():
        gather(p1n_ref, p2n_ref, 1 - slot)

    _wait_rows(ys_hbm, buf.at[slot, 0], sem.at[slot, 0])
    _wait_rows(ys_hbm, buf.at[slot, 1], sem.at[slot, 1])
    y_ref[...] = x1_ref[...] + (w1_ref[...] * _rows_to_mat(buf.at[slot, 0], tm)
                                + w2_ref[...] * _rows_to_mat(buf.at[slot, 1], tm))


def _moe_combine(ys, pos1, pos2, w1, w2, x1, tm):
    n = x1.shape[0]
    n_tiles = n // tm
    idx = lambda off: pl.BlockSpec((tm,), lambda i: (jnp.minimum(i + off, n_tiles - 1),),
                                   memory_space=pltpu.SMEM)
    col = pl.BlockSpec((tm, 1), lambda i: (i, 0))
    return pl.pallas_call(
        functools.partial(_moe_combine_kernel, tm=tm),
        grid=(n_tiles,),
        in_specs=[idx(0), idx(1), idx(0), idx(1), pl.BlockSpec(memory_space=pl.ANY),
                  pl.BlockSpec((tm, 1024), lambda i: (i, 0)), col, col],
        out_specs=pl.BlockSpec((tm, 1024), lambda i: (i, 0)),
        out_shape=jax.ShapeDtypeStruct((n, 1024), F32),
        scratch_shapes=[pltpu.VMEM((2, 2, tm * ROW_TILES, LANES), F32), pltpu.SemaphoreType.DMA((2, 2))],
        compiler_params=_params("arbitrary"),
        name="moe_combine",
    )(pos1, pos1, pos2, pos2, ys, x1, w1, w2)


def _rank_kernel(picks_ref, tri_ref, rank_ref, cnt_ref, base_scr):
    @pl.when(pl.program_id(0) == 0)
    def _():
        base_scr[...] = jnp.zeros(base_scr.shape, F32)

    picks = picks_ref[...]
    lane = lax.broadcasted_iota(jnp.int32, picks.shape, 1)
    chosen = (lane == picks[:, 0:1].astype(jnp.int32)) | (lane == picks[:, 1:2].astype(jnp.int32))
    both = jnp.where(chosen, 1.0, 0.0)
    before = _dot(tri_ref[...], both.astype(BF16)) + base_scr[...]
    rank_ref[...] = jnp.where(chosen, before, 0.0)
    base_scr[...] += jnp.sum(both, axis=0, keepdims=True)
    cnt_ref[...] = base_scr[...]


def _pos_kernel(picks_ref, rank_ref, start_ref, pos_ref):
    picks = picks_ref[...]
    lane = lax.broadcasted_iota(jnp.int32, picks.shape, 1)
    where_to = rank_ref[...] + start_ref[...]
    pos = [jnp.sum(jnp.where(lane == picks[:, p:p + 1].astype(jnp.int32), where_to, 0.0), axis=-1, keepdims=True)
           for p in range(2)]
    pos_ref[...] = jnp.where(lane == 0, pos[0], jnp.where(lane == 1, pos[1], 0.0))


def _dispatch(picks, tp, tm):
    n = picks.shape[0]
    tri = jnp.asarray(np.tril(np.ones((tm, tm), np.float32), -1), BF16)
    tok = pl.BlockSpec((tm, LANES), lambda i: (i, 0))
    one = pl.BlockSpec((1, LANES), lambda i: (0, 0))
    rank, cnt = pl.pallas_call(
        _rank_kernel,
        grid=(n // tm,),
        in_specs=[tok, pl.BlockSpec((tm, tm), lambda i: (0, 0))],
        out_specs=(tok, one),
        out_shape=(jax.ShapeDtypeStruct((n, LANES), F32), jax.ShapeDtypeStruct((1, LANES), F32)),
        scratch_shapes=[pltpu.VMEM((1, LANES), F32)],
        compiler_params=_params("arbitrary"),
        name="moe_rank",
    )(picks, tri)
    counts = cnt[0, :N_EXPERTS].astype(jnp.int32)
    tiles = (counts + tp - 1) // tp
    tile_end = jnp.cumsum(tiles)
    row_start = jnp.zeros((1, LANES), F32).at[0, :N_EXPERTS].set(((tile_end - tiles) * tp).astype(F32))
    tp_pos = _pick(n, 4 * tm)
    tok_pos = pl.BlockSpec((tp_pos, LANES), lambda i: (i, 0))
    pos = pl.pallas_call(
        _pos_kernel,
        grid=(n // tp_pos,),
        in_specs=[tok_pos, tok_pos, one],
        out_specs=tok_pos,
        out_shape=jax.ShapeDtypeStruct((n, LANES), F32),
        compiler_params=_params("parallel"),
        name="moe_pos",
    )(picks, rank, row_start)
    n_tiles = (2 * n) // tp + N_EXPERTS
    tile_ids = jnp.arange(n_tiles, dtype=jnp.int32)
    tile_expert = jnp.minimum(jnp.sum((tile_end[None, :] <= tile_ids[:, None]).astype(jnp.int32), axis=1),
                              N_EXPERTS - 1)
    return pos[:, 0].astype(jnp.int32), pos[:, 1].astype(jnp.int32), tile_expert, n_tiles * tp


def _pick(n, pref):
    t = min(n, pref)
    while n % t:
        t //= 2
    return t


def kernel(x_prompt, x_sample, cache_k, cache_v, state_hgrn, page_table, rms_in, w_in, att_q_norm, att_k_norm, att_lambda, att_sub_norm, hg_lower_bound, hg_out_norm, w_branch_att, w_branch_hg, w_out, rms_ffn, w_router_group, b_router_group, w_router_expert, b_router_expert, w_exp_gate, w_exp_up, w_exp_down):
    depth = rms_in.shape[0]
    assert depth == 1 and hg_lower_bound.shape[0] == 2
    b, t, _ = x_prompt.shape
    db, ds, _ = x_sample.shape
    assert ds == 1
    lam_init = 0.8 - 0.6 * math.exp(-0.3 * 0)

    w_in_bf = w_in[0].astype(BF16)
    qn_row = jnp.tile(att_q_norm[0], 2).reshape(1, LANES)
    kn_row = jnp.tile(att_k_norm[0], 2).reshape(1, LANES)
    seg_np = (np.arange(LANES)[:, None] // ATT_DHALF) == (np.arange(LANES)[None, :] // ATT_DHALF)
    seg = jnp.asarray(seg_np.astype(np.float32), BF16)
    sub_row = att_sub_norm[0].reshape(1, LANES)
    hgn_row = hg_out_norm[0].reshape(1, LANES)
    wpa = w_branch_att[0].astype(BF16)
    wph = w_branch_hg[0].astype(BF16)
    wo = w_out[0].astype(BF16)
    wr = jnp.zeros((D_MODEL, LANES), F32)
    wr = wr.at[:, :N_EXPERTS].set(w_router_expert[0]).at[:, N_EXPERTS:N_EXPERTS + N_GROUPS].set(w_router_group[0])
    wr = wr.astype(BF16)
    br = jnp.zeros((1, LANES), F32)
    br = br.at[0, :N_EXPERTS].set(b_router_expert[0]).at[0, N_EXPERTS:N_EXPERTS + N_GROUPS].set(b_router_group[0])
    weg = w_exp_gate[0].astype(BF16)
    weu = w_exp_up[0].astype(BF16)
    wed = w_exp_down[0].astype(BF16)
    gin = rms_in[0].reshape(1, D_MODEL)
    gffn = rms_ffn[0].reshape(1, D_MODEL)
    lam_p = att_lambda[0]

    def tail(x2, oa, oh, sga, sgh, tm_merge, tm_moe):
        sparse = x2.shape[0] >= 16 * MOE_TILE
        x1, xn, route = _merge(oa, oh, sga, sgh, x2, wpa, wph, wo, gffn, wr, br, tm_merge, sparse)
        if not sparse:
            return _moe(xn, route, x1, weg, weu, wed, tm_moe)
        pos1, pos2, tile_expert, rows = _dispatch(route, MOE_TILE, _pick(x2.shape[0], 1024))
        xs = _moe_scatter(xn, pos1, pos2, rows, _pick(x2.shape[0], 512))
        ys = _moe_ffn(xs, tile_expert, weg, weu, wed, MOE_TILE)
        return _moe_combine(ys, pos1, pos2, route[:, 2:3], route[:, 3:4], x1, _pick(x2.shape[0], 512))

    n = b * t
    xp = x_prompt.reshape(n, D_MODEL)
    q, kt, v, hq, logf, kk, hv, og, sga, sgh = _inproj(
        xp, gin, w_in_bf, qn_row, kn_row, hg_lower_bound, seg, _pick(t, 2048), (b, t))
    oa = _attn_prompt(q.reshape(b, t, 1024), kt, v.reshape(b, t * ATT_KV_HEADS, LANES), lam_p, sub_row, lam_init,
                      _pick(t, 512))
    r3 = lambda a: a.reshape(b, t, 1024)
    oh, state_p = _hgrn_prompt(r3(hq), r3(logf), r3(kk), r3(hv), r3(og), hgn_row, _pick(t, 2048))
    yp = tail(xp, oa.reshape(n, 1024), oh.reshape(n, 1024), sga, sgh, _pick(n, 512), _pick(n, 1024))
    k_prompt = jnp.transpose(kt.reshape(1, b, ATT_KV_HEADS, 2, ATT_DHALF, t), (0, 1, 5, 2, 3, 4))

    xs = x_sample.reshape(db, D_MODEL)
    q, ks, vs, hq, logf, kk, hv, og, sga, sgh = _inproj(
        xs, gin, w_in_bf, qn_row, kn_row, hg_lower_bound, seg, _pick(db, 1024), None)
    q5 = q.reshape(db, ATT_KV_HEADS, 2, 2, ATT_DHALF)
    eye_n = jnp.eye(ATT_KV_HEADS, dtype=BF16)
    eye_c = jnp.eye(2, dtype=BF16)
    wq = jnp.einsum('bngcd,nm,ce->bcngmed', q5, eye_n, eye_c).reshape(db, 16, 512)
    n_pool = cache_k.shape[1]
    page = cache_k.shape[2]
    kt_pages = jnp.transpose(cache_k[0], (0, 2, 3, 4, 1)).reshape(n_pool, 512, page)
    v_pages = cache_v[0].reshape(n_pool, page * ATT_KV_HEADS, ATT_DV)
    oa_s = _attn_sample(wq, ks.reshape(db, 1, 512), vs.reshape(db, 1, 512), kt_pages, v_pages,
                        page_table, lam_p, sub_row, lam_init, _pick(page_table.shape[1], 32))
    row = lambda a: a.astype(F32).reshape(db, HG_HEADS, HG_DK)
    oh_s, state_s = _hgrn_sample(state_hgrn[0], row(hq), row(logf), row(kk), row(hv), row(og), hgn_row)
    ys = tail(xs, oa_s.reshape(db, 1024).astype(BF16), oh_s.reshape(db, 1024).astype(BF16), sga, sgh,
              _pick(db, 512), _pick(db, 1024))

    return (yp.reshape(b, t, D_MODEL), ys.reshape(db, 1, D_MODEL),
            k_prompt, v.reshape(1, b, t, ATT_KV_HEADS, ATT_DV),
            state_p.reshape(1, b, HG_HEADS, HG_DK, HG_DV),
            ks.reshape(1, db, 1, ATT_KV_HEADS, 2, ATT_DHALF), vs.reshape(1, db, 1, ATT_KV_HEADS, ATT_DV),
            state_s.reshape(1, db, HG_HEADS, HG_DK, HG_DV))
```
